```python
import math, functools
import jax, jax.numpy as jnp
from jax import lax
import numpy as np

D_MODEL = 1024
BATCH = 8
SEQ = 2048
DEPTH = 4

GRID_W = 64
CTX_LEN = 256
D_MIX = D_MODEL
CF_W = D_MIX // 2
CF_CONV = 31
GDN_HEADS = 4
GDN_DK = 128
GDN_DV = 128
GDN_QK = GDN_HEADS * GDN_DK
GDN_W = GDN_HEADS * GDN_DV
GDN_CONV_W = 2 * GDN_QK + GDN_W
GDN_CONV = 3
GDN_CHUNK = 64
S5_W = D_MIX // 2
S5_P = 16
S5_G = S5_W // S5_P
S5_N = 64
SC_W = D_MIX // 2
SC_CONV = 3
FFN_F = 2816
FFN_CONV = 3
N_EVEN = (DEPTH + 1) // 2
N_ODD = DEPTH // 2
EV_IN = 2 * CF_W + GDN_CONV_W + GDN_W + 4 * GDN_HEADS
OD_IN = S5_W + 3 * SC_W
DN_ALPHA = (2 * DEPTH) ** 0.25
DN_BETA = (8 * DEPTH) ** -0.25
LN_EPS = 1e-5
RMS_EPS = 1e-6
DT_MIN = 1e-3
DT_MAX = 1e-1

kernel_name = "hybrid_dit_conformer_gdn_s5_shortconv"


def layer_norm(x, g, b):
    xf = x.astype(jnp.float32)
    mu = jnp.mean(xf, -1, keepdims=True)
    var = jnp.mean(jnp.square(xf - mu), -1, keepdims=True)
    return ((xf - mu) * lax.rsqrt(var + LN_EPS)).astype(x.dtype) * g + b


def dwconv_seq(x, w):
    k = w.shape[0]
    return lax.conv_general_dilated(
        x, w[:, None, :].astype(x.dtype), window_strides=(1,), padding=[(k // 2, k // 2)],
        dimension_numbers=("NWC", "WIO", "NWC"), feature_group_count=x.shape[-1])


def dwconv_tokens(x, w, rows, axis):
    if rows is None:
        return dwconv_seq(x, w)
    bsz, l, ch = x.shape
    grid = x.reshape(bsz, rows, GRID_W, ch)
    if axis == "row":
        return dwconv_seq(grid.reshape(bsz * rows, GRID_W, ch), w).reshape(bsz, l, ch)
    cols = grid.transpose(0, 2, 1, 3).reshape(bsz * GRID_W, rows, ch)
    y = dwconv_seq(cols, w).reshape(bsz, GRID_W, rows, ch)
    return y.transpose(0, 2, 1, 3).reshape(bsz, l, ch)


def l2norm(t):
    return t * lax.rsqrt(jnp.sum(t * t, -1, keepdims=True) + RMS_EPS)


def to_heads(t, d):
    bsz, l, _ = t.shape
    return t.reshape(bsz, l, GDN_HEADS, d).transpose(0, 2, 1, 3)


def chunk_gated_delta(q, k, v, g, beta, s0):
    bsz, nh, l, dk = q.shape
    dv = v.shape[-1]
    nc = l // GDN_CHUNK

    def chunks(t):
        return t.reshape(bsz, nh, nc, GDN_CHUNK, *t.shape[3:])

    q, k, v, g, beta = (chunks(t) for t in (q, k, v, g, beta))
    gc = jnp.cumsum(g, axis=-1)
    idx = jnp.arange(GDN_CHUNK)
    causal = idx[:, None] >= idx[None, :]
    strict = idx[:, None] > idx[None, :]
    decay = jnp.exp(jnp.where(causal, gc[..., :, None] - gc[..., None, :], -jnp.inf))
    k_beta = k * beta[..., None]
    kkt = jnp.einsum("bhncd,bhnsd->bhncs", k_beta, k) * decay
    tmat = jnp.where(strict, kkt, 0.0) + jnp.eye(GDN_CHUNK, dtype=q.dtype)
    rhs = jnp.concatenate([v * beta[..., None], k_beta * jnp.exp(gc)[..., None]], axis=-1)
    sol = lax.linalg.triangular_solve(tmat, rhs, left_side=True, lower=True, unit_diagonal=True)
    u, w = sol[..., :dv], sol[..., dv:]
    qk = jnp.einsum("bhncd,bhnsd->bhncs", q, k) * decay
    q_dec = q * jnp.exp(gc)[..., None]
    k_dec = k * jnp.exp(gc[..., -1:] - gc)[..., None]
    g_tot = jnp.exp(gc[..., -1])

    def step(s, xs):
        qd, kd, uc, wc, qkc, gt = xs
        v_new = uc - jnp.einsum("bhcd,bhde->bhce", wc, s)
        o = jnp.einsum("bhcd,bhde->bhce", qd, s) + jnp.einsum("bhcs,bhse->bhce", qkc, v_new)
        s = s * gt[..., None, None] + jnp.einsum("bhcd,bhce->bhde", kd, v_new)
        return s, o

    xs = tuple(jnp.moveaxis(t, 2, 0) for t in (q_dec, k_dec, u, w, qk, g_tot))
    s_fin, o = lax.scan(step, s0, xs)
    o = jnp.moveaxis(o, 0, 2).reshape(bsz, nh, l, dv)
    return o, s_fin


def even_mixer(h, w_in, cf_conv, cf_ln_g, cf_ln_b, gdn_conv, a_log, dt_bias, norm_g, rows, init):
    f32 = jnp.float32
    bsz, l, _ = h.shape
    splits = np.cumsum([CF_W, CF_W, GDN_CONV_W, GDN_W, 2 * GDN_HEADS]).tolist()
    a_val, a_gate, qkv, z, beta_logit, a_logit = jnp.split(h @ w_in, splits, axis=-1)
    a = dwconv_tokens(a_val * jax.nn.sigmoid(a_gate), cf_conv, rows, "row")
    a = jax.nn.silu(layer_norm(a, cf_ln_g, cf_ln_b))
    qkv = jax.nn.silu(dwconv_tokens(qkv, gdn_conv, rows, "row")).astype(f32)
    q, k, v = jnp.split(qkv, [GDN_QK, 2 * GDN_QK], axis=-1)
    q = l2norm(to_heads(q, GDN_DK)) * (GDN_DK ** -0.5)
    k = l2norm(to_heads(k, GDN_DK))
    v = to_heads(v, GDN_DV)
    beta = jax.nn.sigmoid(beta_logit.astype(f32)).reshape(bsz, l, 2, GDN_HEADS).transpose(2, 0, 3, 1)
    g = -jnp.exp(a_log.astype(f32)) * jax.nn.softplus(
        a_logit.astype(f32).reshape(bsz, l, 2, GDN_HEADS) + dt_bias.astype(f32))
    g = g.transpose(2, 0, 3, 1)
    if init is None:
        zero = jnp.zeros((bsz, GDN_HEADS, GDN_DK, GDN_DV), f32)
        init = (zero, zero)
    flip = lambda t: jnp.flip(t, axis=2)
    o_f, s_f = chunk_gated_delta(q, k, v, g[0], beta[0], init[0])
    o_b, s_b = chunk_gated_delta(flip(q), flip(k), flip(v), flip(g[1]), flip(beta[1]), init[1])
    o = (o_f + flip(o_b)).transpose(0, 2, 1, 3)
    o = o * lax.rsqrt(jnp.mean(o * o, -1, keepdims=True) + RMS_EPS) * norm_g.astype(f32)
    o = o * jax.nn.silu(z.astype(f32).reshape(bsz, l, GDN_HEADS, GDN_DV))
    o = o.reshape(bsz, l, GDN_W).astype(h.dtype)
    return jnp.concatenate([a, o], axis=-1), (s_f, s_b)


def complex_linear_scan(ar, ai, br, bi):
    ar = jnp.broadcast_to(ar, br.shape)
    ai = jnp.broadcast_to(ai, br.shape)

    def combine(e1, e2):
        ar1, ai1, br1, bi1 = e1
        ar2, ai2, br2, bi2 = e2
        return (ar1 * ar2 - ai1 * ai2, ar1 * ai2 + ai1 * ar2,
                ar2 * br1 - ai2 * bi1 + br2, ar2 * bi1 + ai2 * br1 + bi2)

    _, _, xr, xi = lax.associative_scan(combine, (ar, ai, br, bi), axis=1)
    return xr, xi


def s5_bidir(u, lam_re, lam_im, log_dt, b_re, b_im, c_re, c_im, d_skip, init):
    f32 = jnp.float32
    bsz, l, _ = u.shape
    ug = u.reshape(bsz, l, S5_G, S5_P)
    b_re, b_im, c_re, c_im = (t.astype(f32) for t in (b_re, b_im, c_re, c_im))
    y = d_skip.astype(f32) * u
    finals = []
    for di in range(2):
        lr, li = lam_re[di].astype(f32), lam_im[di].astype(f32)
        dt = jnp.exp(log_dt[di].astype(f32))[:, None]
        mag = jnp.exp(lr * dt)
        ar, ai = mag * jnp.cos(li * dt), mag * jnp.sin(li * dt)
        den = lr * lr + li * li
        fr = ((ar - 1.0) * lr + ai * li) / den
        fi = (ai * lr - (ar - 1.0) * li) / den
        bbr = fr[..., None] * b_re - fi[..., None] * b_im
        bbi = fr[..., None] * b_im + fi[..., None] * b_re
        seq = ug if di == 0 else jnp.flip(ug, 1)
        bu_r = jnp.einsum("blgp,gnp->blgn", seq, bbr)
        bu_i = jnp.einsum("blgp,gnp->blgn", seq, bbi)
        if init is not None:
            xr0, xi0 = init[di]
            bu_r = bu_r.at[:, 0].add(ar * xr0 - ai * xi0)
            bu_i = bu_i.at[:, 0].add(ar * xi0 + ai * xr0)
        xr, xi = complex_linear_scan(ar, ai, bu_r, bu_i)
        yd = jnp.einsum("blgn,gpn->blgp", xr, c_re) - jnp.einsum("blgn,gpn->blgp", xi, c_im)
        if di == 1:
            yd = jnp.flip(yd, 1)
        y = y + yd.reshape(bsz, l, S5_W)
        finals.append((xr[:, -1], xi[:, -1]))
    return y, (finals[0], finals[1])


def odd_mixer(h, w_in, lam_re, lam_im, log_dt, b_re, b_im, c_re, c_im, d_skip, glu_w, glu_b, sc_conv, rows, init):
    f32 = jnp.float32
    u, b_gate, c_gate, xin = jnp.split(h @ w_in, [S5_W, S5_W + SC_W, S5_W + 2 * SC_W], axis=-1)
    y, finals = s5_bidir(u.astype(f32), lam_re, lam_im, log_dt, b_re, b_im, c_re, c_im, d_skip, init)
    zg = jax.nn.gelu(y)
    s5_out = (zg * jax.nn.sigmoid(zg @ glu_w.astype(f32) + glu_b.astype(f32))).astype(h.dtype)
    sc_out = b_gate * dwconv_tokens(c_gate * xin, sc_conv, rows, "col")
    return jnp.concatenate([s5_out, sc_out], axis=-1), finals


def conv_ffn(h, w_up, conv_w, w_down, rows, axis):
    val, gate = jnp.split(h @ w_up, 2, axis=-1)
    return (val * jax.nn.silu(dwconv_tokens(gate, conv_w, rows, axis))) @ w_down


def setup_inputs(seed: int = 0) -> dict:
    key = jax.random.key(seed)
    ks = iter(jax.random.split(key, 48))
    f32 = jnp.float32

    def nrm(shape, std):
        return std * jax.random.normal(next(ks), shape, f32)

    def gain(shape):
        return 1.0 + nrm(shape, 0.02)

    def log_dt(shape):
        return jax.random.uniform(next(ks), shape, f32, math.log(DT_MIN), math.log(DT_MAX))

    gdn_dt = jnp.exp(log_dt((N_EVEN, 2, GDN_HEADS)))
    gdn_dt_bias = gdn_dt + jnp.log(-jnp.expm1(-gdn_dt))
    gdn_a_log = jnp.log(jax.random.uniform(next(ks), (N_EVEN, 2, GDN_HEADS), f32, 1.0, 16.0))
    n_idx = jnp.arange(S5_N, dtype=f32)
    return {
        "x": nrm((BATCH, SEQ, D_MODEL), 1.0),
        "c": nrm((BATCH, D_MODEL), 1.0),
        "ctx": nrm((BATCH, CTX_LEN, D_MODEL), 1.0),
        "c_ctx": nrm((D_MODEL,), 1.0),
        "ada_w": nrm((DEPTH, D_MODEL, 6 * D_MODEL), 0.5 * D_MODEL ** -0.5),
        "ada_b": nrm((DEPTH, 6 * D_MODEL), 0.01),
        "ln_g": gain((DEPTH, 2, D_MODEL)),
        "ln_b": nrm((DEPTH, 2, D_MODEL), 0.02),
        "ev_w_in": nrm((N_EVEN, D_MODEL, EV_IN), D_MODEL ** -0.5),
        "ev_w_out": nrm((N_EVEN, D_MIX, D_MODEL), DN_BETA * D_MIX ** -0.5),
        "cf_conv": nrm((N_EVEN, CF_CONV, CF_W), CF_CONV ** -0.5),
        "cf_ln_g": gain((N_EVEN, CF_W)),
        "cf_ln_b": nrm((N_EVEN, CF_W), 0.02),
        "gdn_conv": nrm((N_EVEN, GDN_CONV, GDN_CONV_W), GDN_CONV ** -0.5),
        "gdn_a_log": gdn_a_log,
        "gdn_dt_bias": gdn_dt_bias,
        "gdn_norm_g": gain((N_EVEN, GDN_DV)),
        "od_w_in": nrm((N_ODD, D_MODEL, OD_IN), D_MODEL ** -0.5),
        "od_w_out": nrm((N_ODD, D_MIX, D_MODEL), DN_BETA * D_MIX ** -0.5),
        "s5_lam_re": -0.5 + nrm((N_ODD, 2, S5_G, S5_N), 0.01),
        "s5_lam_im": math.pi * n_idx + nrm((N_ODD, 2, S5_G, S5_N), 0.01),
        "s5_log_dt": log_dt((N_ODD, 2, S5_G)),
        "s5_b_re": nrm((N_ODD, S5_G, S5_N, S5_P), (2 * S5_P) ** -0.5),
        "s5_b_im": nrm((N_ODD, S5_G, S5_N, S5_P), (2 * S5_P) ** -0.5),
        "s5_c_re": nrm((N_ODD, S5_G, S5_P, S5_N), S5_N ** -0.5),
        "s5_c_im": nrm((N_ODD, S5_G, S5_P, S5_N), S5_N ** -0.5),
        "s5_d": nrm((N_ODD, S5_W), 0.5),
        "s5_glu_w": nrm((N_ODD, S5_W, S5_W), S5_W ** -0.5),
        "s5_glu_b": nrm((N_ODD, S5_W), 0.01),
        "sc_conv": nrm((N_ODD, SC_CONV, SC_W), SC_CONV ** -0.5),
        "ffn_w_up": nrm((DEPTH, D_MODEL, 2 * FFN_F), D_MODEL ** -0.5),
        "ffn_conv": nrm((DEPTH, FFN_CONV, FFN_F), FFN_CONV ** -0.5),
        "ffn_w_down": nrm((DEPTH, FFN_F, D_MODEL), DN_BETA * FFN_F ** -0.5),
    }


def reference(x, c, ctx, c_ctx, ada_w, ada_b, ln_g, ln_b,
              ev_w_in, ev_w_out, cf_conv, cf_ln_g, cf_ln_b, gdn_conv, gdn_a_log, gdn_dt_bias, gdn_norm_g,
              od_w_in, od_w_out, s5_lam_re, s5_lam_im, s5_log_dt, s5_b_re, s5_b_im, s5_c_re, s5_c_im,
              s5_d, s5_glu_w, s5_glu_b, sc_conv, ffn_w_up, ffn_conv, ffn_w_down):
    rows = x.shape[1] // GRID_W
    for i in range(DEPTH):
        j = i // 2
        last = i == DEPTH - 1
        mx = (jax.nn.silu(c) @ ada_w[i] + ada_b[i])[:, None, :]
        mc = jax.nn.silu(c_ctx) @ ada_w[i] + ada_b[i]
        sh1, sc1, g1, sh2, sc2, g2 = jnp.split(mx, 6, axis=-1)
        csh1, csc1, cg1, csh2, csc2, cg2 = jnp.split(mc, 6, axis=-1)
        hx = x * (1.0 + sc1) + sh1
        hc = ctx * (1.0 + csc1) + csh1
        if i % 2 == 0:
            mix = functools.partial(even_mixer, w_in=ev_w_in[j], cf_conv=cf_conv[j], cf_ln_g=cf_ln_g[j],
                                    cf_ln_b=cf_ln_b[j], gdn_conv=gdn_conv[j], a_log=gdn_a_log[j],
                                    dt_bias=gdn_dt_bias[j], norm_g=gdn_norm_g[j])
            w_out = ev_w_out[j]
            ffn_axis = "col"
        else:
            mix = functools.partial(odd_mixer, w_in=od_w_in[j], lam_re=s5_lam_re[j], lam_im=s5_lam_im[j],
                                    log_dt=s5_log_dt[j], b_re=s5_b_re[j], b_im=s5_b_im[j], c_re=s5_c_re[j],
                                    c_im=s5_c_im[j], d_skip=s5_d[j], glu_w=s5_glu_w[j], glu_b=s5_glu_b[j],
                                    sc_conv=sc_conv[j])
            w_out = od_w_out[j]
            ffn_axis = "row"
        fc, ctx_states = mix(hc, rows=None, init=None)
        fx, _ = mix(hx, rows=rows, init=ctx_states)
        x = layer_norm(DN_ALPHA * x + g1 * (fx @ w_out), ln_g[i, 0], ln_b[i, 0])
        x = layer_norm(DN_ALPHA * x + g2 * conv_ffn(x * (1.0 + sc2) + sh2, ffn_w_up[i], ffn_conv[i],
                                                    ffn_w_down[i], rows, ffn_axis), ln_g[i, 1], ln_b[i, 1])
        if not last:
            ctx = layer_norm(DN_ALPHA * ctx + cg1 * (fc @ w_out), ln_g[i, 0], ln_b[i, 0])
            ctx = layer_norm(DN_ALPHA * ctx + cg2 * conv_ffn(ctx * (1.0 + csc2) + csh2, ffn_w_up[i], ffn_conv[i],
                                                              ffn_w_down[i], None, ffn_axis), ln_g[i, 1], ln_b[i, 1])
    return x
```

```python
import functools
import math

import jax
import jax.numpy as jnp
from jax import lax
from jax.experimental import pallas as pl
from jax.experimental.pallas import tpu as pltpu

D_MODEL = 1024
BATCH = 8
SEQ = 2048
DEPTH = 4
GRID_W = 64
GRID_H = SEQ // GRID_W
CTX_LEN = 256
CF_W = 512
CF_CONV = 31
GDN_HEADS = 4
GDN_DK = 128
GDN_DV = 128
GDN_QK = GDN_HEADS * GDN_DK
GDN_W = GDN_HEADS * GDN_DV
GDN_CONV_W = 2 * GDN_QK + GDN_W
GDN_CHUNK = 64
S5_W = 512
S5_P = 16
S5_G = S5_W // S5_P
S5_N = 64
SC_W = 512
FFN_F = 2816
DN_ALPHA = (2 * DEPTH) ** 0.25
LN_EPS = 1e-5
RMS_EPS = 1e-6

SUBLANES = 8
LANES = 128
VMEM_LIMIT_BYTES = 56 * 1024 * 1024

ROWS_LAT = SEQ * BATCH
ROWS_CTX = CTX_LEN * BATCH
SEG_ROW = GRID_W * BATCH
PIECE = 512
CONV_ROWS = 64
FFN_TF = 256
S5_HALF = S5_W // 2
S5_HSTATE = (S5_G // 2) * S5_N
S5_TOK = 64

F32 = jnp.float32
BF16 = jnp.bfloat16

assert BATCH == SUBLANES


def _dot(a, b):
    return jnp.dot(a, b, preferred_element_type=F32)


def _dot_nt(a, b):
    return lax.dot_general(a, b, (((1,), (1,)), ((), ())), preferred_element_type=F32)


def _dot_tn(a, b):
    return lax.dot_general(a, b, (((0,), (0,)), ((), ())), preferred_element_type=F32)


def _split_bf16(a):
    hi = a.astype(BF16)
    lo = (a - hi.astype(F32)).astype(BF16)
    return hi, lo


def _dot_split(a, b):
    ah, al = _split_bf16(a)
    bh, bl = _split_bf16(b)
    return _dot(ah, bh) + (_dot(ah, bl) + _dot(al, bh))


def _sigmoid(x):
    return 1.0 / (1.0 + jnp.exp(-x))


def _silu(x):
    return x * _sigmoid(x)


def _per_sample(x, vec, op):
    r, c = x.shape
    x3 = x.reshape(r // SUBLANES, SUBLANES, c)
    return op(x3, vec[None]).reshape(r, c)


def _modulate(x, scale, shift):
    y = _per_sample(x, 1.0 + scale, lambda a, b: a * b)
    return _per_sample(y, shift, lambda a, b: a + b)


def _layer_norm(x, g, b):
    mu = jnp.mean(x, axis=-1, keepdims=True)
    xc = x - mu
    var = jnp.mean(xc * xc, axis=-1, keepdims=True)
    return xc * lax.rsqrt(var + LN_EPS) * g + b


def _load_rows(ref, start, n):
    if len(ref.shape) == 2:
        return ref[start:start + n, :]
    rg = ref.shape[1]
    return ref[start // rg:(start + n) // rg, :, :].reshape(n, ref.shape[2])


def _store_rows(ref, start, value):
    n = value.shape[0]
    if len(ref.shape) == 2:
        ref[start:start + n, :] = value
    else:
        rg = ref.shape[1]
        ref[start // rg:(start + n) // rg, :, :] = value.reshape(n // rg, rg, value.shape[1])


def _params(n_axes):
    return pltpu.CompilerParams(dimension_semantics=("arbitrary",) * n_axes,
                                vmem_limit_bytes=VMEM_LIMIT_BYTES)


class _Tiling:
    def __init__(self, kind, rows, tm):
        self.kind, self.rows, self.tm = kind, rows, tm
        self.n_tiles = rows // tm
        if kind == "col":
            self.rg = tm // GRID_H

    def view(self, a):
        if self.kind == "row":
            return a
        return a.reshape(GRID_H, SEG_ROW, a.shape[-1])

    def unview(self, a):
        return a.reshape(self.rows, a.shape[-1])

    def shape(self, c):
        return (self.rows, c) if self.kind == "row" else (GRID_H, SEG_ROW, c)

    def spec(self, c, col=None):
        col = col or (lambda *ij: 0)
        if self.kind == "row":
            return pl.BlockSpec((self.tm, c), lambda *ij: (ij[0], col(*ij)))
        return pl.BlockSpec((GRID_H, self.rg, c), lambda *ij: (0, ij[0], col(*ij)))


def _const_spec(shape):
    nd = len(shape)
    return pl.BlockSpec(shape, lambda *ij: (0,) * nd)


def _ada_kernel(c_ref, w_ref, b_ref, o_ref):
    s = _silu(c_ref[...]).astype(BF16)
    o_ref[...] = _dot(s, w_ref[...].astype(BF16)) + b_ref[...]


def _ada(cvec, ada_w, ada_b):
    return pl.pallas_call(
        _ada_kernel,
        grid=(DEPTH, 6),
        in_specs=[_const_spec((2 * SUBLANES, D_MODEL)),
                  pl.BlockSpec((None, D_MODEL, D_MODEL), lambda i, j: (i, 0, j)),
                  pl.BlockSpec((None, 1, D_MODEL), lambda i, j: (i, 0, j))],
        out_specs=pl.BlockSpec((None, 2 * SUBLANES, D_MODEL), lambda i, j: (i, 0, j)),
        out_shape=jax.ShapeDtypeStruct((DEPTH, 2 * SUBLANES, 6 * D_MODEL), F32),
        compiler_params=_params(2), name="ada",
    )(cvec, ada_w, ada_b.reshape(DEPTH, 1, 6 * D_MODEL))


def _cf_kernel(x_ref, sc_ref, sh_ref, w_ref, cw_ref, g_ref, b_ref, o_ref, pad_ref, *, seg):
    tm = x_ref.shape[0]
    halo = (CF_CONV // 2) * SUBLANES
    zeros = jnp.zeros((halo, CF_W), F32)
    for s in range(tm // seg):
        pad_ref[s, 0:halo, :] = zeros
        pad_ref[s, halo + seg:halo + seg + halo, :] = zeros
    for p in range(tm // PIECE):
        h = _modulate(x_ref[p * PIECE:(p + 1) * PIECE, :], sc_ref[...], sh_ref[...]).astype(BF16)
        y = _dot(h, w_ref[...])
        glu = y[:, :CF_W] * _sigmoid(y[:, CF_W:])
        s, off = divmod(p * PIECE, seg)
        pad_ref[s, halo + off:halo + off + PIECE, :] = glu
    for s in range(tm // seg):
        def body(i, carry, s=s):
            r0 = pl.multiple_of(i * CONV_ROWS, CONV_ROWS)
            acc = jnp.zeros((CONV_ROWS, CF_W), F32)
            for k in range(CF_CONV):
                acc = acc + cw_ref[k:k + 1, :] * pad_ref[s, pl.ds(r0 + k * SUBLANES, CONV_ROWS), :]
            o_ref[pl.ds(s * seg + r0, CONV_ROWS), :] = _silu(_layer_norm(acc, g_ref[...], b_ref[...]))
            return carry
        lax.fori_loop(0, seg // CONV_ROWS, body, 0)


def _cf_mixer(x, sc, sh, w, cw, ln_g, ln_b, *, tm, seg):
    rows = x.shape[0]
    halo = (CF_CONV // 2) * SUBLANES
    return pl.pallas_call(
        functools.partial(_cf_kernel, seg=seg),
        grid=(rows // tm,),
        in_specs=[pl.BlockSpec((tm, D_MODEL), lambda i: (i, 0)),
                  _const_spec((SUBLANES, D_MODEL)), _const_spec((SUBLANES, D_MODEL)),
                  _const_spec((D_MODEL, 2 * CF_W)), _const_spec((CF_CONV, CF_W)),
                  _const_spec((1, CF_W)), _const_spec((1, CF_W))],
        out_specs=pl.BlockSpec((tm, CF_W), lambda i: (i, 0)),
        out_shape=jax.ShapeDtypeStruct((rows, CF_W), F32),
        scratch_shapes=[pltpu.VMEM((tm // seg, seg + 2 * halo, CF_W), F32)],
        compiler_params=_params(1), name="cf_mixer",
    )(x, sc, sh, w, cw, ln_g, ln_b)


def _qkv_kernel(x_ref, sc_ref, sh_ref, w_ref, cw_ref, o_ref, h_ref, pad_ref, *, seg):
    j = pl.program_id(1)
    tm = x_ref.shape[0]
    halo = SUBLANES

    @pl.when(j == 0)
    def _():
        zeros = jnp.zeros((halo, GDN_QK), F32)
        for s in range(tm // seg):
            pad_ref[s, 0:halo, :] = zeros
            pad_ref[s, halo + seg:halo + seg + halo, :] = zeros
        for p in range(tm // PIECE):
            rows = slice(p * PIECE, (p + 1) * PIECE)
            h_ref[rows, :] = _modulate(x_ref[rows, :], sc_ref[...], sh_ref[...]).astype(BF16)

    for p in range(tm // PIECE):
        s, off = divmod(p * PIECE, seg)
        pad_ref[s, halo + off:halo + off + PIECE, :] = _dot(h_ref[p * PIECE:(p + 1) * PIECE, :], w_ref[...])

    is_qk = j < 2
    scale = jnp.where(j == 0, GDN_DK ** -0.5, 1.0).astype(F32)
    blk = 2 * CONV_ROWS
    for s in range(tm // seg):
        def body(i, carry, s=s):
            r0 = pl.multiple_of(i * blk, blk)
            y = (cw_ref[0:1, :] * pad_ref[s, pl.ds(r0, blk), :]
                 + cw_ref[1:2, :] * pad_ref[s, pl.ds(r0 + halo, blk), :]
                 + cw_ref[2:3, :] * pad_ref[s, pl.ds(r0 + 2 * halo, blk), :])
            y = _silu(y)
            heads = []
            for hh in range(GDN_HEADS):
                t = y[:, hh * GDN_DK:(hh + 1) * GDN_DK]
                nrm = t * (lax.rsqrt(jnp.sum(t * t, axis=-1, keepdims=True) + RMS_EPS) * scale)
                heads.append(jnp.where(is_qk, nrm, t))
            o_ref[pl.ds(s * seg + r0, blk), :] = jnp.concatenate(heads, axis=-1)
            return carry
        lax.fori_loop(0, seg // blk, body, 0)


def _qkv_proj(x, sc, sh, w, cw, *, tm, seg):
    rows = x.shape[0]
    return pl.pallas_call(
        functools.partial(_qkv_kernel, seg=seg),
        grid=(rows // tm, 3),
        in_specs=[pl.BlockSpec((tm, D_MODEL), lambda i, j: (i, 0)),
                  _const_spec((SUBLANES, D_MODEL)), _const_spec((SUBLANES, D_MODEL)),
                  pl.BlockSpec((D_MODEL, GDN_QK), lambda i, j: (0, j)),
                  pl.BlockSpec((3, GDN_QK), lambda i, j: (0, j))],
        out_specs=pl.BlockSpec((tm, GDN_QK), lambda i, j: (i, j)),
        out_shape=jax.ShapeDtypeStruct((rows, GDN_CONV_W), F32),
        scratch_shapes=[pltpu.VMEM((tm, D_MODEL), BF16),
                        pltpu.VMEM((tm // seg, seg + 2 * SUBLANES, GDN_QK), F32)],
        compiler_params=_params(2), name="qkv_proj",
    )(x, sc, sh, w, cw)


def _zbg_kernel(x_ref, sc_ref, sh_ref, w_ref, alog_ref, dtb_ref, z_ref, bg_ref):
    tm = x_ref.shape[0]
    for p in range(tm // PIECE):
        rows = slice(p * PIECE, (p + 1) * PIECE)
        h = _modulate(x_ref[rows, :], sc_ref[...], sh_ref[...]).astype(BF16)
        y = _dot(h, w_ref[...])
        z_ref[rows, :] = y[:, :GDN_W]
        s = y[:, GDN_W:]
        t = s + dtb_ref[...]
        softplus = jnp.maximum(t, 0.0) + jnp.log1p(jnp.exp(-jnp.abs(t)))
        g = -jnp.exp(alog_ref[...]) * softplus
        lane = lax.broadcasted_iota(jnp.int32, s.shape, 1)
        bg_ref[rows, :] = jnp.where(lane < 2 * GDN_HEADS, _sigmoid(s), g)


def _zbg_proj(x, sc, sh, w, alog, dtb, *, tm):
    rows = x.shape[0]
    return pl.pallas_call(
        _zbg_kernel,
        grid=(rows // tm,),
        in_specs=[pl.BlockSpec((tm, D_MODEL), lambda i: (i, 0)),
                  _const_spec((SUBLANES, D_MODEL)), _const_spec((SUBLANES, D_MODEL)),
                  _const_spec((D_MODEL, GDN_W + LANES)), _const_spec((1, LANES)), _const_spec((1, LANES))],
        out_specs=[pl.BlockSpec((tm, GDN_W), lambda i: (i, 0)), pl.BlockSpec((tm, LANES), lambda i: (i, 0))],
        out_shape=[jax.ShapeDtypeStruct((rows, GDN_W), F32), jax.ShapeDtypeStruct((rows, LANES), F32)],
        compiler_params=_params(1), name="zbg_proj",
    )(x, sc, sh, w, alog, dtb)


def _gdn_chunk(q, k, v, beta_row, g_row, state, rev):
    c = GDN_CHUNK
    ri = lax.broadcasted_iota(jnp.int32, (c, c), 0)
    ci = lax.broadcasted_iota(jnp.int32, (c, c), 1)
    eye = ri == ci
    incl = (ri <= ci) if rev else (ri >= ci)
    incl_t = (ci <= ri) if rev else (ci >= ri)
    strict = (ri < ci) if rev else (ri > ci)
    g_b = jnp.broadcast_to(g_row, (c, c))
    beta_b = jnp.broadcast_to(beta_row, (c, c))
    g_col = jnp.sum(jnp.where(eye, g_b, 0.0), axis=1, keepdims=True)
    beta_col = jnp.sum(jnp.where(eye, beta_b, 0.0), axis=1, keepdims=True)
    gc_col = jnp.sum(jnp.where(incl, g_b, 0.0), axis=1, keepdims=True)
    gc_row = jnp.sum(jnp.where(incl_t, g_col, 0.0), axis=0, keepdims=True)
    g_sum = jnp.sum(g_row, axis=1, keepdims=True)
    decay = jnp.exp(jnp.where(incl, gc_col - gc_row, -1e30))
    e_col = jnp.exp(gc_col)
    e_end = jnp.exp(g_sum - gc_col)
    g_tot = jnp.exp(g_sum)

    k16 = k.astype(BF16)
    kkt = beta_col * _dot_nt(k16, k16) * decay
    nil = jnp.where(strict, -kkt, 0.0)
    inv = jnp.where(eye, 1.0, 0.0) + nil
    power = nil
    for _ in range(5):
        power = _dot_split(power, power)
        inv = inv + _dot_split(power, inv)
    rhs = jnp.concatenate([v * beta_col, k * (beta_col * e_col)], axis=-1)
    sol = _dot_split(inv, rhs)
    u, w = sol[:, :GDN_DV], sol[:, GDN_DV:]
    qk = _dot_nt(q.astype(BF16), k16) * decay

    s16 = state.astype(BF16)
    v_new = u - _dot(w.astype(BF16), s16)
    v16 = v_new.astype(BF16)
    o = _dot((q * e_col).astype(BF16), s16) + _dot(qk.astype(BF16), v16)
    new_state = state * g_tot + _dot_tn((k * e_end).astype(BF16), v16)
    return o, new_state


def _gdn_kernel(qc, kc, vc, zc, bgc, ql, kl, vl, zl, bgl, ng_ref, oc_ref, ol_ref, s_ref):
    s_ref[...] = jnp.zeros(s_ref.shape, F32)
    for q, k, v, z, bg, o in ((qc, kc, vc, zc, bgc, oc_ref), (ql, kl, vl, zl, bgl, ol_ref)):
        n_tok = q.shape[0]
        nc = n_tok // GDN_CHUNK
        o[...] = jnp.zeros(o.shape, F32)

        def body(c, carry, q=q, k=k, v=v, bg=bg, o=o, nc=nc):
            for d in range(2):
                cc = c if d == 0 else nc - 1 - c
                r0 = pl.multiple_of(cc * GDN_CHUNK, GDN_CHUNK)
                rows = pl.ds(r0, GDN_CHUNK)
                o_d, s_new = _gdn_chunk(q[rows, :], k[rows, :], v[rows, :],
                                        bg[d, pl.ds(cc, 1), :], bg[2 + d, pl.ds(cc, 1), :],
                                        s_ref[d], rev=(d == 1))
                s_ref[d] = s_new
                o[rows, :] = o[rows, :] + o_d
            return carry
        lax.fori_loop(0, nc, body, 0)

        blk = 256
        for r in range(n_tok // blk):
            rows = slice(r * blk, (r + 1) * blk)
            ov = o[rows, :]
            ov = ov * lax.rsqrt(jnp.mean(ov * ov, axis=-1, keepdims=True) + RMS_EPS) * ng_ref[...]
            o[rows, :] = ov * _silu(z[rows, :])


def _gdn(qkv_c, z_c, bg_c, qkv_l, z_l, bg_l, norm_g):
    def tok_spec(n_tok, col0):
        return pl.BlockSpec((None, n_tok, GDN_DK), lambda b, h: (b, 0, col0 + h))

    def bg_spec(n_tok):
        return pl.BlockSpec((None, None, 4, n_tok // GDN_CHUNK, GDN_CHUNK), lambda b, h: (b, h, 0, 0, 0))

    in_specs = []
    for n_tok in (CTX_LEN, SEQ):
        in_specs += [tok_spec(n_tok, 0), tok_spec(n_tok, GDN_HEADS), tok_spec(n_tok, 2 * GDN_HEADS),
                     tok_spec(n_tok, 0), bg_spec(n_tok)]
    in_specs.append(_const_spec((1, GDN_DV)))
    return pl.pallas_call(
        _gdn_kernel,
        grid=(BATCH, GDN_HEADS),
        in_specs=in_specs,
        out_specs=[tok_spec(CTX_LEN, 0), tok_spec(SEQ, 0)],
        out_shape=[jax.ShapeDtypeStruct((BATCH, CTX_LEN, GDN_W), F32),
                   jax.ShapeDtypeStruct((BATCH, SEQ, GDN_W), F32)],
        scratch_shapes=[pltpu.VMEM((2, GDN_DK, GDN_DV), F32)],
        compiler_params=_params(2), name="gdn",
    )(qkv_c, qkv_c, qkv_c, z_c, bg_c, qkv_l, qkv_l, qkv_l, z_l, bg_l, norm_g)


def _residual_norm(x, branch, gate, g, b):
    return _layer_norm(DN_ALPHA * x + _per_sample(branch, gate, lambda a, c: a * c), g, b)


def _out_even_kernel(x_ref, a_ref, o_ref, g1_ref, wa_ref, wo_ref, lng_ref, lnb_ref, out_ref):
    tm = x_ref.shape[0]
    for p in range(tm // PIECE):
        rows = slice(p * PIECE, (p + 1) * PIECE)
        m = _dot(a_ref[rows, :].astype(BF16), wa_ref[...]) + _dot(o_ref[rows, :].astype(BF16), wo_ref[...])
        out_ref[rows, :] = _residual_norm(x_ref[rows, :], m, g1_ref[...], lng_ref[...], lnb_ref[...])


def _out_even(x, a, o, g1, wa, wo, ln_g, ln_b, *, tm):
    rows = x.shape[0]
    row = lambda c: pl.BlockSpec((tm, c), lambda i: (i, 0))
    return pl.pallas_call(
        _out_even_kernel,
        grid=(rows // tm,),
        in_specs=[row(D_MODEL), row(CF_W), row(GDN_W), _const_spec((SUBLANES, D_MODEL)),
                  _const_spec((CF_W, D_MODEL)), _const_spec((GDN_W, D_MODEL)),
                  _const_spec((1, D_MODEL)), _const_spec((1, D_MODEL))],
        out_specs=row(D_MODEL),
        out_shape=jax.ShapeDtypeStruct((rows, D_MODEL), F32),
        compiler_params=_params(1), name="out_even",
    )(x, a, o, g1, wa, wo, ln_g, ln_b)


def _gelu_tanh(x):
    return 0.5 * x * (1.0 + jnp.tanh(math.sqrt(2.0 / math.pi) * (x + 0.044715 * (x * x * x))))


def _out_odd_kernel(x_ref, u_ref, yf_ref, yb_ref, s_ref, g1_ref, d_ref, gw_ref, gb_ref, wa_ref, wo_ref,
                    lng_ref, lnb_ref, out_ref):
    tm = x_ref.shape[0]
    for p in range(tm // PIECE):
        rows = slice(p * PIECE, (p + 1) * PIECE)
        y = d_ref[...] * u_ref[rows, :] + yf_ref[rows, :] + yb_ref[rows, :]
        zg = _gelu_tanh(y)
        s5 = zg * _sigmoid(_dot(zg.astype(BF16), gw_ref[...]) + gb_ref[...])
        m = _dot(s5.astype(BF16), wa_ref[...]) + _dot(s_ref[rows, :].astype(BF16), wo_ref[...])
        out_ref[rows, :] = _residual_norm(x_ref[rows, :], m, g1_ref[...], lng_ref[...], lnb_ref[...])


def _out_odd(x, u, yf, yb, s, g1, d_skip, glu_w, glu_b, wa, wo, ln_g, ln_b, *, tm):
    rows = x.shape[0]
    row = lambda c: pl.BlockSpec((tm, c), lambda i: (i, 0))
    return pl.pallas_call(
        _out_odd_kernel,
        grid=(rows // tm,),
        in_specs=[row(D_MODEL), row(S5_W), row(S5_W), row(S5_W), row(SC_W), _const_spec((SUBLANES, D_MODEL)),
                  _const_spec((1, S5_W)), _const_spec((S5_W, S5_W)), _const_spec((1, S5_W)),
                  _const_spec((S5_W, D_MODEL)), _const_spec((SC_W, D_MODEL)),
                  _const_spec((1, D_MODEL)), _const_spec((1, D_MODEL))],
        out_specs=row(D_MODEL),
        out_shape=jax.ShapeDtypeStruct((rows, D_MODEL), F32),
        compiler_params=_params(1), name="out_odd",
    )(x, u, yf, yb, s, g1, d_skip, glu_w, glu_b, wa, wo, ln_g, ln_b)


def _conv3(pad_ref, s, r0, n, cw_ref, shift):
    return (cw_ref[0:1, :] * pad_ref[s, r0:r0 + n, :]
            + cw_ref[1:2, :] * pad_ref[s, r0 + shift:r0 + shift + n, :]
            + cw_ref[2:3, :] * pad_ref[s, r0 + 2 * shift:r0 + 2 * shift + n, :])


def _ffn_kernel(x_ref, sc_ref, sh_ref, g2_ref, wv_ref, wg_ref, cw_ref, wd_ref, lng_ref, lnb_ref, out_ref,
                h_ref, val_ref, pad_ref, *, tm, shift, seg):
    j = pl.program_id(1)
    n_pieces = tm // PIECE

    @pl.when(j == 0)
    def _():
        zeros = jnp.zeros((shift, FFN_TF), F32)
        for s in range(tm // seg):
            pad_ref[s, 0:shift, :] = zeros
            pad_ref[s, shift + seg:shift + seg + shift, :] = zeros
        for p in range(n_pieces):
            x = _load_rows(x_ref, p * PIECE, PIECE)
            h_ref[p * PIECE:(p + 1) * PIECE, :] = _modulate(x, sc_ref[...], sh_ref[...]).astype(BF16)

    for p in range(n_pieces):
        h = h_ref[p * PIECE:(p + 1) * PIECE, :]
        val_ref[p * PIECE:(p + 1) * PIECE, :] = _dot(h, wv_ref[...])
        s, off = divmod(p * PIECE, seg)
        pad_ref[s, shift + off:shift + off + PIECE, :] = _dot(h, wg_ref[...])

    for p in range(n_pieces):
        s, off = divmod(p * PIECE, seg)
        gate = _conv3(pad_ref, s, off, PIECE, cw_ref, shift)
        act = (val_ref[p * PIECE:(p + 1) * PIECE, :] * _silu(gate)).astype(BF16)
        part = _dot(act, wd_ref[...])

        @pl.when(j == 0)
        def _():
            _store_rows(out_ref, p * PIECE, part)

        @pl.when(j > 0)
        def _():
            _store_rows(out_ref, p * PIECE, _load_rows(out_ref, p * PIECE, PIECE) + part)

    @pl.when(j == pl.num_programs(1) - 1)
    def _():
        for p in range(n_pieces):
            x = _load_rows(x_ref, p * PIECE, PIECE)
            y = _residual_norm(x, _load_rows(out_ref, p * PIECE, PIECE), g2_ref[...], lng_ref[...], lnb_ref[...])
            _store_rows(out_ref, p * PIECE, y)


def _ffn(x, sc, sh, g2, w_up, cw, w_down, ln_g, ln_b, *, tiling, shift, seg):
    tm = tiling.tm
    n_f = FFN_F // FFN_TF
    out = pl.pallas_call(
        functools.partial(_ffn_kernel, tm=tm, shift=shift, seg=seg),
        grid=(tiling.n_tiles, n_f),
        in_specs=[tiling.spec(D_MODEL),
                  _const_spec((SUBLANES, D_MODEL)), _const_spec((SUBLANES, D_MODEL)), _const_spec((SUBLANES, D_MODEL)),
                  pl.BlockSpec((D_MODEL, FFN_TF), lambda i, j: (0, j)),
                  pl.BlockSpec((D_MODEL, FFN_TF), lambda i, j: (0, n_f + j)),
                  pl.BlockSpec((3, FFN_TF), lambda i, j: (0, j)),
                  pl.BlockSpec((FFN_TF, D_MODEL), lambda i, j: (j, 0)),
                  _const_spec((1, D_MODEL)), _const_spec((1, D_MODEL))],
        out_specs=tiling.spec(D_MODEL),
        out_shape=jax.ShapeDtypeStruct(tiling.shape(D_MODEL), F32),
        scratch_shapes=[pltpu.VMEM((tm, D_MODEL), BF16), pltpu.VMEM((tm, FFN_TF), F32),
                        pltpu.VMEM((tm // seg, seg + 2 * shift, FFN_TF), F32)],
        compiler_params=_params(2), name="conv_ffn",
    )(tiling.view(x), sc, sh, g2, w_up, w_up, cw, w_down, ln_g, ln_b)
    return tiling.unview(out)


def _odd_in_kernel(x_ref, sc_ref, sh_ref, wu_ref, wb_ref, wc_ref, wx_ref, cw_ref, u_ref, s_ref,
                   h_ref, b_ref, pad_ref, *, tm, shift, seg):
    j = pl.program_id(1)
    n_pieces = tm // PIECE
    cb = u_ref.shape[-1]

    @pl.when(j == 0)
    def _():
        zeros = jnp.zeros((shift, cb), F32)
        for s in range(tm // seg):
            pad_ref[s, 0:shift, :] = zeros
            pad_ref[s, shift + seg:shift + seg + shift, :] = zeros
        for p in range(n_pieces):
            x = _load_rows(x_ref, p * PIECE, PIECE)
            h_ref[p * PIECE:(p + 1) * PIECE, :] = _modulate(x, sc_ref[...], sh_ref[...]).astype(BF16)

    for p in range(n_pieces):
        h = h_ref[p * PIECE:(p + 1) * PIECE, :]
        _store_rows(u_ref, p * PIECE, _dot(h, wu_ref[...]))
        b_ref[p * PIECE:(p + 1) * PIECE, :] = _dot(h, wb_ref[...])
        s, off = divmod(p * PIECE, seg)
        pad_ref[s, shift + off:shift + off + PIECE, :] = _dot(h, wc_ref[...]) * _dot(h, wx_ref[...])

    for p in range(n_pieces):
        s, off = divmod(p * PIECE, seg)
        _store_rows(s_ref, p * PIECE, b_ref[p * PIECE:(p + 1) * PIECE, :] * _conv3(pad_ref, s, off, PIECE, cw_ref, shift))


def _odd_in(x, sc, sh, w_in, cw, *, tiling, shift, seg):
    tm = tiling.tm
    cb = 2 * LANES
    nb = S5_W // cb
    wspec = lambda k: pl.BlockSpec((D_MODEL, cb), lambda i, j: (0, k * nb + j))
    u, s = pl.pallas_call(
        functools.partial(_odd_in_kernel, tm=tm, shift=shift, seg=seg),
        grid=(tiling.n_tiles, nb),
        in_specs=[tiling.spec(D_MODEL), _const_spec((SUBLANES, D_MODEL)), _const_spec((SUBLANES, D_MODEL)),
                  wspec(0), wspec(1), wspec(2), wspec(3),
                  pl.BlockSpec((3, cb), lambda i, j: (0, j))],
        out_specs=[tiling.spec(cb, lambda i, j: j), tiling.spec(cb, lambda i, j: j)],
        out_shape=[jax.ShapeDtypeStruct(tiling.shape(S5_W), F32), jax.ShapeDtypeStruct(tiling.shape(SC_W), F32)],
        scratch_shapes=[pltpu.VMEM((tm, D_MODEL), BF16), pltpu.VMEM((tm, cb), F32),
                        pltpu.VMEM((tm // seg, seg + 2 * shift, cb), F32)],
        compiler_params=_params(2), name="odd_in",
    )(tiling.view(x), sc, sh, w_in, w_in, w_in, w_in, cw)
    return tiling.unview(u), tiling.unview(s)


def _s5_kernel(uf_ref, ub_ref, wb_ref, wc_ref, a_ref, x0_ref, yf_ref, yb_ref, xfin_ref, bu_ref, st_ref):
    t = pl.program_id(0)
    rows = uf_ref.shape[0]
    n_tok = rows // SUBLANES
    hs = S5_HSTATE

    @pl.when(t == 0)
    def _():
        st_ref[...] = x0_ref[...]

    for d, u_ref in enumerate((uf_ref, ub_ref)):
        for hf in range(2):
            u = u_ref[:, hf * S5_HALF:(hf + 1) * S5_HALF].astype(BF16)
            bu_ref[d, :, hf * 2 * hs:(hf + 1) * 2 * hs] = _dot(u, wb_ref[d, hf])

    for d in range(2):
        for hf in range(2):
            c_re = hf * 2 * hs
            c_im = c_re + hs
            a_re = jnp.broadcast_to(a_ref[d, hf, 0:1, :], (SUBLANES, hs))
            a_im = jnp.broadcast_to(a_ref[d, hf, 1:2, :], (SUBLANES, hs))

            def body(i, carry, d=d, c_re=c_re, c_im=c_im, a_re=a_re, a_im=a_im):
                xr, xi = carry
                tok = i if d == 0 else n_tok - 1 - i
                r = pl.ds(pl.multiple_of(tok * SUBLANES, SUBLANES), SUBLANES)
                nr = a_re * xr - a_im * xi + bu_ref[d, r, c_re:c_re + hs]
                ni = a_re * xi + a_im * xr + bu_ref[d, r, c_im:c_im + hs]
                bu_ref[d, r, c_re:c_re + hs] = nr
                bu_ref[d, r, c_im:c_im + hs] = ni
                return nr, ni

            xr, xi = lax.fori_loop(0, n_tok, body, (st_ref[d, :, c_re:c_re + hs], st_ref[d, :, c_im:c_im + hs]))
            st_ref[d, :, c_re:c_re + hs] = xr
            st_ref[d, :, c_im:c_im + hs] = xi

    for d, y_ref in enumerate((yf_ref, yb_ref)):
        for hf in range(2):
            xs = bu_ref[d, :, hf * 2 * hs:(hf + 1) * 2 * hs].astype(BF16)
            y_ref[:, hf * S5_HALF:(hf + 1) * S5_HALF] = _dot(xs, wc_ref[d, hf])

    @pl.when(t == pl.num_programs(0) - 1)
    def _():
        xfin_ref[...] = st_ref[...]


def _s5(u, wb, wc, a, x0):
    rows = u.shape[0]
    tr = S5_TOK * SUBLANES
    nt = rows // tr
    state = (2, SUBLANES, 4 * S5_HSTATE)
    return pl.pallas_call(
        _s5_kernel,
        grid=(nt,),
        in_specs=[pl.BlockSpec((tr, S5_W), lambda t: (t, 0)),
                  pl.BlockSpec((tr, S5_W), lambda t: (nt - 1 - t, 0)),
                  _const_spec((2, 2, S5_HALF, 2 * S5_HSTATE)), _const_spec((2, 2, 2 * S5_HSTATE, S5_HALF)),
                  _const_spec((2, 2, 2, S5_HSTATE)), _const_spec(state)],
        out_specs=[pl.BlockSpec((tr, S5_W), lambda t: (t, 0)),
                   pl.BlockSpec((tr, S5_W), lambda t: (nt - 1 - t, 0)),
                   _const_spec(state)],
        out_shape=[jax.ShapeDtypeStruct((rows, S5_W), F32), jax.ShapeDtypeStruct((rows, S5_W), F32),
                   jax.ShapeDtypeStruct(state, F32)],
        scratch_shapes=[pltpu.VMEM((2, tr, 4 * S5_HSTATE), F32), pltpu.VMEM(state, F32)],
        compiler_params=_params(1), name="s5_scan",
    )(u, u, wb, wc, a, x0)


def _s5_weights(lam_re, lam_im, log_dt, b_re, b_im, c_re, c_im):
    eye = jnp.eye(S5_G // 2, dtype=F32)
    wbs, wcs, avs = [], [], []
    for di in range(2):
        lr, li = lam_re[di], lam_im[di]
        dt = jnp.exp(log_dt[di])[:, None]
        mag = jnp.exp(lr * dt)
        ar, ai = mag * jnp.cos(li * dt), mag * jnp.sin(li * dt)
        den = lr * lr + li * li
        fr = ((ar - 1.0) * lr + ai * li) / den
        fi = (ai * lr - (ar - 1.0) * li) / den
        bbr = fr[..., None] * b_re - fi[..., None] * b_im
        bbi = fr[..., None] * b_im + fi[..., None] * b_re

        def in_block(t):
            t = t.reshape(2, S5_G // 2, S5_N, S5_P)
            return jnp.einsum("hgnp,gk->hgpkn", t, eye).reshape(2, S5_HALF, S5_HSTATE)

        def out_block(t):
            t = t.reshape(2, S5_G // 2, S5_P, S5_N)
            return jnp.einsum("hgpn,gk->hgnkp", t, eye).reshape(2, S5_HSTATE, S5_HALF)

        wbs.append(jnp.concatenate([in_block(bbr), in_block(bbi)], axis=-1))
        wcs.append(jnp.concatenate([out_block(c_re), out_block(-c_im)], axis=1))
        avs.append(jnp.stack([ar.reshape(2, S5_HSTATE), ai.reshape(2, S5_HSTATE)], axis=1))
    return jnp.stack(wbs).astype(BF16), jnp.stack(wcs).astype(BF16), jnp.stack(avs)


def _bg_rows(bg, n_tok):
    t = bg.reshape(n_tok, BATCH, LANES)[:, :, :4 * GDN_HEADS].reshape(n_tok, BATCH, 4, GDN_HEADS)
    return t.transpose(1, 3, 2, 0).reshape(BATCH, GDN_HEADS, 4, n_tok // GDN_CHUNK, GDN_CHUNK)


def _to_batch_major(a, n_tok):
    return a.reshape(n_tok, BATCH, a.shape[-1]).transpose(1, 0, 2)


def _to_token_major(a):
    return a.transpose(1, 0, 2).reshape(a.shape[0] * a.shape[1], a.shape[2])


def kernel(x, c, ctx, c_ctx, ada_w, ada_b, ln_g, ln_b, ev_w_in, ev_w_out, cf_conv, cf_ln_g, cf_ln_b, gdn_conv,
           gdn_a_log, gdn_dt_bias, gdn_norm_g, od_w_in, od_w_out, s5_lam_re, s5_lam_im, s5_log_dt, s5_b_re,
           s5_b_im, s5_c_re, s5_c_im, s5_d, s5_glu_w, s5_glu_b, sc_conv, ffn_w_up, ffn_conv, ffn_w_down):
    xl = _to_token_major(x)
    cl = _to_token_major(ctx)

    cvec = jnp.zeros((2 * SUBLANES, D_MODEL), F32).at[:BATCH].set(c).at[BATCH].set(c_ctx)
    mods = _ada(cvec, ada_w, ada_b)

    lat_row = _Tiling("row", ROWS_LAT, 2 * SEG_ROW)
    lat_col = _Tiling("col", ROWS_LAT, 2 * SEG_ROW)
    ctx_row = _Tiling("row", ROWS_CTX, ROWS_CTX)
    lat_along_row = dict(tiling=lat_row, shift=SUBLANES, seg=SEG_ROW)
    lat_along_col = dict(tiling=lat_col, shift=lat_col.rg, seg=lat_col.tm)
    ctx_conv = dict(tiling=ctx_row, shift=SUBLANES, seg=ROWS_CTX)

    row2 = lambda v: v.reshape(1, -1)
    for i in range(DEPTH):
        j = i // 2
        last = i == DEPTH - 1
        m_lat = jnp.split(mods[i, :BATCH], 6, axis=-1)
        m_ctx = jnp.split(jnp.broadcast_to(mods[i, BATCH:BATCH + 1], (BATCH, 6 * D_MODEL)), 6, axis=-1)
        lng1, lnb1, lng2, lnb2 = row2(ln_g[i, 0]), row2(ln_b[i, 0]), row2(ln_g[i, 1]), row2(ln_b[i, 1])
        w_up, w_down, f_cw = ffn_w_up[i].astype(BF16), ffn_w_down[i].astype(BF16), ffn_conv[i]

        if i % 2 == 0:
            w_in = ev_w_in[j]
            w_cf = w_in[:, :2 * CF_W].astype(BF16)
            w_qkv = w_in[:, 2 * CF_W:2 * CF_W + GDN_CONV_W].astype(BF16)
            w_zbg = jnp.pad(w_in[:, 2 * CF_W + GDN_CONV_W:], ((0, 0), (0, LANES - 4 * GDN_HEADS))).astype(BF16)
            pad16 = lambda v: jnp.pad(v.reshape(1, -1), ((0, 0), (2 * GDN_HEADS, LANES - 4 * GDN_HEADS)))
            alog, dtb = pad16(gdn_a_log[j]), pad16(gdn_dt_bias[j])
            w_out = ev_w_out[j].astype(BF16)
            parts = []
            for xs, m, tm, seg, n_tok in ((cl, m_ctx, ROWS_CTX, ROWS_CTX, CTX_LEN), (xl, m_lat, 4 * SEG_ROW, SEG_ROW, SEQ)):
                sh1, sc1 = m[0], m[1]
                a = _cf_mixer(xs, sc1, sh1, w_cf, cf_conv[j], row2(cf_ln_g[j]), row2(cf_ln_b[j]), tm=tm, seg=seg)
                qkv = _qkv_proj(xs, sc1, sh1, w_qkv, gdn_conv[j], tm=tm, seg=seg)
                z, bg = _zbg_proj(xs, sc1, sh1, w_zbg, alog, dtb, tm=tm)
                parts.append((a, _to_batch_major(qkv, n_tok), _to_batch_major(z, n_tok), _bg_rows(bg, n_tok)))
            (a_c, qkv_c, z_c, bg_c), (a_l, qkv_l, z_l, bg_l) = parts
            o_c, o_l = _gdn(qkv_c, z_c, bg_c, qkv_l, z_l, bg_l, row2(gdn_norm_g[j]))
            x1 = _out_even(xl, a_l, _to_token_major(o_l), m_lat[2], w_out[:CF_W], w_out[CF_W:], lng1, lnb1, tm=2 * SEG_ROW)
            if not last:
                c1 = _out_even(cl, a_c, _to_token_major(o_c), m_ctx[2], w_out[:CF_W], w_out[CF_W:], lng1, lnb1, tm=2 * SEG_ROW)
            lat_ffn = lat_along_col
        else:
            w_in = od_w_in[j].astype(BF16)
            w_out = od_w_out[j].astype(BF16)
            wb, wc, av = _s5_weights(s5_lam_re[j], s5_lam_im[j], s5_log_dt[j], s5_b_re[j], s5_b_im[j],
                                     s5_c_re[j], s5_c_im[j])
            u_c, s_c = _odd_in(cl, m_ctx[1], m_ctx[0], w_in, sc_conv[j], **ctx_conv)
            u_l, s_l = _odd_in(xl, m_lat[1], m_lat[0], w_in, sc_conv[j], **lat_along_col)
            zero_state = jnp.zeros((2, SUBLANES, 4 * S5_HSTATE), F32)
            yf_c, yb_c, fin_c = _s5(u_c, wb, wc, av, zero_state)
            yf_l, yb_l, _ = _s5(u_l, wb, wc, av, fin_c)
            odd_w = (row2(s5_d[j]), s5_glu_w[j].astype(BF16), row2(s5_glu_b[j]), w_out[:S5_W], w_out[S5_W:])
            x1 = _out_odd(xl, u_l, yf_l, yb_l, s_l, m_lat[2], *odd_w, lng1, lnb1, tm=2 * SEG_ROW)
            if not last:
                c1 = _out_odd(cl, u_c, yf_c, yb_c, s_c, m_ctx[2], *odd_w, lng1, lnb1, tm=2 * SEG_ROW)
            lat_ffn = lat_along_row

        xl = _ffn(x1, m_lat[4], m_lat[3], m_lat[5], w_up, f_cw, w_down, lng2, lnb2, **lat_ffn)
        if not last:
            cl = _ffn(c1, m_ctx[4], m_ctx[3], m_ctx[5], w_up, f_cw, w_down, lng2, lnb2, **ctx_conv)

    return xl.reshape(SEQ, BATCH, D_MODEL).transpose(1, 0, 2)
```

```python
import functools
import math

import jax
import jax.numpy as jnp
from jax import lax
from jax.experimental import pallas as pl
from jax.experimental.pallas import tpu as pltpu

D_MODEL = 1024
BATCH = 8
SEQ = 2048
DEPTH = 4
GRID_W = 64
GRID_H = SEQ // GRID_W
CTX_LEN = 256
CF_W = 512
CF_CONV = 31
GDN_HEADS = 4
GDN_DK = 128
GDN_DV = 128
GDN_QK = GDN_HEADS * GDN_DK
GDN_W = GDN_HEADS * GDN_DV
GDN_CONV_W = 2 * GDN_QK + GDN_W
S5_W = 512
S5_P = 16
S5_G = S5_W // S5_P
S5_N = 64
SC_W = 512
FFN_F = 2816
DN_ALPHA = (2 * DEPTH) ** 0.25
LN_EPS = 1e-5
RMS_EPS = 1e-6

SUBLANES = 8
LANES = 128
VMEM_LIMIT_BYTES = 56 * 1024 * 1024

ROWS_LAT = SEQ * BATCH
ROWS_CTX = CTX_LEN * BATCH
SEG_ROW = GRID_W * BATCH
PIECE = 512
CONV_ROWS = 64
FFN_TF = 256
S5_HALF = S5_W // 2
S5_HSTATE = (S5_G // 2) * S5_N
S5_TOK = 64
GDN_CK = 64
GDN_BASE = 16
GDN_UNROLL = 4

F32 = jnp.float32
BF16 = jnp.bfloat16

assert BATCH == SUBLANES


def _dot(a, b):
    return jnp.dot(a, b, preferred_element_type=F32)


def _dot_nt(a, b):
    return lax.dot_general(a, b, (((1,), (1,)), ((), ())), preferred_element_type=F32)


def _dot_tn(a, b):
    return lax.dot_general(a, b, (((0,), (0,)), ((), ())), preferred_element_type=F32)


def _sigmoid(x):
    return 1.0 / (1.0 + jnp.exp(-x))


def _silu(x):
    return x * _sigmoid(x)


def _per_sample(x, vec, op):
    r, c = x.shape
    x3 = x.reshape(r // SUBLANES, SUBLANES, c)
    return op(x3, vec[None]).reshape(r, c)


def _modulate(x, scale, shift):
    y = _per_sample(x, 1.0 + scale, lambda a, b: a * b)
    return _per_sample(y, shift, lambda a, b: a + b)


def _layer_norm(x, g, b):
    mu = jnp.mean(x, axis=-1, keepdims=True)
    xc = x - mu
    var = jnp.mean(xc * xc, axis=-1, keepdims=True)
    return xc * lax.rsqrt(var + LN_EPS) * g + b


def _load_rows(ref, start, n):
    if len(ref.shape) == 2:
        return ref[start:start + n, :]
    rg = ref.shape[1]
    return ref[start // rg:(start + n) // rg, :, :].reshape(n, ref.shape[2])


def _store_rows(ref, start, value):
    n = value.shape[0]
    if len(ref.shape) == 2:
        ref[start:start + n, :] = value
    else:
        rg = ref.shape[1]
        ref[start // rg:(start + n) // rg, :, :] = value.reshape(n // rg, rg, value.shape[1])


def _params(n_axes):
    return pltpu.CompilerParams(dimension_semantics=("arbitrary",) * n_axes,
                                vmem_limit_bytes=VMEM_LIMIT_BYTES)


class _Tiling:
    def __init__(self, kind, rows, tm):
        self.kind, self.rows, self.tm = kind, rows, tm
        self.n_tiles = rows // tm
        if kind == "col":
            self.rg = tm // GRID_H

    def view(self, a):
        if self.kind == "row":
            return a
        return a.reshape(GRID_H, SEG_ROW, a.shape[-1])

    def unview(self, a):
        return a.reshape(self.rows, a.shape[-1])

    def shape(self, c):
        return (self.rows, c) if self.kind == "row" else (GRID_H, SEG_ROW, c)

    def spec(self, c, col=None):
        col = col or (lambda *ij: 0)
        if self.kind == "row":
            return pl.BlockSpec((self.tm, c), lambda *ij: (ij[0], col(*ij)))
        return pl.BlockSpec((GRID_H, self.rg, c), lambda *ij: (0, ij[0], col(*ij)))


def _const_spec(shape):
    nd = len(shape)
    return pl.BlockSpec(shape, lambda *ij: (0,) * nd)


def _ada_kernel(c_ref, w_ref, b_ref, o_ref):
    s = _silu(c_ref[...]).astype(BF16)
    o_ref[...] = _dot(s, w_ref[...].astype(BF16)) + b_ref[...]


def _ada(cvec, ada_w, ada_b):
    return pl.pallas_call(
        _ada_kernel,
        grid=(DEPTH, 6),
        in_specs=[_const_spec((2 * SUBLANES, D_MODEL)),
                  pl.BlockSpec((None, D_MODEL, D_MODEL), lambda i, j: (i, 0, j)),
                  pl.BlockSpec((None, 1, D_MODEL), lambda i, j: (i, 0, j))],
        out_specs=pl.BlockSpec((None, 2 * SUBLANES, D_MODEL), lambda i, j: (i, 0, j)),
        out_shape=jax.ShapeDtypeStruct((DEPTH, 2 * SUBLANES, 6 * D_MODEL), F32),
        compiler_params=_params(2), name="ada",
    )(cvec, ada_w, ada_b.reshape(DEPTH, 1, 6 * D_MODEL))


def _cf_kernel(x_ref, sc_ref, sh_ref, w_ref, cw_ref, g_ref, b_ref, o_ref, pad_ref, *, seg):
    tm = x_ref.shape[0]
    halo = (CF_CONV // 2) * SUBLANES
    zeros = jnp.zeros((halo, CF_W), F32)
    for s in range(tm // seg):
        pad_ref[s, 0:halo, :] = zeros
        pad_ref[s, halo + seg:halo + seg + halo, :] = zeros
    for p in range(tm // PIECE):
        h = _modulate(x_ref[p * PIECE:(p + 1) * PIECE, :], sc_ref[...], sh_ref[...]).astype(BF16)
        y = _dot(h, w_ref[...])
        glu = y[:, :CF_W] * _sigmoid(y[:, CF_W:])
        s, off = divmod(p * PIECE, seg)
        pad_ref[s, halo + off:halo + off + PIECE, :] = glu
    for s in range(tm // seg):
        def body(i, carry, s=s):
            r0 = pl.multiple_of(i * CONV_ROWS, CONV_ROWS)
            acc = jnp.zeros((CONV_ROWS, CF_W), F32)
            for k in range(CF_CONV):
                acc = acc + cw_ref[k:k + 1, :] * pad_ref[s, pl.ds(r0 + k * SUBLANES, CONV_ROWS), :]
            o_ref[pl.ds(s * seg + r0, CONV_ROWS), :] = _silu(_layer_norm(acc, g_ref[...], b_ref[...]))
            return carry
        lax.fori_loop(0, seg // CONV_ROWS, body, 0)


def _cf_mixer(x, sc, sh, w, cw, ln_g, ln_b, *, tm, seg):
    rows = x.shape[0]
    halo = (CF_CONV // 2) * SUBLANES
    return pl.pallas_call(
        functools.partial(_cf_kernel, seg=seg),
        grid=(rows // tm,),
        in_specs=[pl.BlockSpec((tm, D_MODEL), lambda i: (i, 0)),
                  _const_spec((SUBLANES, D_MODEL)), _const_spec((SUBLANES, D_MODEL)),
                  _const_spec((D_MODEL, 2 * CF_W)), _const_spec((CF_CONV, CF_W)),
                  _const_spec((1, CF_W)), _const_spec((1, CF_W))],
        out_specs=pl.BlockSpec((tm, CF_W), lambda i: (i, 0)),
        out_shape=jax.ShapeDtypeStruct((rows, CF_W), F32),
        scratch_shapes=[pltpu.VMEM((tm // seg, seg + 2 * halo, CF_W), F32)],
        compiler_params=_params(1), name="cf_mixer",
    )(x, sc, sh, w, cw, ln_g, ln_b)


def _store_batch_major(dst_ref, stage_ref, value, tok0, n_tok):
    tok0 = pl.multiple_of(tok0, SUBLANES) if not isinstance(tok0, int) else tok0
    for c in range(stage_ref.shape[0]):
        lanes = slice(c * LANES, (c + 1) * LANES)
        stage_ref[c] = value[:, lanes]
        for b in range(BATCH):
            dst_ref[b, pl.ds(tok0, n_tok), lanes] = stage_ref[c, pl.ds(b, n_tok, stride=BATCH), :]


def _load_token_major(src_ref, stage_ref, tok0, n_tok):
    cols = []
    for c in range(stage_ref.shape[0]):
        lanes = slice(c * LANES, (c + 1) * LANES)
        for b in range(BATCH):
            stage_ref[c, pl.ds(b, n_tok, stride=BATCH), :] = src_ref[b, pl.ds(tok0, n_tok), lanes]
        cols.append(stage_ref[c])
    return jnp.concatenate(cols, axis=-1)


def _qkv_kernel(x_ref, sc_ref, sh_ref, w_ref, cw_ref, o_ref, h_ref, pad_ref, stage_ref, *, seg):
    j = pl.program_id(1)
    tm = x_ref.shape[0]
    halo = SUBLANES

    @pl.when(j == 0)
    def _():
        zeros = jnp.zeros((halo, GDN_QK), F32)
        for s in range(tm // seg):
            pad_ref[s, 0:halo, :] = zeros
            pad_ref[s, halo + seg:halo + seg + halo, :] = zeros
        for p in range(tm // PIECE):
            rows = slice(p * PIECE, (p + 1) * PIECE)
            h_ref[rows, :] = _modulate(x_ref[rows, :], sc_ref[...], sh_ref[...]).astype(BF16)

    for p in range(tm // PIECE):
        s, off = divmod(p * PIECE, seg)
        pad_ref[s, halo + off:halo + off + PIECE, :] = _dot(h_ref[p * PIECE:(p + 1) * PIECE, :], w_ref[...])

    is_qk = j < 2
    scale = jnp.where(j == 0, GDN_DK ** -0.5, 1.0).astype(F32)
    blk = 2 * CONV_ROWS
    for s in range(tm // seg):
        def body(i, carry, s=s):
            r0 = pl.multiple_of(i * blk, blk)
            y = (cw_ref[0:1, :] * pad_ref[s, pl.ds(r0, blk), :]
                 + cw_ref[1:2, :] * pad_ref[s, pl.ds(r0 + halo, blk), :]
                 + cw_ref[2:3, :] * pad_ref[s, pl.ds(r0 + 2 * halo, blk), :])
            y = _silu(y)
            heads = []
            for hh in range(GDN_HEADS):
                t = y[:, hh * GDN_DK:(hh + 1) * GDN_DK]
                nrm = t * (lax.rsqrt(jnp.sum(t * t, axis=-1, keepdims=True) + RMS_EPS) * scale)
                heads.append(jnp.where(is_qk, nrm, t))
            _store_batch_major(o_ref, stage_ref, jnp.concatenate(heads, axis=-1),
                               (s * seg) // SUBLANES + i * (blk // SUBLANES), blk // SUBLANES)
            return carry
        lax.fori_loop(0, seg // blk, body, 0)


def _qkv_proj(x, sc, sh, w, cw, *, tm, seg):
    rows = x.shape[0]
    return pl.pallas_call(
        functools.partial(_qkv_kernel, seg=seg),
        grid=(rows // tm, 3),
        in_specs=[pl.BlockSpec((tm, D_MODEL), lambda i, j: (i, 0)),
                  _const_spec((SUBLANES, D_MODEL)), _const_spec((SUBLANES, D_MODEL)),
                  pl.BlockSpec((D_MODEL, GDN_QK), lambda i, j: (0, j)),
                  pl.BlockSpec((3, GDN_QK), lambda i, j: (0, j))],
        out_specs=pl.BlockSpec((BATCH, tm // BATCH, GDN_QK), lambda i, j: (0, i, j)),
        out_shape=jax.ShapeDtypeStruct((BATCH, rows // BATCH, GDN_CONV_W), F32),
        scratch_shapes=[pltpu.VMEM((tm, D_MODEL), BF16),
                        pltpu.VMEM((tm // seg, seg + 2 * SUBLANES, GDN_QK), F32),
                        pltpu.VMEM((GDN_QK // LANES, 2 * CONV_ROWS, LANES), F32)],
        compiler_params=_params(2), name="qkv_proj",
    )(x, sc, sh, w, cw)


def _zbg_kernel(x_ref, sc_ref, sh_ref, w_ref, alog_ref, dtb_ref, z_ref, bg_ref, stage_ref):
    tm = x_ref.shape[0]
    for p in range(tm // PIECE):
        rows = slice(p * PIECE, (p + 1) * PIECE)
        h = _modulate(x_ref[rows, :], sc_ref[...], sh_ref[...]).astype(BF16)
        y = _dot(h, w_ref[...])
        _store_batch_major(z_ref, stage_ref, y[:, :GDN_W], p * (PIECE // BATCH), PIECE // BATCH)
        s = y[:, GDN_W:]
        t = s + dtb_ref[...]
        softplus = jnp.maximum(t, 0.0) + jnp.log1p(jnp.exp(-jnp.abs(t)))
        g = -jnp.exp(alog_ref[...]) * softplus
        lane = lax.broadcasted_iota(jnp.int32, s.shape, 1)
        bg_ref[rows, :] = jnp.where(lane < 2 * GDN_HEADS, _sigmoid(s), g)


def _zbg_proj(x, sc, sh, w, alog, dtb, *, tm):
    rows = x.shape[0]
    return pl.pallas_call(
        _zbg_kernel,
        grid=(rows // tm,),
        in_specs=[pl.BlockSpec((tm, D_MODEL), lambda i: (i, 0)),
                  _const_spec((SUBLANES, D_MODEL)), _const_spec((SUBLANES, D_MODEL)),
                  _const_spec((D_MODEL, GDN_W + LANES)), _const_spec((1, LANES)), _const_spec((1, LANES))],
        out_specs=[pl.BlockSpec((BATCH, tm // BATCH, GDN_W), lambda i: (0, i, 0)),
                   pl.BlockSpec((tm, LANES), lambda i: (i, 0))],
        out_shape=[jax.ShapeDtypeStruct((BATCH, rows // BATCH, GDN_W), F32), jax.ShapeDtypeStruct((rows, LANES), F32)],
        scratch_shapes=[pltpu.VMEM((GDN_W // LANES, PIECE, LANES), F32)],
        compiler_params=_params(1), name="zbg_proj",
    )(x, sc, sh, w, alog, dtb)


def _gdn_decays(beta_row, g_row, rev):
    c = GDN_CK
    ri = lax.broadcasted_iota(jnp.int32, (c, c), 0)
    ci = lax.broadcasted_iota(jnp.int32, (c, c), 1)
    eye = ri == ci
    incl = (ri <= ci) if rev else (ri >= ci)
    incl_t = (ci <= ri) if rev else (ci >= ri)
    strict = (ri < ci) if rev else (ri > ci)
    g_b = jnp.broadcast_to(g_row, (c, c))
    beta_b = jnp.broadcast_to(beta_row, (c, c))
    g_col = jnp.sum(jnp.where(eye, g_b, 0.0), axis=1, keepdims=True)
    beta_col = jnp.sum(jnp.where(eye, beta_b, 0.0), axis=1, keepdims=True)
    gc_col = jnp.sum(jnp.where(incl, g_b, 0.0), axis=1, keepdims=True)
    gc_row = jnp.sum(jnp.where(incl_t, g_col, 0.0), axis=0, keepdims=True)
    g_sum = jnp.sum(g_row, axis=1, keepdims=True)
    decay = jnp.exp(jnp.where(incl, gc_col - gc_row, -1e30))
    return dict(strict=strict, beta_col=beta_col, decay=decay, e_col=jnp.exp(gc_col),
                e_end=jnp.exp(g_sum - gc_col), g_tot=jnp.exp(g_sum))


def _gdn_chunk_terms(chains):
    c = GDN_CK
    ri = lax.broadcasted_iota(jnp.int32, (c, c), 0)
    ci = lax.broadcasted_iota(jnp.int32, (c, c), 1)
    same = lambda s: (ri // s) == (ci // s)
    nils = [jnp.where(ch["strict"], -(ch["beta_col"] * ch["kk"] * ch["decay"]), 0.0) for ch in chains]
    nds = [jnp.where(same(GDN_BASE), n, 0.0) for n in nils]
    invs = [jnp.where(ri == ci, 1.0, 0.0) + nd for nd in nds]
    p16s = [nd.astype(BF16) for nd in nds]
    for _ in range(int(math.log2(GDN_BASE)) - 1):
        p16s = [_dot(p16, p16).astype(BF16) for p16 in p16s]
        invs = [inv + _dot(p16, inv.astype(BF16)) for p16, inv in zip(p16s, invs)]
    s = GDN_BASE
    while s < c:
        off = jnp.logical_and(same(2 * s), jnp.logical_not(same(s)))
        inv16s = [inv.astype(BF16) for inv in invs]
        m16s = [_dot(inv16, jnp.where(off, n, 0.0).astype(BF16)).astype(BF16) for inv16, n in zip(inv16s, nils)]
        invs = [inv + _dot(m16, inv16) for inv, m16, inv16 in zip(invs, m16s, inv16s)]
        s *= 2
    y16s = [_dot(inv.astype(BF16),
                 jnp.concatenate([ch["v"] * ch["beta_col"], ch["k"] * (ch["beta_col"] * ch["e_col"])], axis=-1).astype(BF16)
                 ).astype(BF16) for inv, ch in zip(invs, chains)]
    kts = [_dot_tn((ch["k"] * ch["e_end"]).astype(BF16), y16) for ch, y16 in zip(chains, y16s)]
    qys = [_dot((ch["qk_raw"] * ch["decay"]).astype(BF16), y16) for ch, y16 in zip(chains, y16s)]
    out = []
    for ch, kt, qy in zip(chains, kts, qys):
        q_eff = ch["q"] * ch["e_col"] - qy[:, GDN_DV:]
        out.append(((-kt[:, GDN_DV:]).astype(BF16), kt[:, :GDN_DV], q_eff.astype(BF16), qy[:, :GDN_DV],
                    jnp.broadcast_to(ch["g_tot"], (1, GDN_DV))))
    return out


def _gdn_kernel(qc, kc, vc, zc, bgc, ql, kl, vl, zl, bgl, ng_ref, oc_ref, ol_ref,
                a_ref, b_ref, qe_ref, gt_ref, s_ref):
    seqs = ((qc, kc, vc, zc, bgc, oc_ref, 0), (ql, kl, vl, zl, bgl, ol_ref, CTX_LEN // GDN_CK))

    for q, k, v, z, bg, o, base in seqs:
        nc = q.shape[0] // GDN_CK

        def prepare(i, carry, q=q, k=k, v=v, bg=bg, o=o, base=base):
            chains = []
            for uu in range(GDN_UNROLL):
                cc = i * GDN_UNROLL + uu
                rows = pl.ds(pl.multiple_of(cc * GDN_CK, GDN_CK), GDN_CK)
                qv, kv, vv = q[rows, :], k[rows, :], v[rows, :]
                k16 = kv.astype(BF16)
                kk = _dot_nt(k16, k16)
                qk_raw = _dot_nt(qv.astype(BF16), k16)
                for d in range(2):
                    chains.append(dict(q=qv, k=kv, v=vv, kk=kk, qk_raw=qk_raw, cc=cc, rows=rows, d=d,
                                       **_gdn_decays(bg[d, pl.ds(cc, 1), :], bg[2 + d, pl.ds(cc, 1), :], rev=(d == 1))))
            terms = _gdn_chunk_terms(chains)
            for ch, (a_neg, b_mat, q_eff, o_d, g_tot) in zip(chains, terms):
                d, cc = ch["d"], ch["cc"]
                a_ref[d, base + cc] = a_neg
                b_ref[d, base + cc] = b_mat
                qe_ref[d, base + cc] = q_eff
                gt_ref[d, base + cc] = g_tot
                if d == 0:
                    o_fwd = o_d
                else:
                    o[ch["rows"], :] = o_fwd + o_d
            return carry
        lax.fori_loop(0, nc // GDN_UNROLL, prepare, 0)

    s_ref[...] = jnp.zeros(s_ref.shape, F32)
    for q, k, v, z, bg, o, base in seqs:
        nc = q.shape[0] // GDN_CK

        def advance(i, carry, o=o, base=base, nc=nc):
            for d in range(2):
                cc = i if d == 0 else nc - 1 - i
                rows = pl.ds(pl.multiple_of(cc * GDN_CK, GDN_CK), GDN_CK)
                s = s_ref[d]
                s16 = s.astype(BF16)
                o[rows, :] = o[rows, :] + _dot(qe_ref[d, base + cc], s16)
                s_ref[d] = s * gt_ref[d, base + cc] + _dot(a_ref[d, base + cc], s16) + b_ref[d, base + cc]
            return carry
        lax.fori_loop(0, nc, advance, 0)

    for q, k, v, z, bg, o, base in seqs:
        blk = 256
        for r in range(q.shape[0] // blk):
            rows = slice(r * blk, (r + 1) * blk)
            ov = o[rows, :]
            ov = ov * lax.rsqrt(jnp.mean(ov * ov, axis=-1, keepdims=True) + RMS_EPS) * ng_ref[...]
            o[rows, :] = ov * _silu(z[rows, :])


def _gdn(qkv_c, z_c, bg_c, qkv_l, z_l, bg_l, norm_g):
    def tok_spec(n_tok, col0):
        return pl.BlockSpec((None, n_tok, GDN_DK), lambda b, h: (b, 0, col0 + h))

    def bg_spec(n_tok):
        return pl.BlockSpec((None, None, 4, n_tok // GDN_CK, GDN_CK), lambda b, h: (b, h, 0, 0, 0))

    in_specs = []
    for n_tok in (CTX_LEN, SEQ):
        in_specs += [tok_spec(n_tok, 0), tok_spec(n_tok, GDN_HEADS), tok_spec(n_tok, 2 * GDN_HEADS),
                     tok_spec(n_tok, 0), bg_spec(n_tok)]
    in_specs.append(_const_spec((1, GDN_DV)))
    n_chunks = (CTX_LEN + SEQ) // GDN_CK
    return pl.pallas_call(
        _gdn_kernel,
        grid=(BATCH, GDN_HEADS),
        in_specs=in_specs,
        out_specs=[tok_spec(CTX_LEN, 0), tok_spec(SEQ, 0)],
        out_shape=[jax.ShapeDtypeStruct((BATCH, CTX_LEN, GDN_W), F32),
                   jax.ShapeDtypeStruct((BATCH, SEQ, GDN_W), F32)],
        scratch_shapes=[pltpu.VMEM((2, n_chunks, GDN_DK, GDN_DV), BF16),
                        pltpu.VMEM((2, n_chunks, GDN_DK, GDN_DV), F32),
                        pltpu.VMEM((2, n_chunks, GDN_CK, GDN_DK), BF16),
                        pltpu.VMEM((2, n_chunks, 1, GDN_DV), F32),
                        pltpu.VMEM((2, GDN_DK, GDN_DV), F32)],
        compiler_params=_params(2), name="gdn",
    )(qkv_c, qkv_c, qkv_c, z_c, bg_c, qkv_l, qkv_l, qkv_l, z_l, bg_l, norm_g)


def _residual_norm(x, branch, gate, g, b):
    return _layer_norm(DN_ALPHA * x + _per_sample(branch, gate, lambda a, c: a * c), g, b)


def _out_even_kernel(x_ref, a_ref, o_ref, g1_ref, wa_ref, wo_ref, lng_ref, lnb_ref, out_ref, stage_ref):
    tm = x_ref.shape[0]
    for p in range(tm // PIECE):
        rows = slice(p * PIECE, (p + 1) * PIECE)
        o = _load_token_major(o_ref, stage_ref, p * (PIECE // BATCH), PIECE // BATCH)
        m = _dot(a_ref[rows, :].astype(BF16), wa_ref[...]) + _dot(o.astype(BF16), wo_ref[...])
        out_ref[rows, :] = _residual_norm(x_ref[rows, :], m, g1_ref[...], lng_ref[...], lnb_ref[...])


def _out_even(x, a, o, g1, wa, wo, ln_g, ln_b, *, tm):
    rows = x.shape[0]
    row = lambda c: pl.BlockSpec((tm, c), lambda i: (i, 0))
    return pl.pallas_call(
        _out_even_kernel,
        grid=(rows // tm,),
        in_specs=[row(D_MODEL), row(CF_W), pl.BlockSpec((BATCH, tm // BATCH, GDN_W), lambda i: (0, i, 0)),
                  _const_spec((SUBLANES, D_MODEL)),
                  _const_spec((CF_W, D_MODEL)), _const_spec((GDN_W, D_MODEL)),
                  _const_spec((1, D_MODEL)), _const_spec((1, D_MODEL))],
        out_specs=row(D_MODEL),
        out_shape=jax.ShapeDtypeStruct((rows, D_MODEL), F32),
        scratch_shapes=[pltpu.VMEM((GDN_W // LANES, PIECE, LANES), F32)],
        compiler_params=_params(1), name="out_even",
    )(x, a, o, g1, wa, wo, ln_g, ln_b)


def _gelu_tanh(x):
    return 0.5 * x * (1.0 + jnp.tanh(math.sqrt(2.0 / math.pi) * (x + 0.044715 * (x * x * x))))


def _out_odd_kernel(x_ref, u_ref, yf_ref, yb_ref, s_ref, g1_ref, d_ref, gw_ref, gb_ref, wa_ref, wo_ref,
                    lng_ref, lnb_ref, out_ref):
    tm = x_ref.shape[0]
    for p in range(tm // PIECE):
        rows = slice(p * PIECE, (p + 1) * PIECE)
        y = d_ref[...] * u_ref[rows, :] + yf_ref[rows, :] + yb_ref[rows, :]
        zg = _gelu_tanh(y)
        s5 = zg * _sigmoid(_dot(zg.astype(BF16), gw_ref[...]) + gb_ref[...])
        m = _dot(s5.astype(BF16), wa_ref[...]) + _dot(s_ref[rows, :].astype(BF16), wo_ref[...])
        out_ref[rows, :] = _residual_norm(x_ref[rows, :], m, g1_ref[...], lng_ref[...], lnb_ref[...])


def _out_odd(x, u, yf, yb, s, g1, d_skip, glu_w, glu_b, wa, wo, ln_g, ln_b, *, tm):
    rows = x.shape[0]
    row = lambda c: pl.BlockSpec((tm, c), lambda i: (i, 0))
    return pl.pallas_call(
        _out_odd_kernel,
        grid=(rows // tm,),
        in_specs=[row(D_MODEL), row(S5_W), row(S5_W), row(S5_W), row(SC_W), _const_spec((SUBLANES, D_MODEL)),
                  _const_spec((1, S5_W)), _const_spec((S5_W, S5_W)), _const_spec((1, S5_W)),
                  _const_spec((S5_W, D_MODEL)), _const_spec((SC_W, D_MODEL)),
                  _const_spec((1, D_MODEL)), _const_spec((1, D_MODEL))],
        out_specs=row(D_MODEL),
        out_shape=jax.ShapeDtypeStruct((rows, D_MODEL), F32),
        compiler_params=_params(1), name="out_odd",
    )(x, u, yf, yb, s, g1, d_skip, glu_w, glu_b, wa, wo, ln_g, ln_b)


def _conv3(pad_ref, s, r0, n, cw_ref, shift):
    return (cw_ref[0:1, :] * pad_ref[s, r0:r0 + n, :]
            + cw_ref[1:2, :] * pad_ref[s, r0 + shift:r0 + shift + n, :]
            + cw_ref[2:3, :] * pad_ref[s, r0 + 2 * shift:r0 + 2 * shift + n, :])


def _ffn_kernel(x_ref, sc_ref, sh_ref, g2_ref, wv_ref, wg_ref, cw_ref, wd_ref, lng_ref, lnb_ref, out_ref,
                h_ref, val_ref, pad_ref, *, tm, shift, seg):
    j = pl.program_id(1)
    n_pieces = tm // PIECE

    @pl.when(j == 0)
    def _():
        zeros = jnp.zeros((shift, FFN_TF), F32)
        for s in range(tm // seg):
            pad_ref[s, 0:shift, :] = zeros
            pad_ref[s, shift + seg:shift + seg + shift, :] = zeros
        for p in range(n_pieces):
            x = _load_rows(x_ref, p * PIECE, PIECE)
            h_ref[p * PIECE:(p + 1) * PIECE, :] = _modulate(x, sc_ref[...], sh_ref[...]).astype(BF16)

    for p in range(n_pieces):
        h = h_ref[p * PIECE:(p + 1) * PIECE, :]
        val_ref[p * PIECE:(p + 1) * PIECE, :] = _dot(h, wv_ref[...])
        s, off = divmod(p * PIECE, seg)
        pad_ref[s, shift + off:shift + off + PIECE, :] = _dot(h, wg_ref[...])

    for p in range(n_pieces):
        s, off = divmod(p * PIECE, seg)
        gate = _conv3(pad_ref, s, off, PIECE, cw_ref, shift)
        act = (val_ref[p * PIECE:(p + 1) * PIECE, :] * _silu(gate)).astype(BF16)
        part = _dot(act, wd_ref[...])

        @pl.when(j == 0)
        def _():
            _store_rows(out_ref, p * PIECE, part)

        @pl.when(j > 0)
        def _():
            _store_rows(out_ref, p * PIECE, _load_rows(out_ref, p * PIECE, PIECE) + part)

    @pl.when(j == pl.num_programs(1) - 1)
    def _():
        for p in range(n_pieces):
            x = _load_rows(x_ref, p * PIECE, PIECE)
            y = _residual_norm(x, _load_rows(out_ref, p * PIECE, PIECE), g2_ref[...], lng_ref[...], lnb_ref[...])
            _store_rows(out_ref, p * PIECE, y)


def _ffn(x, sc, sh, g2, w_up, cw, w_down, ln_g, ln_b, *, tiling, shift, seg):
    tm = tiling.tm
    n_f = FFN_F // FFN_TF
    out = pl.pallas_call(
        functools.partial(_ffn_kernel, tm=tm, shift=shift, seg=seg),
        grid=(tiling.n_tiles, n_f),
        in_specs=[tiling.spec(D_MODEL),
                  _const_spec((SUBLANES, D_MODEL)), _const_spec((SUBLANES, D_MODEL)), _const_spec((SUBLANES, D_MODEL)),
                  pl.BlockSpec((D_MODEL, FFN_TF), lambda i, j: (0, j)),
                  pl.BlockSpec((D_MODEL, FFN_TF), lambda i, j: (0, n_f + j)),
                  pl.BlockSpec((3, FFN_TF), lambda i, j: (0, j)),
                  pl.BlockSpec((FFN_TF, D_MODEL), lambda i, j: (j, 0)),
                  _const_spec((1, D_MODEL)), _const_spec((1, D_MODEL))],
        out_specs=tiling.spec(D_MODEL),
        out_shape=jax.ShapeDtypeStruct(tiling.shape(D_MODEL), F32),
        scratch_shapes=[pltpu.VMEM((tm, D_MODEL), BF16), pltpu.VMEM((tm, FFN_TF), F32),
                        pltpu.VMEM((tm // seg, seg + 2 * shift, FFN_TF), F32)],
        compiler_params=_params(2), name="conv_ffn",
    )(tiling.view(x), sc, sh, g2, w_up, w_up, cw, w_down, ln_g, ln_b)
    return tiling.unview(out)


def _odd_in_kernel(x_ref, sc_ref, sh_ref, wu_ref, wb_ref, wc_ref, wx_ref, cw_ref, u_ref, s_ref,
                   h_ref, b_ref, pad_ref, *, tm, shift, seg):
    j = pl.program_id(1)
    n_pieces = tm // PIECE
    cb = u_ref.shape[-1]

    @pl.when(j == 0)
    def _():
        zeros = jnp.zeros((shift, cb), F32)
        for s in range(tm // seg):
            pad_ref[s, 0:shift, :] = zeros
            pad_ref[s, shift + seg:shift + seg + shift, :] = zeros
        for p in range(n_pieces):
            x = _load_rows(x_ref, p * PIECE, PIECE)
            h_ref[p * PIECE:(p + 1) * PIECE, :] = _modulate(x, sc_ref[...], sh_ref[...]).astype(BF16)

    for p in range(n_pieces):
        h = h_ref[p * PIECE:(p + 1) * PIECE, :]
        _store_rows(u_ref, p * PIECE, _dot(h, wu_ref[...]))
        b_ref[p * PIECE:(p + 1) * PIECE, :] = _dot(h, wb_ref[...])
        s, off = divmod(p * PIECE, seg)
        pad_ref[s, shift + off:shift + off + PIECE, :] = _dot(h, wc_ref[...]) * _dot(h, wx_ref[...])

    for p in range(n_pieces):
        s, off = divmod(p * PIECE, seg)
        _store_rows(s_ref, p * PIECE, b_ref[p * PIECE:(p + 1) * PIECE, :] * _conv3(pad_ref, s, off, PIECE, cw_ref, shift))


def _odd_in(x, sc, sh, w_in, cw, *, tiling, shift, seg):
    tm = tiling.tm
    cb = 2 * LANES
    nb = S5_W // cb
    wspec = lambda k: pl.BlockSpec((D_MODEL, cb), lambda i, j: (0, k * nb + j))
    u, s = pl.pallas_call(
        functools.partial(_odd_in_kernel, tm=tm, shift=shift, seg=seg),
        grid=(tiling.n_tiles, nb),
        in_specs=[tiling.spec(D_MODEL), _const_spec((SUBLANES, D_MODEL)), _const_spec((SUBLANES, D_MODEL)),
                  wspec(0), wspec(1), wspec(2), wspec(3),
                  pl.BlockSpec((3, cb), lambda i, j: (0, j))],
        out_specs=[tiling.spec(cb, lambda i, j: j), tiling.spec(cb, lambda i, j: j)],
        out_shape=[jax.ShapeDtypeStruct(tiling.shape(S5_W), F32), jax.ShapeDtypeStruct(tiling.shape(SC_W), F32)],
        scratch_shapes=[pltpu.VMEM((tm, D_MODEL), BF16), pltpu.VMEM((tm, cb), F32),
                        pltpu.VMEM((tm // seg, seg + 2 * shift, cb), F32)],
        compiler_params=_params(2), name="odd_in",
    )(tiling.view(x), sc, sh, w_in, w_in, w_in, w_in, cw)
    return tiling.unview(u), tiling.unview(s)


def _s5_kernel(uf_ref, ub_ref, wb_ref, wc_ref, a_ref, x0_ref, yf_ref, yb_ref, xfin_ref, bu_ref, st_ref):
    t = pl.program_id(0)
    rows = uf_ref.shape[0]
    n_tok = rows // SUBLANES
    hs = S5_HSTATE

    @pl.when(t == 0)
    def _():
        st_ref[...] = x0_ref[...]

    for d, u_ref in enumerate((uf_ref, ub_ref)):
        for hf in range(2):
            u = u_ref[:, hf * S5_HALF:(hf + 1) * S5_HALF].astype(BF16)
            bu_ref[d, :, hf * 2 * hs:(hf + 1) * 2 * hs] = _dot(u, wb_ref[d, hf])

    for d in range(2):
        for hf in range(2):
            c_re = hf * 2 * hs
            c_im = c_re + hs
            a_re = jnp.broadcast_to(a_ref[d, hf, 0:1, :], (SUBLANES, hs))
            a_im = jnp.broadcast_to(a_ref[d, hf, 1:2, :], (SUBLANES, hs))

            def body(i, carry, d=d, c_re=c_re, c_im=c_im, a_re=a_re, a_im=a_im):
                xr, xi = carry
                tok = i if d == 0 else n_tok - 1 - i
                r = pl.ds(pl.multiple_of(tok * SUBLANES, SUBLANES), SUBLANES)
                nr = a_re * xr - a_im * xi + bu_ref[d, r, c_re:c_re + hs]
                ni = a_re * xi + a_im * xr + bu_ref[d, r, c_im:c_im + hs]
                bu_ref[d, r, c_re:c_re + hs] = nr
                bu_ref[d, r, c_im:c_im + hs] = ni
                return nr, ni

            xr, xi = lax.fori_loop(0, n_tok, body, (st_ref[d, :, c_re:c_re + hs], st_ref[d, :, c_im:c_im + hs]))
            st_ref[d, :, c_re:c_re + hs] = xr
            st_ref[d, :, c_im:c_im + hs] = xi

    for d, y_ref in enumerate((yf_ref, yb_ref)):
        for hf in range(2):
            xs = bu_ref[d, :, hf * 2 * hs:(hf + 1) * 2 * hs].astype(BF16)
            y_ref[:, hf * S5_HALF:(hf + 1) * S5_HALF] = _dot(xs, wc_ref[d, hf])

    @pl.when(t == pl.num_programs(0) - 1)
    def _():
        xfin_ref[...] = st_ref[...]


def _s5(u, wb, wc, a, x0):
    rows = u.shape[0]
    tr = S5_TOK * SUBLANES
    nt = rows // tr
    state = (2, SUBLANES, 4 * S5_HSTATE)
    return pl.pallas_call(
        _s5_kernel,
        grid=(nt,),
        in_specs=[pl.BlockSpec((tr, S5_W), lambda t: (t, 0)),
                  pl.BlockSpec((tr, S5_W), lambda t: (nt - 1 - t, 0)),
                  _const_spec((2, 2, S5_HALF, 2 * S5_HSTATE)), _const_spec((2, 2, 2 * S5_HSTATE, S5_HALF)),
                  _const_spec((2, 2, 2, S5_HSTATE)), _const_spec(state)],
        out_specs=[pl.BlockSpec((tr, S5_W), lambda t: (t, 0)),
                   pl.BlockSpec((tr, S5_W), lambda t: (nt - 1 - t, 0)),
                   _const_spec(state)],
        out_shape=[jax.ShapeDtypeStruct((rows, S5_W), F32), jax.ShapeDtypeStruct((rows, S5_W), F32),
                   jax.ShapeDtypeStruct(state, F32)],
        scratch_shapes=[pltpu.VMEM((2, tr, 4 * S5_HSTATE), F32), pltpu.VMEM(state, F32)],
        compiler_params=_params(1), name="s5_scan",
    )(u, u, wb, wc, a, x0)


def _s5_weights(lam_re, lam_im, log_dt, b_re, b_im, c_re, c_im):
    eye = jnp.eye(S5_G // 2, dtype=F32)
    wbs, wcs, avs = [], [], []
    for di in range(2):
        lr, li = lam_re[di], lam_im[di]
        dt = jnp.exp(log_dt[di])[:, None]
        mag = jnp.exp(lr * dt)
        ar, ai = mag * jnp.cos(li * dt), mag * jnp.sin(li * dt)
        den = lr * lr + li * li
        fr = ((ar - 1.0) * lr + ai * li) / den
        fi = (ai * lr - (ar - 1.0) * li) / den
        bbr = fr[..., None] * b_re - fi[..., None] * b_im
        bbi = fr[..., None] * b_im + fi[..., None] * b_re

        def in_block(t):
            t = t.reshape(2, S5_G // 2, S5_N, S5_P)
            return jnp.einsum("hgnp,gk->hgpkn", t, eye).reshape(2, S5_HALF, S5_HSTATE)

        def out_block(t):
            t = t.reshape(2, S5_G // 2, S5_P, S5_N)
            return jnp.einsum("hgpn,gk->hgnkp", t, eye).reshape(2, S5_HSTATE, S5_HALF)

        wbs.append(jnp.concatenate([in_block(bbr), in_block(bbi)], axis=-1))
        wcs.append(jnp.concatenate([out_block(c_re), out_block(-c_im)], axis=1))
        avs.append(jnp.stack([ar.reshape(2, S5_HSTATE), ai.reshape(2, S5_HSTATE)], axis=1))
    return jnp.stack(wbs).astype(BF16), jnp.stack(wcs).astype(BF16), jnp.stack(avs)


def _bg_rows(bg, n_tok):
    t = bg.reshape(n_tok, BATCH, LANES)[:, :, :4 * GDN_HEADS].reshape(n_tok, BATCH, 4, GDN_HEADS)
    return t.transpose(1, 3, 2, 0).reshape(BATCH, GDN_HEADS, 4, n_tok // GDN_CK, GDN_CK)


def _to_batch_major(a, n_tok):
    return a.reshape(n_tok, BATCH, a.shape[-1]).transpose(1, 0, 2)


def _to_token_major(a):
    return a.transpose(1, 0, 2).reshape(a.shape[0] * a.shape[1], a.shape[2])


def kernel(x, c, ctx, c_ctx, ada_w, ada_b, ln_g, ln_b, ev_w_in, ev_w_out, cf_conv, cf_ln_g, cf_ln_b, gdn_conv,
           gdn_a_log, gdn_dt_bias, gdn_norm_g, od_w_in, od_w_out, s5_lam_re, s5_lam_im, s5_log_dt, s5_b_re,
           s5_b_im, s5_c_re, s5_c_im, s5_d, s5_glu_w, s5_glu_b, sc_conv, ffn_w_up, ffn_conv, ffn_w_down):
    xl = _to_token_major(x)
    cl = _to_token_major(ctx)

    cvec = jnp.zeros((2 * SUBLANES, D_MODEL), F32).at[:BATCH].set(c).at[BATCH].set(c_ctx)
    mods = _ada(cvec, ada_w, ada_b)

    lat_row = _Tiling("row", ROWS_LAT, 2 * SEG_ROW)
    lat_col = _Tiling("col", ROWS_LAT, 2 * SEG_ROW)
    ctx_row = _Tiling("row", ROWS_CTX, ROWS_CTX)
    lat_along_row = dict(tiling=lat_row, shift=SUBLANES, seg=SEG_ROW)
    lat_along_col = dict(tiling=lat_col, shift=lat_col.rg, seg=lat_col.tm)
    ctx_conv = dict(tiling=ctx_row, shift=SUBLANES, seg=ROWS_CTX)

    row2 = lambda v: v.reshape(1, -1)
    for i in range(DEPTH):
        j = i // 2
        last = i == DEPTH - 1
        m_lat = jnp.split(mods[i, :BATCH], 6, axis=-1)
        m_ctx = jnp.split(jnp.broadcast_to(mods[i, BATCH:BATCH + 1], (BATCH, 6 * D_MODEL)), 6, axis=-1)
        lng1, lnb1, lng2, lnb2 = row2(ln_g[i, 0]), row2(ln_b[i, 0]), row2(ln_g[i, 1]), row2(ln_b[i, 1])
        w_up, w_down, f_cw = ffn_w_up[i].astype(BF16), ffn_w_down[i].astype(BF16), ffn_conv[i]

        if i % 2 == 0:
            w_in = ev_w_in[j]
            w_cf = w_in[:, :2 * CF_W].astype(BF16)
            w_qkv = w_in[:, 2 * CF_W:2 * CF_W + GDN_CONV_W].astype(BF16)
            w_zbg = jnp.pad(w_in[:, 2 * CF_W + GDN_CONV_W:], ((0, 0), (0, LANES - 4 * GDN_HEADS))).astype(BF16)
            pad16 = lambda v: jnp.pad(v.reshape(1, -1), ((0, 0), (2 * GDN_HEADS, LANES - 4 * GDN_HEADS)))
            alog, dtb = pad16(gdn_a_log[j]), pad16(gdn_dt_bias[j])
            w_out = ev_w_out[j].astype(BF16)
            parts = []
            for xs, m, tm, seg, n_tok in ((cl, m_ctx, ROWS_CTX, ROWS_CTX, CTX_LEN), (xl, m_lat, 4 * SEG_ROW, SEG_ROW, SEQ)):
                sh1, sc1 = m[0], m[1]
                a = _cf_mixer(xs, sc1, sh1, w_cf, cf_conv[j], row2(cf_ln_g[j]), row2(cf_ln_b[j]), tm=tm, seg=seg)
                qkv = _qkv_proj(xs, sc1, sh1, w_qkv, gdn_conv[j], tm=tm, seg=seg)
                z, bg = _zbg_proj(xs, sc1, sh1, w_zbg, alog, dtb, tm=tm)
                parts.append((a, qkv, z, _bg_rows(bg, n_tok)))
            (a_c, qkv_c, z_c, bg_c), (a_l, qkv_l, z_l, bg_l) = parts
            o_c, o_l = _gdn(qkv_c, z_c, bg_c, qkv_l, z_l, bg_l, row2(gdn_norm_g[j]))
            x1 = _out_even(xl, a_l, o_l, m_lat[2], w_out[:CF_W], w_out[CF_W:], lng1, lnb1, tm=2 * SEG_ROW)
            if not last:
                c1 = _out_even(cl, a_c, o_c, m_ctx[2], w_out[:CF_W], w_out[CF_W:], lng1, lnb1, tm=2 * SEG_ROW)
            lat_ffn = lat_along_col
        else:
            w_in = od_w_in[j].astype(BF16)
            w_out = od_w_out[j].astype(BF16)
            wb, wc, av = _s5_weights(s5_lam_re[j], s5_lam_im[j], s5_log_dt[j], s5_b_re[j], s5_b_im[j],
                                     s5_c_re[j], s5_c_im[j])
            u_c, s_c = _odd_in(cl, m_ctx[1], m_ctx[0], w_in, sc_conv[j], **ctx_conv)
            u_l, s_l = _odd_in(xl, m_lat[1], m_lat[0], w_in, sc_conv[j], **lat_along_col)
            zero_state = jnp.zeros((2, SUBLANES, 4 * S5_HSTATE), F32)
            yf_c, yb_c, fin_c = _s5(u_c, wb, wc, av, zero_state)
            yf_l, yb_l, _ = _s5(u_l, wb, wc, av, fin_c)
            odd_w = (row2(s5_d[j]), s5_glu_w[j].astype(BF16), row2(s5_glu_b[j]), w_out[:S5_W], w_out[S5_W:])
            x1 = _out_odd(xl, u_l, yf_l, yb_l, s_l, m_lat[2], *odd_w, lng1, lnb1, tm=2 * SEG_ROW)
            if not last:
                c1 = _out_odd(cl, u_c, yf_c, yb_c, s_c, m_ctx[2], *odd_w, lng1, lnb1, tm=2 * SEG_ROW)
            lat_ffn = lat_along_row

        xl = _ffn(x1, m_lat[4], m_lat[3], m_lat[5], w_up, f_cw, w_down, lng2, lnb2, **lat_ffn)
        if not last:
            cl = _ffn(c1, m_ctx[4], m_ctx[3], m_ctx[5], w_up, f_cw, w_down, lng2, lnb2, **ctx_conv)

    return xl.reshape(SEQ, BATCH, D_MODEL).transpose(1, 0, 2)
```

```python
import functools
import math

import jax
import jax.numpy as jnp
from jax import lax
from jax.experimental import pallas as pl
from jax.experimental.pallas import tpu as pltpu

D_MODEL = 1024
BATCH = 8
SEQ = 2048
DEPTH = 4
GRID_W = 64
GRID_H = SEQ // GRID_W
CTX_LEN = 256
CF_W = 512
CF_CONV = 31
GDN_HEADS = 4
GDN_DK = 128
GDN_DV = 128
GDN_QK = GDN_HEADS * GDN_DK
GDN_W = GDN_HEADS * GDN_DV
GDN_CONV_W = 2 * GDN_QK + GDN_W
S5_W = 512
S5_P = 16
S5_G = S5_W // S5_P
S5_N = 64
SC_W = 512
FFN_F = 2816
DN_ALPHA = (2 * DEPTH) ** 0.25
LN_EPS = 1e-5
RMS_EPS = 1e-6

SUBLANES = 8
LANES = 128
VMEM_LIMIT_BYTES = 56 * 1024 * 1024

ROWS_LAT = SEQ * BATCH
ROWS_CTX = CTX_LEN * BATCH
SEG_ROW = GRID_W * BATCH
PIECE = 512
CONV_ROWS = 64
FFN_TF_LAT = FFN_F // 2
FFN_TF_CTX = 256
S5_HALF = S5_W // 2
S5_HSTATE = (S5_G // 2) * S5_N
S5_TOK = 64
GDN_CK = 64
GDN_BASE = 16
GDN_NCHUNK = (CTX_LEN + SEQ) // GDN_CK
GDN_GROUP = 9

F32 = jnp.float32
BF16 = jnp.bfloat16

assert BATCH == SUBLANES


def _dot(a, b):
    return jnp.dot(a, b, preferred_element_type=F32)


def _dot_nt(a, b):
    return lax.dot_general(a, b, (((1,), (1,)), ((), ())), preferred_element_type=F32)


def _dot_tn(a, b):
    return lax.dot_general(a, b, (((0,), (0,)), ((), ())), preferred_element_type=F32)


def _sigmoid(x):
    return 1.0 / (1.0 + jnp.exp(-x))


def _silu(x):
    return x * _sigmoid(x)


def _per_sample(x, vec, op):
    r, c = x.shape
    x3 = x.reshape(r // SUBLANES, SUBLANES, c)
    return op(x3, vec[None]).reshape(r, c)


def _modulate(x, scale, shift):
    y = _per_sample(x, 1.0 + scale, lambda a, b: a * b)
    return _per_sample(y, shift, lambda a, b: a + b)


def _layer_norm(x, g, b):
    mu = jnp.mean(x, axis=-1, keepdims=True)
    xc = x - mu
    var = jnp.mean(xc * xc, axis=-1, keepdims=True)
    return xc * lax.rsqrt(var + LN_EPS) * g + b


def _load_rows(ref, start, n):
    if len(ref.shape) == 2:
        return ref[start:start + n, :]
    rg = ref.shape[1]
    return ref[start // rg:(start + n) // rg, :, :].reshape(n, ref.shape[2])


def _store_rows(ref, start, value):
    n = value.shape[0]
    if len(ref.shape) == 2:
        ref[start:start + n, :] = value
    else:
        rg = ref.shape[1]
        ref[start // rg:(start + n) // rg, :, :] = value.reshape(n // rg, rg, value.shape[1])


def _params(n_axes):
    return pltpu.CompilerParams(dimension_semantics=("arbitrary",) * n_axes,
                                vmem_limit_bytes=VMEM_LIMIT_BYTES)


class _Tiling:
    def __init__(self, kind, rows, tm):
        self.kind, self.rows, self.tm = kind, rows, tm
        self.n_tiles = rows // tm
        if kind == "col":
            self.rg = tm // GRID_H

    def view(self, a):
        if self.kind == "row":
            return a
        return a.reshape(GRID_H, SEG_ROW, a.shape[-1])

    def unview(self, a):
        return a.reshape(self.rows, a.shape[-1])

    def shape(self, c):
        return (self.rows, c) if self.kind == "row" else (GRID_H, SEG_ROW, c)

    def spec(self, c, col=None):
        col = col or (lambda *ij: 0)
        if self.kind == "row":
            return pl.BlockSpec((self.tm, c), lambda *ij: (ij[0], col(*ij)))
        return pl.BlockSpec((GRID_H, self.rg, c), lambda *ij: (0, ij[0], col(*ij)))


def _const_spec(shape):
    nd = len(shape)
    return pl.BlockSpec(shape, lambda *ij: (0,) * nd)


SH1, SC1, G1, SH2, SC2, G2 = range(6)


class _Mod:
    def __init__(self, arr, layer, stream):
        self.arr, self.layer, self.stream = arr, layer, stream

    def spec(self, col):
        layer, stream = self.layer, self.stream
        return pl.BlockSpec((None, None, SUBLANES, D_MODEL), lambda *ij: (layer, stream, 0, col))


def _ada_kernel(c_ref, w_ref, b_ref, o_ref):
    s = _silu(c_ref[...]).astype(BF16)
    o_ref[...] = _dot(s, w_ref[...].astype(BF16)) + b_ref[...]


def _ada(cvec, ada_w, ada_b):
    return pl.pallas_call(
        _ada_kernel,
        grid=(DEPTH, 6),
        in_specs=[_const_spec((2 * SUBLANES, D_MODEL)),
                  pl.BlockSpec((None, D_MODEL, D_MODEL), lambda i, j: (i, 0, j)),
                  pl.BlockSpec((None, 1, D_MODEL), lambda i, j: (i, 0, j))],
        out_specs=pl.BlockSpec((None, 2 * SUBLANES, D_MODEL), lambda i, j: (i, 0, j)),
        out_shape=jax.ShapeDtypeStruct((DEPTH, 2 * SUBLANES, 6 * D_MODEL), F32),
        compiler_params=_params(2), name="ada",
    )(cvec, ada_w, ada_b.reshape(DEPTH, 1, 6 * D_MODEL))


def _cf_kernel(x_ref, sc_ref, sh_ref, w_ref, cw_ref, g_ref, b_ref, o_ref, pad_ref, *, seg):
    tm = x_ref.shape[0]
    halo = (CF_CONV // 2) * SUBLANES
    zeros = jnp.zeros((halo, CF_W), F32)
    for s in range(tm // seg):
        pad_ref[s, 0:halo, :] = zeros
        pad_ref[s, halo + seg:halo + seg + halo, :] = zeros
    for p in range(tm // PIECE):
        h = _modulate(x_ref[p * PIECE:(p + 1) * PIECE, :], sc_ref[...], sh_ref[...]).astype(BF16)
        y = _dot(h, w_ref[...])
        glu = y[:, :CF_W] * _sigmoid(y[:, CF_W:])
        s, off = divmod(p * PIECE, seg)
        pad_ref[s, halo + off:halo + off + PIECE, :] = glu
    for s in range(tm // seg):
        def body(i, carry, s=s):
            for r0 in (pl.multiple_of(i * 2 * CONV_ROWS, CONV_ROWS), pl.multiple_of((i * 2 + 1) * CONV_ROWS, CONV_ROWS)):
                acc = cw_ref[0:1, :] * pad_ref[s, pl.ds(r0, CONV_ROWS), :]
                for k in range(1, CF_CONV):
                    acc = acc + cw_ref[k:k + 1, :] * pad_ref[s, pl.ds(r0 + k * SUBLANES, CONV_ROWS), :]
                o_ref[pl.ds(s * seg + r0, CONV_ROWS), :] = _silu(_layer_norm(acc, g_ref[...], b_ref[...]))
            return carry
        lax.fori_loop(0, seg // (2 * CONV_ROWS), body, 0)


def _cf_mixer(x, mod, w, cw, ln_g, ln_b, *, tm, seg):
    rows = x.shape[0]
    halo = (CF_CONV // 2) * SUBLANES
    return pl.pallas_call(
        functools.partial(_cf_kernel, seg=seg),
        grid=(rows // tm,),
        in_specs=[pl.BlockSpec((tm, D_MODEL), lambda i: (i, 0)), mod.spec(SC1), mod.spec(SH1),
                  _const_spec((D_MODEL, 2 * CF_W)), _const_spec((CF_CONV, CF_W)),
                  _const_spec((1, CF_W)), _const_spec((1, CF_W))],
        out_specs=pl.BlockSpec((tm, CF_W), lambda i: (i, 0)),
        out_shape=jax.ShapeDtypeStruct((rows, CF_W), F32),
        scratch_shapes=[pltpu.VMEM((tm // seg, seg + 2 * halo, CF_W), F32)],
        compiler_params=_params(1), name="cf_mixer",
    )(x, mod.arr, mod.arr, w, cw, ln_g, ln_b)


def _store_batch_major(dst_ref, stage_ref, value, tok0, n_tok):
    tok0 = pl.multiple_of(tok0, SUBLANES) if not isinstance(tok0, int) else tok0
    for c in range(stage_ref.shape[0]):
        lanes = slice(c * LANES, (c + 1) * LANES)
        stage_ref[c] = value[:, lanes]
        for b in range(BATCH):
            dst_ref[b, pl.ds(tok0, n_tok), lanes] = stage_ref[c, pl.ds(b, n_tok, stride=BATCH), :]


def _load_token_major(src_ref, stage_ref, tok0, n_tok):
    cols = []
    for c in range(stage_ref.shape[0]):
        lanes = slice(c * LANES, (c + 1) * LANES)
        for b in range(BATCH):
            stage_ref[c, pl.ds(b, n_tok, stride=BATCH), :] = src_ref[b, pl.ds(tok0, n_tok), lanes]
        cols.append(stage_ref[c])
    return jnp.concatenate(cols, axis=-1)


def _qkv_kernel(x_ref, sc_ref, sh_ref, w_ref, cw_ref, o_ref, h_ref, pad_ref, stage_ref, *, seg):
    j = pl.program_id(1)
    tm = x_ref.shape[0]
    halo = SUBLANES

    @pl.when(j == 0)
    def _():
        zeros = jnp.zeros((halo, GDN_QK), F32)
        for s in range(tm // seg):
            pad_ref[s, 0:halo, :] = zeros
            pad_ref[s, halo + seg:halo + seg + halo, :] = zeros
        for p in range(tm // PIECE):
            rows = slice(p * PIECE, (p + 1) * PIECE)
            h_ref[rows, :] = _modulate(x_ref[rows, :], sc_ref[...], sh_ref[...]).astype(BF16)

    for p in range(tm // PIECE):
        s, off = divmod(p * PIECE, seg)
        pad_ref[s, halo + off:halo + off + PIECE, :] = _dot(h_ref[p * PIECE:(p + 1) * PIECE, :], w_ref[...])

    is_qk = j < 2
    scale = jnp.where(j == 0, GDN_DK ** -0.5, 1.0).astype(F32)
    blk = 2 * CONV_ROWS
    for s in range(tm // seg):
        def body(i, carry, s=s):
            r0 = pl.multiple_of(i * blk, blk)
            y = (cw_ref[0:1, :] * pad_ref[s, pl.ds(r0, blk), :]
                 + cw_ref[1:2, :] * pad_ref[s, pl.ds(r0 + halo, blk), :]
                 + cw_ref[2:3, :] * pad_ref[s, pl.ds(r0 + 2 * halo, blk), :])
            y = _silu(y)
            heads = []
            for hh in range(GDN_HEADS):
                t = y[:, hh * GDN_DK:(hh + 1) * GDN_DK]
                nrm = t * (lax.rsqrt(jnp.sum(t * t, axis=-1, keepdims=True) + RMS_EPS) * scale)
                heads.append(jnp.where(is_qk, nrm, t))
            _store_batch_major(o_ref, stage_ref, jnp.concatenate(heads, axis=-1),
                               (s * seg) // SUBLANES + i * (blk // SUBLANES), blk // SUBLANES)
            return carry
        lax.fori_loop(0, seg // blk, body, 0)


def _qkv_proj(x, mod, w, cw, *, tm, seg):
    rows = x.shape[0]
    return pl.pallas_call(
        functools.partial(_qkv_kernel, seg=seg),
        grid=(rows // tm, 3),
        in_specs=[pl.BlockSpec((tm, D_MODEL), lambda i, j: (i, 0)), mod.spec(SC1), mod.spec(SH1),
                  pl.BlockSpec((D_MODEL, GDN_QK), lambda i, j: (0, j)),
                  pl.BlockSpec((3, GDN_QK), lambda i, j: (0, j))],
        out_specs=pl.BlockSpec((BATCH, tm // BATCH, GDN_QK), lambda i, j: (0, i, j)),
        out_shape=jax.ShapeDtypeStruct((BATCH, rows // BATCH, GDN_CONV_W), F32),
        scratch_shapes=[pltpu.VMEM((tm, D_MODEL), BF16),
                        pltpu.VMEM((tm // seg, seg + 2 * SUBLANES, GDN_QK), F32),
                        pltpu.VMEM((GDN_QK // LANES, 2 * CONV_ROWS, LANES), F32)],
        compiler_params=_params(2), name="qkv_proj",
    )(x, mod.arr, mod.arr, w, cw)


def _zbg_kernel(x_ref, sc_ref, sh_ref, w_ref, alog_ref, dtb_ref, z_ref, bg_ref, stage_ref):
    tm = x_ref.shape[0]
    for p in range(tm // PIECE):
        rows = slice(p * PIECE, (p + 1) * PIECE)
        h = _modulate(x_ref[rows, :], sc_ref[...], sh_ref[...]).astype(BF16)
        y = _dot(h, w_ref[...])
        _store_batch_major(z_ref, stage_ref, y[:, :GDN_W], p * (PIECE // BATCH), PIECE // BATCH)
        s = y[:, GDN_W:]
        t = s + dtb_ref[...]
        softplus = jnp.maximum(t, 0.0) + jnp.log1p(jnp.exp(-jnp.abs(t)))
        g = -jnp.exp(alog_ref[...]) * softplus
        lane = lax.broadcasted_iota(jnp.int32, s.shape, 1)
        bg_ref[rows, :] = jnp.where(lane < 2 * GDN_HEADS, _sigmoid(s), g)


def _zbg_proj(x, mod, w, alog, dtb, *, tm):
    rows = x.shape[0]
    return pl.pallas_call(
        _zbg_kernel,
        grid=(rows // tm,),
        in_specs=[pl.BlockSpec((tm, D_MODEL), lambda i: (i, 0)), mod.spec(SC1), mod.spec(SH1),
                  _const_spec((D_MODEL, GDN_W + LANES)), _const_spec((1, LANES)), _const_spec((1, LANES))],
        out_specs=[pl.BlockSpec((BATCH, tm // BATCH, GDN_W), lambda i: (0, i, 0)),
                   pl.BlockSpec((tm, LANES), lambda i: (i, 0))],
        out_shape=[jax.ShapeDtypeStruct((BATCH, rows // BATCH, GDN_W), F32), jax.ShapeDtypeStruct((rows, LANES), F32)],
        scratch_shapes=[pltpu.VMEM((GDN_W // LANES, PIECE, LANES), F32)],
        compiler_params=_params(1), name="zbg_proj",
    )(x, mod.arr, mod.arr, w, alog, dtb)


def _gdn_decays(beta_row, g_row, rev):
    c = GDN_CK
    ri = lax.broadcasted_iota(jnp.int32, (c, c), 0)
    ci = lax.broadcasted_iota(jnp.int32, (c, c), 1)
    eye = ri == ci
    incl = (ri <= ci) if rev else (ri >= ci)
    incl_t = (ci <= ri) if rev else (ci >= ri)
    strict = (ri < ci) if rev else (ri > ci)
    g_b = jnp.broadcast_to(g_row, (c, c))
    beta_b = jnp.broadcast_to(beta_row, (c, c))
    g_col = jnp.sum(jnp.where(eye, g_b, 0.0), axis=1, keepdims=True)
    beta_col = jnp.sum(jnp.where(eye, beta_b, 0.0), axis=1, keepdims=True)
    gc_col = jnp.sum(jnp.where(incl, g_b, 0.0), axis=1, keepdims=True)
    gc_row = jnp.sum(jnp.where(incl_t, g_col, 0.0), axis=0, keepdims=True)
    g_sum = jnp.sum(g_row, axis=1, keepdims=True)
    decay = jnp.exp(jnp.where(incl, gc_col - gc_row, -1e30))
    return dict(strict=strict, beta_col=beta_col, decay=decay, e_col=jnp.exp(gc_col),
                e_end=jnp.exp(g_sum - gc_col), g_tot=jnp.exp(g_sum))


def _gdn_chunk_terms(chains, side_tasks=()):
    side_tasks = list(side_tasks)

    def side():
        if side_tasks:
            side_tasks.pop(0)()

    c = GDN_CK
    ri = lax.broadcasted_iota(jnp.int32, (c, c), 0)
    ci = lax.broadcasted_iota(jnp.int32, (c, c), 1)
    same = lambda s: (ri // s) == (ci // s)
    nils = [jnp.where(ch["strict"], -(ch["beta_col"] * ch["kk"] * ch["decay"]), 0.0) for ch in chains]
    nds = [jnp.where(same(GDN_BASE), n, 0.0) for n in nils]
    invs = [jnp.where(ri == ci, 1.0, 0.0) + nd for nd in nds]
    p16s = [nd.astype(BF16) for nd in nds]
    side()
    for _ in range(int(math.log2(GDN_BASE)) - 1):
        p16s = [_dot(p16, p16).astype(BF16) for p16 in p16s]
        side()
        invs = [inv + _dot(p16, inv.astype(BF16)) for p16, inv in zip(p16s, invs)]
    s = GDN_BASE
    while s < c:
        off = jnp.logical_and(same(2 * s), jnp.logical_not(same(s)))
        inv16s = [inv.astype(BF16) for inv in invs]
        side()
        m16s = [_dot(inv16, jnp.where(off, n, 0.0).astype(BF16)).astype(BF16) for inv16, n in zip(inv16s, nils)]
        side()
        invs = [inv + _dot(m16, inv16) for inv, m16, inv16 in zip(invs, m16s, inv16s)]
        s *= 2
    side()
    y16s = [_dot(inv.astype(BF16),
                 jnp.concatenate([ch["v"] * ch["beta_col"], ch["k"] * (ch["beta_col"] * ch["e_col"])], axis=-1).astype(BF16)
                 ).astype(BF16) for inv, ch in zip(invs, chains)]
    side()
    kts = [_dot_tn((ch["k"] * ch["e_end"]).astype(BF16), y16) for ch, y16 in zip(chains, y16s)]
    qys = [_dot((ch["qk_raw"] * ch["decay"]).astype(BF16), y16) for ch, y16 in zip(chains, y16s)]
    while side_tasks:
        side()
    out = []
    for ch, kt, qy in zip(chains, kts, qys):
        q_eff = ch["q"] * ch["e_col"] - qy[:, GDN_DV:]
        out.append(((-kt[:, GDN_DV:]).astype(BF16), kt[:, :GDN_DV], q_eff.astype(BF16), qy[:, :GDN_DV],
                    jnp.broadcast_to(ch["g_tot"], (1, GDN_DV))))
    return out


def _gdn_kernel(qc, kc, vc, zc, ql, kl, vl, zl, bg, ng_ref, oc_ref, ol_ref,
                q_all, k_all, v_all, o_dir, a_ref, b_ref, qe_ref, oz_ref, gt_ref, s_ref):
    ncc = CTX_LEN // GDN_CK
    for src, dst in ((qc, q_all), (kc, k_all), (vc, v_all)):
        dst[0:CTX_LEN, :] = src[...]
    for src, dst in ((ql, q_all), (kl, k_all), (vl, v_all)):
        for r in range(SEQ // PIECE):
            dst[CTX_LEN + r * PIECE:CTX_LEN + (r + 1) * PIECE, :] = src[r * PIECE:(r + 1) * PIECE, :]

    def chunk_at(d, p):
        if d == 0:
            return p
        return jnp.where(p < ncc, ncc - 1 - p, GDN_NCHUNK + ncc - 1 - p)

    def rows_of(cc):
        if isinstance(cc, int):
            return pl.ds(cc * GDN_CK, GDN_CK)
        return pl.ds(pl.multiple_of(cc * GDN_CK, GDN_CK), GDN_CK)

    def prepare(g, side_tasks=()):
        chains = []
        for jj in range(GDN_GROUP):
            for d in range(2):
                cc = chunk_at(d, g * GDN_GROUP + jj)
                rows = rows_of(cc)
                qv, kv, vv = q_all[rows, :], k_all[rows, :], v_all[rows, :]
                k16 = kv.astype(BF16)
                chains.append(dict(q=qv, k=kv, v=vv, kk=_dot_nt(k16, k16), qk_raw=_dot_nt(qv.astype(BF16), k16),
                                   **_gdn_decays(bg[d, pl.ds(cc, 1), :], bg[2 + d, pl.ds(cc, 1), :], rev=(d == 1))))
        terms = _gdn_chunk_terms(chains, side_tasks)
        for n, (a_neg, b_mat, q_eff, o_zero, g_tot) in enumerate(terms):
            jj, d = divmod(n, 2)
            a_ref[d, jj] = a_neg
            b_ref[d, jj] = b_mat
            qe_ref[d, jj] = q_eff
            oz_ref[d, jj] = o_zero
            gt_ref[d, jj] = g_tot

    def advance_tasks(g):
        def step(jj):
            for d in range(2):
                rows = rows_of(chunk_at(d, g * GDN_GROUP + jj))
                s = s_ref[d]
                s16 = s.astype(BF16)
                o_dir[d, rows, :] = oz_ref[d, jj] + _dot(qe_ref[d, jj], s16)
                s_ref[d] = s * gt_ref[d, jj] + _dot(a_ref[d, jj], s16) + b_ref[d, jj]
        return [functools.partial(step, jj) for jj in range(GDN_GROUP)]

    s_ref[...] = jnp.zeros(s_ref.shape, F32)
    n_groups = GDN_NCHUNK // GDN_GROUP
    prepare(0)

    def body(g, carry):
        prepare(g, advance_tasks(g - 1))
        return carry
    lax.fori_loop(1, n_groups, body, 0)
    for task in advance_tasks(n_groups - 1):
        task()

    blk = 256
    for z, o, tok0 in ((zc, oc_ref, 0), (zl, ol_ref, CTX_LEN)):
        for r in range(z.shape[0] // blk):
            rows = slice(r * blk, (r + 1) * blk)
            src = slice(tok0 + r * blk, tok0 + (r + 1) * blk)
            ov = o_dir[0, src, :] + o_dir[1, src, :]
            ov = ov * lax.rsqrt(jnp.mean(ov * ov, axis=-1, keepdims=True) + RMS_EPS) * ng_ref[...]
            o[rows, :] = ov * _silu(z[rows, :])


def _gdn(qkv_c, z_c, qkv_l, z_l, bg, norm_g):
    def tok_spec(n_tok, col0):
        return pl.BlockSpec((None, n_tok, GDN_DK), lambda b, h: (b, 0, col0 + h))

    in_specs = []
    for n_tok in (CTX_LEN, SEQ):
        in_specs += [tok_spec(n_tok, 0), tok_spec(n_tok, GDN_HEADS), tok_spec(n_tok, 2 * GDN_HEADS), tok_spec(n_tok, 0)]
    in_specs += [pl.BlockSpec((None, None, 4, GDN_NCHUNK, GDN_CK), lambda b, h: (b, h, 0, 0, 0)),
                 _const_spec((1, GDN_DV))]
    n_tok = CTX_LEN + SEQ
    return pl.pallas_call(
        _gdn_kernel,
        grid=(BATCH, GDN_HEADS),
        in_specs=in_specs,
        out_specs=[tok_spec(CTX_LEN, 0), tok_spec(SEQ, 0)],
        out_shape=[jax.ShapeDtypeStruct((BATCH, CTX_LEN, GDN_W), F32),
                   jax.ShapeDtypeStruct((BATCH, SEQ, GDN_W), F32)],
        scratch_shapes=[pltpu.VMEM((n_tok, GDN_DK), F32), pltpu.VMEM((n_tok, GDN_DK), F32),
                        pltpu.VMEM((n_tok, GDN_DV), F32), pltpu.VMEM((2, n_tok, GDN_DV), F32),
                        pltpu.VMEM((2, GDN_GROUP, GDN_DK, GDN_DV), BF16),
                        pltpu.VMEM((2, GDN_GROUP, GDN_DK, GDN_DV), F32),
                        pltpu.VMEM((2, GDN_GROUP, GDN_CK, GDN_DK), BF16),
                        pltpu.VMEM((2, GDN_GROUP, GDN_CK, GDN_DV), F32),
                        pltpu.VMEM((2, GDN_GROUP, 1, GDN_DV), F32),
                        pltpu.VMEM((2, GDN_DK, GDN_DV), F32)],
        compiler_params=_params(2), name="gdn",
    )(qkv_c, qkv_c, qkv_c, z_c, qkv_l, qkv_l, qkv_l, z_l, bg, norm_g)


def _residual_norm(x, branch, gate, g, b):
    return _layer_norm(DN_ALPHA * x + _per_sample(branch, gate, lambda a, c: a * c), g, b)


def _out_even_kernel(x_ref, a_ref, o_ref, g1_ref, wa_ref, wo_ref, lng_ref, lnb_ref, out_ref, stage_ref):
    tm = x_ref.shape[0]
    for p in range(tm // PIECE):
        rows = slice(p * PIECE, (p + 1) * PIECE)
        o = _load_token_major(o_ref, stage_ref, p * (PIECE // BATCH), PIECE // BATCH)
        m = _dot(a_ref[rows, :].astype(BF16), wa_ref[...]) + _dot(o.astype(BF16), wo_ref[...])
        out_ref[rows, :] = _residual_norm(x_ref[rows, :], m, g1_ref[...], lng_ref[...], lnb_ref[...])


def _out_even(x, a, o, mod, wa, wo, ln_g, ln_b, *, tm):
    rows = x.shape[0]
    row = lambda c: pl.BlockSpec((tm, c), lambda i: (i, 0))
    return pl.pallas_call(
        _out_even_kernel,
        grid=(rows // tm,),
        in_specs=[row(D_MODEL), row(CF_W), pl.BlockSpec((BATCH, tm // BATCH, GDN_W), lambda i: (0, i, 0)),
                  mod.spec(G1),
                  _const_spec((CF_W, D_MODEL)), _const_spec((GDN_W, D_MODEL)),
                  _const_spec((1, D_MODEL)), _const_spec((1, D_MODEL))],
        out_specs=row(D_MODEL),
        out_shape=jax.ShapeDtypeStruct((rows, D_MODEL), F32),
        scratch_shapes=[pltpu.VMEM((GDN_W // LANES, PIECE, LANES), F32)],
        compiler_params=_params(1), name="out_even",
    )(x, a, o, mod.arr, wa, wo, ln_g, ln_b)


def _gelu_tanh(x):
    return 0.5 * x * (1.0 + jnp.tanh(math.sqrt(2.0 / math.pi) * (x + 0.044715 * (x * x * x))))


def _out_odd_kernel(x_ref, u_ref, yf_ref, yb_ref, s_ref, g1_ref, d_ref, gw_ref, gb_ref, wa_ref, wo_ref,
                    lng_ref, lnb_ref, out_ref):
    tm = x_ref.shape[0]
    for p in range(tm // PIECE):
        rows = slice(p * PIECE, (p + 1) * PIECE)
        y = d_ref[...] * u_ref[rows, :] + yf_ref[rows, :] + yb_ref[rows, :]
        zg = _gelu_tanh(y)
        s5 = zg * _sigmoid(_dot(zg.astype(BF16), gw_ref[...]) + gb_ref[...])
        m = _dot(s5.astype(BF16), wa_ref[...]) + _dot(s_ref[rows, :].astype(BF16), wo_ref[...])
        out_ref[rows, :] = _residual_norm(x_ref[rows, :], m, g1_ref[...], lng_ref[...], lnb_ref[...])


def _out_odd(x, u, yf, yb, s, mod, d_skip, glu_w, glu_b, wa, wo, ln_g, ln_b, *, tm):
    rows = x.shape[0]
    row = lambda c: pl.BlockSpec((tm, c), lambda i: (i, 0))
    return pl.pallas_call(
        _out_odd_kernel,
        grid=(rows // tm,),
        in_specs=[row(D_MODEL), row(S5_W), row(S5_W), row(S5_W), row(SC_W), mod.spec(G1),
                  _const_spec((1, S5_W)), _const_spec((S5_W, S5_W)), _const_spec((1, S5_W)),
                  _const_spec((S5_W, D_MODEL)), _const_spec((SC_W, D_MODEL)),
                  _const_spec((1, D_MODEL)), _const_spec((1, D_MODEL))],
        out_specs=row(D_MODEL),
        out_shape=jax.ShapeDtypeStruct((rows, D_MODEL), F32),
        compiler_params=_params(1), name="out_odd",
    )(x, u, yf, yb, s, mod.arr, d_skip, glu_w, glu_b, wa, wo, ln_g, ln_b)


def _conv3(pad_ref, s, r0, n, cw_ref, shift):
    return (cw_ref[0:1, :] * pad_ref[s, r0:r0 + n, :]
            + cw_ref[1:2, :] * pad_ref[s, r0 + shift:r0 + shift + n, :]
            + cw_ref[2:3, :] * pad_ref[s, r0 + 2 * shift:r0 + 2 * shift + n, :])


def _ffn_kernel(x_ref, sc_ref, sh_ref, g2_ref, wu_ref, cw_ref, wd_ref, lng_ref, lnb_ref, out_ref,
                h_ref, val_ref, pad_ref, *, tm, shift, seg):
    j = pl.program_id(1)
    n_pieces = tm // PIECE
    tf = val_ref.shape[1]

    @pl.when(j == 0)
    def _():
        zeros = jnp.zeros((shift, tf), F32)
        for s in range(tm // seg):
            pad_ref[s, 0:shift, :] = zeros
            pad_ref[s, shift + seg:shift + seg + shift, :] = zeros
        for p in range(n_pieces):
            x = _load_rows(x_ref, p * PIECE, PIECE)
            h_ref[p * PIECE:(p + 1) * PIECE, :] = _modulate(x, sc_ref[...], sh_ref[...]).astype(BF16)

    for p in range(n_pieces):
        h = h_ref[p * PIECE:(p + 1) * PIECE, :]
        up = _dot(h, wu_ref[...])
        val_ref[p * PIECE:(p + 1) * PIECE, :] = up[:, :tf]
        s, off = divmod(p * PIECE, seg)
        pad_ref[s, shift + off:shift + off + PIECE, :] = up[:, tf:]

    for p in range(n_pieces):
        s, off = divmod(p * PIECE, seg)
        gate = _conv3(pad_ref, s, off, PIECE, cw_ref, shift)
        act = (val_ref[p * PIECE:(p + 1) * PIECE, :] * _silu(gate)).astype(BF16)
        part = _dot(act, wd_ref[...])

        @pl.when(j == 0)
        def _():
            _store_rows(out_ref, p * PIECE, part)

        @pl.when(j > 0)
        def _():
            _store_rows(out_ref, p * PIECE, _load_rows(out_ref, p * PIECE, PIECE) + part)

    @pl.when(j == pl.num_programs(1) - 1)
    def _():
        for p in range(n_pieces):
            x = _load_rows(x_ref, p * PIECE, PIECE)
            y = _residual_norm(x, _load_rows(out_ref, p * PIECE, PIECE), g2_ref[...], lng_ref[...], lnb_ref[...])
            _store_rows(out_ref, p * PIECE, y)


def _ffn(x, mod, w_up, cw, w_down, ln_g, ln_b, *, tiling, shift, seg, tf):
    tm = tiling.tm
    n_f = FFN_F // tf
    out = pl.pallas_call(
        functools.partial(_ffn_kernel, tm=tm, shift=shift, seg=seg),
        grid=(tiling.n_tiles, n_f),
        in_specs=[tiling.spec(D_MODEL), mod.spec(SC2), mod.spec(SH2), mod.spec(G2),
                  pl.BlockSpec((D_MODEL, 2 * tf), lambda i, j: (0, j)),
                  pl.BlockSpec((3, tf), lambda i, j: (0, j)),
                  pl.BlockSpec((tf, D_MODEL), lambda i, j: (j, 0)),
                  _const_spec((1, D_MODEL)), _const_spec((1, D_MODEL))],
        out_specs=tiling.spec(D_MODEL),
        out_shape=jax.ShapeDtypeStruct(tiling.shape(D_MODEL), F32),
        scratch_shapes=[pltpu.VMEM((tm, D_MODEL), BF16), pltpu.VMEM((tm, tf), F32),
                        pltpu.VMEM((tm // seg, seg + 2 * shift, tf), F32)],
        compiler_params=_params(2), name="conv_ffn",
    )(tiling.view(x), mod.arr, mod.arr, mod.arr, w_up, cw, w_down, ln_g, ln_b)
    return tiling.unview(out)


def _odd_in_kernel(x_ref, sc_ref, sh_ref, wu_ref, wb_ref, wc_ref, wx_ref, cw_ref, u_ref, s_ref,
                   h_ref, b_ref, pad_ref, *, tm, shift, seg):
    j = pl.program_id(1)
    n_pieces = tm // PIECE
    cb = u_ref.shape[-1]

    @pl.when(j == 0)
    def _():
        zeros = jnp.zeros((shift, cb), F32)
        for s in range(tm // seg):
            pad_ref[s, 0:shift, :] = zeros
            pad_ref[s, shift + seg:shift + seg + shift, :] = zeros
        for p in range(n_pieces):
            x = _load_rows(x_ref, p * PIECE, PIECE)
            h_ref[p * PIECE:(p + 1) * PIECE, :] = _modulate(x, sc_ref[...], sh_ref[...]).astype(BF16)

    for p in range(n_pieces):
        h = h_ref[p * PIECE:(p + 1) * PIECE, :]
        _store_rows(u_ref, p * PIECE, _dot(h, wu_ref[...]))
        b_ref[p * PIECE:(p + 1) * PIECE, :] = _dot(h, wb_ref[...])
        s, off = divmod(p * PIECE, seg)
        pad_ref[s, shift + off:shift + off + PIECE, :] = _dot(h, wc_ref[...]) * _dot(h, wx_ref[...])

    for p in range(n_pieces):
        s, off = divmod(p * PIECE, seg)
        _store_rows(s_ref, p * PIECE, b_ref[p * PIECE:(p + 1) * PIECE, :] * _conv3(pad_ref, s, off, PIECE, cw_ref, shift))


def _odd_in(x, mod, w_in, cw, *, tiling, shift, seg):
    tm = tiling.tm
    cb = 2 * LANES
    nb = S5_W // cb
    wspec = lambda k: pl.BlockSpec((D_MODEL, cb), lambda i, j: (0, k * nb + j))
    u, s = pl.pallas_call(
        functools.partial(_odd_in_kernel, tm=tm, shift=shift, seg=seg),
        grid=(tiling.n_tiles, nb),
        in_specs=[tiling.spec(D_MODEL), mod.spec(SC1), mod.spec(SH1),
                  wspec(0), wspec(1), wspec(2), wspec(3),
                  pl.BlockSpec((3, cb), lambda i, j: (0, j))],
        out_specs=[tiling.spec(cb, lambda i, j: j), tiling.spec(cb, lambda i, j: j)],
        out_shape=[jax.ShapeDtypeStruct(tiling.shape(S5_W), F32), jax.ShapeDtypeStruct(tiling.shape(SC_W), F32)],
        scratch_shapes=[pltpu.VMEM((tm, D_MODEL), BF16), pltpu.VMEM((tm, cb), F32),
                        pltpu.VMEM((tm // seg, seg + 2 * shift, cb), F32)],
        compiler_params=_params(2), name="odd_in",
    )(tiling.view(x), mod.arr, mod.arr, w_in, w_in, w_in, w_in, cw)
    return tiling.unview(u), tiling.unview(s)


def _s5_kernel(uf_ref, ub_ref, wb_ref, wc_ref, a_ref, x0_ref, yf_ref, yb_ref, xfin_ref, bu_ref, st_ref):
    t = pl.program_id(0)
    rows = uf_ref.shape[0]
    n_tok = rows // SUBLANES
    hs = S5_HSTATE

    @pl.when(t == 0)
    def _():
        st_ref[...] = x0_ref[...]

    for d, u_ref in enumerate((uf_ref, ub_ref)):
        for hf in range(2):
            u = u_ref[:, hf * S5_HALF:(hf + 1) * S5_HALF].astype(BF16)
            bu_ref[d, :, hf * 2 * hs:(hf + 1) * 2 * hs] = _dot(u, wb_ref[d, hf])

    for d in range(2):
        for hf in range(2):
            c_re = hf * 2 * hs
            c_im = c_re + hs
            a_re = jnp.broadcast_to(a_ref[d, hf, 0:1, :], (SUBLANES, hs))
            a_im = jnp.broadcast_to(a_ref[d, hf, 1:2, :], (SUBLANES, hs))

            def body(i, carry, d=d, c_re=c_re, c_im=c_im, a_re=a_re, a_im=a_im):
                xr, xi = carry
                tok = i if d == 0 else n_tok - 1 - i
                r = pl.ds(pl.multiple_of(tok * SUBLANES, SUBLANES), SUBLANES)
                nr = a_re * xr - a_im * xi + bu_ref[d, r, c_re:c_re + hs]
                ni = a_re * xi + a_im * xr + bu_ref[d, r, c_im:c_im + hs]
                bu_ref[d, r, c_re:c_re + hs] = nr
                bu_ref[d, r, c_im:c_im + hs] = ni
                return nr, ni

            xr, xi = lax.fori_loop(0, n_tok, body, (st_ref[d, :, c_re:c_re + hs], st_ref[d, :, c_im:c_im + hs]))
            st_ref[d, :, c_re:c_re + hs] = xr
            st_ref[d, :, c_im:c_im + hs] = xi

    for d, y_ref in enumerate((yf_ref, yb_ref)):
        for hf in range(2):
            xs = bu_ref[d, :, hf * 2 * hs:(hf + 1) * 2 * hs].astype(BF16)
            y_ref[:, hf * S5_HALF:(hf + 1) * S5_HALF] = _dot(xs, wc_ref[d, hf])

    @pl.when(t == pl.num_programs(0) - 1)
    def _():
        xfin_ref[...] = st_ref[...]


def _s5(u, wb, wc, a, x0):
    rows = u.shape[0]
    tr = S5_TOK * SUBLANES
    nt = rows // tr
    state = (2, SUBLANES, 4 * S5_HSTATE)
    return pl.pallas_call(
        _s5_kernel,
        grid=(nt,),
        in_specs=[pl.BlockSpec((tr, S5_W), lambda t: (t, 0)),
                  pl.BlockSpec((tr, S5_W), lambda t: (nt - 1 - t, 0)),
                  _const_spec((2, 2, S5_HALF, 2 * S5_HSTATE)), _const_spec((2, 2, 2 * S5_HSTATE, S5_HALF)),
                  _const_spec((2, 2, 2, S5_HSTATE)), _const_spec(state)],
        out_specs=[pl.BlockSpec((tr, S5_W), lambda t: (t, 0)),
                   pl.BlockSpec((tr, S5_W), lambda t: (nt - 1 - t, 0)),
                   _const_spec(state)],
        out_shape=[jax.ShapeDtypeStruct((rows, S5_W), F32), jax.ShapeDtypeStruct((rows, S5_W), F32),
                   jax.ShapeDtypeStruct(state, F32)],
        scratch_shapes=[pltpu.VMEM((2, tr, 4 * S5_HSTATE), F32), pltpu.VMEM(state, F32)],
        compiler_params=_params(1), name="s5_scan",
    )(u, u, wb, wc, a, x0)


def _s5_weights(lam_re, lam_im, log_dt, b_re, b_im, c_re, c_im):
    eye = jnp.eye(S5_G // 2, dtype=F32)
    wbs, wcs, avs = [], [], []
    for di in range(2):
        lr, li = lam_re[di], lam_im[di]
        dt = jnp.exp(log_dt[di])[:, None]
        mag = jnp.exp(lr * dt)
        ar, ai = mag * jnp.cos(li * dt), mag * jnp.sin(li * dt)
        den = lr * lr + li * li
        fr = ((ar - 1.0) * lr + ai * li) / den
        fi = (ai * lr - (ar - 1.0) * li) / den
        bbr = fr[..., None] * b_re - fi[..., None] * b_im
        bbi = fr[..., None] * b_im + fi[..., None] * b_re

        def in_block(t):
            t = t.reshape(2, S5_G // 2, S5_N, S5_P)
            return jnp.einsum("hgnp,gk->hgpkn", t, eye).reshape(2, S5_HALF, S5_HSTATE)

        def out_block(t):
            t = t.reshape(2, S5_G // 2, S5_P, S5_N)
            return jnp.einsum("hgpn,gk->hgnkp", t, eye).reshape(2, S5_HSTATE, S5_HALF)

        wbs.append(jnp.concatenate([in_block(bbr), in_block(bbi)], axis=-1))
        wcs.append(jnp.concatenate([out_block(c_re), out_block(-c_im)], axis=1))
        avs.append(jnp.stack([ar.reshape(2, S5_HSTATE), ai.reshape(2, S5_HSTATE)], axis=1))
    return jnp.stack(wbs).astype(BF16), jnp.stack(wcs).astype(BF16), jnp.stack(avs)


def _bg_rows(bg, n_tok):
    t = bg.reshape(n_tok, BATCH, LANES)[:, :, :4 * GDN_HEADS].reshape(n_tok, BATCH, 4, GDN_HEADS)
    return t.transpose(1, 3, 2, 0).reshape(BATCH, GDN_HEADS, 4, n_tok // GDN_CK, GDN_CK)


def _to_batch_major(a, n_tok):
    return a.reshape(n_tok, BATCH, a.shape[-1]).transpose(1, 0, 2)


def _to_token_major(a):
    return a.transpose(1, 0, 2).reshape(a.shape[0] * a.shape[1], a.shape[2])


def kernel(x, c, ctx, c_ctx, ada_w, ada_b, ln_g, ln_b, ev_w_in, ev_w_out, cf_conv, cf_ln_g, cf_ln_b, gdn_conv,
           gdn_a_log, gdn_dt_bias, gdn_norm_g, od_w_in, od_w_out, s5_lam_re, s5_lam_im, s5_log_dt, s5_b_re,
           s5_b_im, s5_c_re, s5_c_im, s5_d, s5_glu_w, s5_glu_b, sc_conv, ffn_w_up, ffn_conv, ffn_w_down):
    xl = _to_token_major(x)
    cl = _to_token_major(ctx)

    cvec = jnp.concatenate([c, jnp.broadcast_to(c_ctx[None, :], (SUBLANES, D_MODEL))], axis=0)
    mods = _ada(cvec, ada_w, ada_b).reshape(DEPTH, 2, SUBLANES, 6 * D_MODEL)

    lat_row = _Tiling("row", ROWS_LAT, 2 * SEG_ROW)
    lat_col = _Tiling("col", ROWS_LAT, 2 * SEG_ROW)
    ctx_row = _Tiling("row", ROWS_CTX, ROWS_CTX)
    lat_along_row = dict(tiling=lat_row, shift=SUBLANES, seg=SEG_ROW)
    lat_along_col = dict(tiling=lat_col, shift=lat_col.rg, seg=lat_col.tm)
    ctx_conv = dict(tiling=ctx_row, shift=SUBLANES, seg=ROWS_CTX)

    row2 = lambda v: v.reshape(1, -1)
    for i in range(DEPTH):
        j = i // 2
        last = i == DEPTH - 1
        m_lat, m_ctx = _Mod(mods, i, 0), _Mod(mods, i, 1)
        lng1, lnb1, lng2, lnb2 = row2(ln_g[i, 0]), row2(ln_b[i, 0]), row2(ln_g[i, 1]), row2(ln_b[i, 1])
        w_up, w_down, f_cw = ffn_w_up[i].astype(BF16), ffn_w_down[i].astype(BF16), ffn_conv[i]
        up_blocks = lambda tf: w_up.reshape(D_MODEL, 2, FFN_F // tf, tf).transpose(0, 2, 1, 3).reshape(D_MODEL, 2 * FFN_F)

        if i % 2 == 0:
            w_in = ev_w_in[j]
            w_cf = w_in[:, :2 * CF_W].astype(BF16)
            w_qkv = w_in[:, 2 * CF_W:2 * CF_W + GDN_CONV_W].astype(BF16)
            w_zbg = jnp.pad(w_in[:, 2 * CF_W + GDN_CONV_W:], ((0, 0), (0, LANES - 4 * GDN_HEADS))).astype(BF16)
            pad16 = lambda v: jnp.pad(v.reshape(1, -1), ((0, 0), (2 * GDN_HEADS, LANES - 4 * GDN_HEADS)))
            alog, dtb = pad16(gdn_a_log[j]), pad16(gdn_dt_bias[j])
            w_out = ev_w_out[j].astype(BF16)
            parts = []
            for xs, m, tm, seg, n_tok in ((cl, m_ctx, ROWS_CTX, ROWS_CTX, CTX_LEN), (xl, m_lat, 4 * SEG_ROW, SEG_ROW, SEQ)):
                a = _cf_mixer(xs, m, w_cf, cf_conv[j], row2(cf_ln_g[j]), row2(cf_ln_b[j]), tm=tm, seg=seg)
                qkv = _qkv_proj(xs, m, w_qkv, gdn_conv[j], tm=tm, seg=seg)
                z, bg = _zbg_proj(xs, m, w_zbg, alog, dtb, tm=tm)
                parts.append((a, qkv, z, _bg_rows(bg, n_tok)))
            (a_c, qkv_c, z_c, bg_c), (a_l, qkv_l, z_l, bg_l) = parts
            o_c, o_l = _gdn(qkv_c, z_c, qkv_l, z_l, jnp.concatenate([bg_c, bg_l], axis=3), row2(gdn_norm_g[j]))
            x1 = _out_even(xl, a_l, o_l, m_lat, w_out[:CF_W], w_out[CF_W:], lng1, lnb1, tm=2 * SEG_ROW)
            if not last:
                c1 = _out_even(cl, a_c, o_c, m_ctx, w_out[:CF_W], w_out[CF_W:], lng1, lnb1, tm=2 * SEG_ROW)
            lat_ffn = lat_along_col
        else:
            w_in = od_w_in[j].astype(BF16)
            w_out = od_w_out[j].astype(BF16)
            wb, wc, av = _s5_weights(s5_lam_re[j], s5_lam_im[j], s5_log_dt[j], s5_b_re[j], s5_b_im[j],
                                     s5_c_re[j], s5_c_im[j])
            u_c, s_c = _odd_in(cl, m_ctx, w_in, sc_conv[j], **ctx_conv)
            u_l, s_l = _odd_in(xl, m_lat, w_in, sc_conv[j], **lat_along_col)
            zero_state = jnp.zeros((2, SUBLANES, 4 * S5_HSTATE), F32)
            yf_c, yb_c, fin_c = _s5(u_c, wb, wc, av, zero_state)
            yf_l, yb_l, _ = _s5(u_l, wb, wc, av, fin_c)
            odd_w = (row2(s5_d[j]), s5_glu_w[j].astype(BF16), row2(s5_glu_b[j]), w_out[:S5_W], w_out[S5_W:])
            x1 = _out_odd(xl, u_l, yf_l, yb_l, s_l, m_lat, *odd_w, lng1, lnb1, tm=2 * SEG_ROW)
            if not last:
                c1 = _out_odd(cl, u_c, yf_c, yb_c, s_c, m_ctx, *odd_w, lng1, lnb1, tm=2 * SEG_ROW)
            lat_ffn = lat_along_row

        xl = _ffn(x1, m_lat, up_blocks(FFN_TF_LAT), f_cw, w_down, lng2, lnb2,
                  tf=FFN_TF_LAT, **lat_ffn)
        if not last:
            cl = _ffn(c1, m_ctx, up_blocks(FFN_TF_CTX), f_cw, w_down, lng2, lnb2,
                      tf=FFN_TF_CTX, **ctx_conv)

    return xl.reshape(SEQ, BATCH, D_MODEL).transpose(1, 0, 2)
```

```python
import functools
import math

import jax
import jax.numpy as jnp
from jax import lax
from jax.experimental import pallas as pl
from jax.experimental.pallas import tpu as pltpu

D_MODEL = 1024
BATCH = 8
SEQ = 2048
DEPTH = 4
GRID_W = 64
GRID_H = SEQ // GRID_W
CTX_LEN = 256
CF_W = 512
CF_CONV = 31
GDN_HEADS = 4
GDN_DK = 128
GDN_DV = 128
GDN_QK = GDN_HEADS * GDN_DK
GDN_W = GDN_HEADS * GDN_DV
GDN_CONV_W = 2 * GDN_QK + GDN_W
S5_W = 512
S5_P = 16
S5_G = S5_W // S5_P
S5_N = 64
SC_W = 512
FFN_F = 2816
DN_ALPHA = (2 * DEPTH) ** 0.25
LN_EPS = 1e-5
RMS_EPS = 1e-6

SUBLANES = 8
LANES = 128
VMEM_LIMIT_BYTES = 56 * 1024 * 1024

ROWS_LAT = SEQ * BATCH
ROWS_CTX = CTX_LEN * BATCH
SEG_ROW = GRID_W * BATCH
PIECE = 512
CONV_ROWS = 64
FFN_TF_LAT = FFN_F // 2
FFN_TF_CTX = 256
S5_HALF = S5_W // 2
S5_HSTATE = (S5_G // 2) * S5_N
S5_TOK = 64
GDN_CK = 64
GDN_BASE = 16
GDN_NCHUNK = (CTX_LEN + SEQ) // GDN_CK
GDN_GROUP = 9

F32 = jnp.float32
BF16 = jnp.bfloat16

assert BATCH == SUBLANES


def _dot(a, b):
    return jnp.dot(a, b, preferred_element_type=F32)


def _dot_nt(a, b):
    return lax.dot_general(a, b, (((1,), (1,)), ((), ())), preferred_element_type=F32)


def _dot_tn(a, b):
    return lax.dot_general(a, b, (((0,), (0,)), ((), ())), preferred_element_type=F32)


def _sigmoid(x):
    return 1.0 / (1.0 + jnp.exp(-x))


def _silu(x):
    return x * _sigmoid(x)


def _per_sample(x, vec, op):
    r, c = x.shape
    x3 = x.reshape(r // SUBLANES, SUBLANES, c)
    return op(x3, vec[None]).reshape(r, c)


def _modulate(x, scale, shift):
    y = _per_sample(x, 1.0 + scale, lambda a, b: a * b)
    return _per_sample(y, shift, lambda a, b: a + b)


def _layer_norm(x, g, b):
    mu = jnp.mean(x, axis=-1, keepdims=True)
    xc = x - mu
    var = jnp.mean(xc * xc, axis=-1, keepdims=True)
    return xc * lax.rsqrt(var + LN_EPS) * g + b


def _load_rows(ref, start, n):
    if len(ref.shape) == 2:
        return ref[start:start + n, :]
    rg = ref.shape[1]
    return ref[start // rg:(start + n) // rg, :, :].reshape(n, ref.shape[2])


def _store_rows(ref, start, value):
    n = value.shape[0]
    if len(ref.shape) == 2:
        ref[start:start + n, :] = value
    else:
        rg = ref.shape[1]
        ref[start // rg:(start + n) // rg, :, :] = value.reshape(n // rg, rg, value.shape[1])


def _params(n_axes):
    return pltpu.CompilerParams(dimension_semantics=("arbitrary",) * n_axes,
                                vmem_limit_bytes=VMEM_LIMIT_BYTES)


class _Tiling:
    def __init__(self, kind, rows, tm):
        self.kind, self.rows, self.tm = kind, rows, tm
        self.n_tiles = rows // tm
        if kind == "col":
            self.rg = tm // GRID_H

    def view(self, a):
        if self.kind == "row":
            return a
        return a.reshape(GRID_H, SEG_ROW, a.shape[-1])

    def unview(self, a):
        return a.reshape(self.rows, a.shape[-1])

    def shape(self, c):
        return (self.rows, c) if self.kind == "row" else (GRID_H, SEG_ROW, c)

    def spec(self, c, col=None):
        col = col or (lambda *ij: 0)
        if self.kind == "row":
            return pl.BlockSpec((self.tm, c), lambda *ij: (ij[0], col(*ij)))
        return pl.BlockSpec((GRID_H, self.rg, c), lambda *ij: (0, ij[0], col(*ij)))


def _const_spec(shape):
    nd = len(shape)
    return pl.BlockSpec(shape, lambda *ij: (0,) * nd)


SH1, SC1, G1, SH2, SC2, G2 = range(6)


class _Mod:
    def __init__(self, arr, layer, stream):
        self.arr, self.layer, self.stream = arr, layer, stream

    def spec(self, col):
        layer, stream = self.layer, self.stream
        return pl.BlockSpec((None, None, SUBLANES, D_MODEL), lambda *ij: (layer, stream, 0, col))


def _ada_kernel(c_ref, w_ref, b_ref, o_ref):
    s = _silu(c_ref[...]).astype(BF16)
    o_ref[...] = _dot(s, w_ref[...].astype(BF16)) + b_ref[...]


def _ada(cvec, ada_w, ada_b):
    return pl.pallas_call(
        _ada_kernel,
        grid=(DEPTH, 6),
        in_specs=[_const_spec((2 * SUBLANES, D_MODEL)),
                  pl.BlockSpec((None, D_MODEL, D_MODEL), lambda i, j: (i, 0, j)),
                  pl.BlockSpec((None, 1, D_MODEL), lambda i, j: (i, 0, j))],
        out_specs=pl.BlockSpec((None, 2 * SUBLANES, D_MODEL), lambda i, j: (i, 0, j)),
        out_shape=jax.ShapeDtypeStruct((DEPTH, 2 * SUBLANES, 6 * D_MODEL), F32),
        compiler_params=_params(2), name="ada",
    )(cvec, ada_w, ada_b.reshape(DEPTH, 1, 6 * D_MODEL))


def _cf_kernel(x_ref, sc_ref, sh_ref, w_ref, cw_ref, g_ref, b_ref, o_ref, pad_ref, *, seg):
    tm = x_ref.shape[0]
    halo = (CF_CONV // 2) * SUBLANES
    zeros = jnp.zeros((halo, CF_W), F32)
    for s in range(tm // seg):
        pad_ref[s, 0:halo, :] = zeros
        pad_ref[s, halo + seg:halo + seg + halo, :] = zeros
    def project(p):
        h = _modulate(x_ref[p * PIECE:(p + 1) * PIECE, :], sc_ref[...], sh_ref[...]).astype(BF16)
        y = _dot(h, w_ref[...])
        glu = y[:, :CF_W] * _sigmoid(y[:, CF_W:])
        s, off = divmod(p * PIECE, seg)
        pad_ref[s, halo + off:halo + off + PIECE, :] = glu

    def taps(s, r0):
        acc = cw_ref[0:1, :] * pad_ref[s, r0:r0 + CONV_ROWS, :]
        for k in range(1, CF_CONV):
            acc = acc + cw_ref[k:k + 1, :] * pad_ref[s, r0 + k * SUBLANES:r0 + k * SUBLANES + CONV_ROWS, :]
        o_ref[s * seg + r0:s * seg + r0 + CONV_ROWS, :] = _silu(_layer_norm(acc, g_ref[...], b_ref[...]))

    if seg == PIECE:
        project(0)
        for p in range(tm // PIECE):
            if p + 1 < tm // PIECE:
                project(p + 1)
            for r0 in range(0, PIECE, CONV_ROWS):
                taps(p, r0)
        return

    for p in range(tm // PIECE):
        project(p)
    for s in range(tm // seg):
        def body(i, carry, s=s):
            for r0 in (pl.multiple_of(i * 2 * CONV_ROWS, CONV_ROWS), pl.multiple_of((i * 2 + 1) * CONV_ROWS, CONV_ROWS)):
                acc = cw_ref[0:1, :] * pad_ref[s, pl.ds(r0, CONV_ROWS), :]
                for k in range(1, CF_CONV):
                    acc = acc + cw_ref[k:k + 1, :] * pad_ref[s, pl.ds(r0 + k * SUBLANES, CONV_ROWS), :]
                o_ref[pl.ds(s * seg + r0, CONV_ROWS), :] = _silu(_layer_norm(acc, g_ref[...], b_ref[...]))
            return carry
        lax.fori_loop(0, seg // (2 * CONV_ROWS), body, 0)


def _cf_mixer(x, mod, w, cw, ln_g, ln_b, *, tm, seg):
    rows = x.shape[0]
    halo = (CF_CONV // 2) * SUBLANES
    return pl.pallas_call(
        functools.partial(_cf_kernel, seg=seg),
        grid=(rows // tm,),
        in_specs=[pl.BlockSpec((tm, D_MODEL), lambda i: (i, 0)), mod.spec(SC1), mod.spec(SH1),
                  _const_spec((D_MODEL, 2 * CF_W)), _const_spec((CF_CONV, CF_W)),
                  _const_spec((1, CF_W)), _const_spec((1, CF_W))],
        out_specs=pl.BlockSpec((tm, CF_W), lambda i: (i, 0)),
        out_shape=jax.ShapeDtypeStruct((rows, CF_W), F32),
        scratch_shapes=[pltpu.VMEM((tm // seg, seg + 2 * halo, CF_W), F32)],
        compiler_params=_params(1), name="cf_mixer",
    )(x, mod.arr, mod.arr, w, cw, ln_g, ln_b)


def _store_batch_major(dst_ref, stage_ref, value, tok0, n_tok, stage_row0=0):
    for c in range(stage_ref.shape[0]):
        lanes = slice(c * LANES, (c + 1) * LANES)
        stage_ref[c, stage_row0:stage_row0 + n_tok * BATCH, :] = value[:, lanes]
        for b in range(BATCH):
            dst_ref[b, tok0:tok0 + n_tok, lanes] = stage_ref[c, pl.ds(stage_row0 + b, n_tok, stride=BATCH), :]


def _load_token_major(src_ref, stage_ref, tok0, n_tok):
    cols = []
    for c in range(stage_ref.shape[0]):
        lanes = slice(c * LANES, (c + 1) * LANES)
        for b in range(BATCH):
            stage_ref[c, pl.ds(b, n_tok, stride=BATCH), :] = src_ref[b, pl.ds(tok0, n_tok), lanes]
        cols.append(stage_ref[c])
    return jnp.concatenate(cols, axis=-1)


def _qkv_kernel(x_ref, sc_ref, sh_ref, w_ref, cw_ref, o_ref, h_ref, pad_ref, stage_ref, *, seg):
    j = pl.program_id(1)
    tm = x_ref.shape[0]
    halo = SUBLANES

    @pl.when(j == 0)
    def _():
        zeros = jnp.zeros((halo, GDN_QK), F32)
        for s in range(tm // seg):
            pad_ref[s, 0:halo, :] = zeros
            pad_ref[s, halo + seg:halo + seg + halo, :] = zeros
        for p in range(tm // PIECE):
            rows = slice(p * PIECE, (p + 1) * PIECE)
            h_ref[rows, :] = _modulate(x_ref[rows, :], sc_ref[...], sh_ref[...]).astype(BF16)

    is_qk = j < 2
    scale = jnp.where(j == 0, GDN_DK ** -0.5, 1.0).astype(F32)
    n_pieces = tm // PIECE
    blk = 2 * CONV_ROWS

    def project(p):
        s, off = divmod(p * PIECE, seg)
        pad_ref[s, halo + off:halo + off + PIECE, :] = _dot(h_ref[p * PIECE:(p + 1) * PIECE, :], w_ref[...])

    def finish(p):
        s, off = divmod(p * PIECE, seg)
        for r0 in range(off, off + PIECE, blk):
            y = (cw_ref[0:1, :] * pad_ref[s, r0:r0 + blk, :]
                 + cw_ref[1:2, :] * pad_ref[s, r0 + halo:r0 + halo + blk, :]
                 + cw_ref[2:3, :] * pad_ref[s, r0 + 2 * halo:r0 + 2 * halo + blk, :])
            y = _silu(y)
            heads = []
            for hh in range(GDN_HEADS):
                t = y[:, hh * GDN_DK:(hh + 1) * GDN_DK]
                nrm = t * (lax.rsqrt(jnp.sum(t * t, axis=-1, keepdims=True) + RMS_EPS) * scale)
                heads.append(jnp.where(is_qk, nrm, t))
            row0 = s * seg + r0
            _store_batch_major(o_ref, stage_ref, jnp.concatenate(heads, axis=-1), row0 // SUBLANES, blk // SUBLANES,
                               stage_row0=row0)

    project(0)
    for p in range(n_pieces):
        if p + 1 < n_pieces:
            project(p + 1)
        finish(p)


def _qkv_proj(x, mod, w, cw, *, tm, seg):
    rows = x.shape[0]
    return pl.pallas_call(
        functools.partial(_qkv_kernel, seg=seg),
        grid=(rows // tm, 3),
        in_specs=[pl.BlockSpec((tm, D_MODEL), lambda i, j: (i, 0)), mod.spec(SC1), mod.spec(SH1),
                  pl.BlockSpec((D_MODEL, GDN_QK), lambda i, j: (0, j)),
                  pl.BlockSpec((3, GDN_QK), lambda i, j: (0, j))],
        out_specs=pl.BlockSpec((BATCH, tm // BATCH, GDN_QK), lambda i, j: (0, i, j)),
        out_shape=jax.ShapeDtypeStruct((BATCH, rows // BATCH, GDN_CONV_W), F32),
        scratch_shapes=[pltpu.VMEM((tm, D_MODEL), BF16),
                        pltpu.VMEM((tm // seg, seg + 2 * SUBLANES, GDN_QK), F32),
                        pltpu.VMEM((GDN_QK // LANES, tm, LANES), F32)],
        compiler_params=_params(2), name="qkv_proj",
    )(x, mod.arr, mod.arr, w, cw)


def _zbg_kernel(x_ref, sc_ref, sh_ref, w_ref, alog_ref, dtb_ref, z_ref, bg_ref, stage_ref):
    tm = x_ref.shape[0]
    for p in range(tm // PIECE):
        rows = slice(p * PIECE, (p + 1) * PIECE)
        h = _modulate(x_ref[rows, :], sc_ref[...], sh_ref[...]).astype(BF16)
        y = _dot(h, w_ref[...])
        _store_batch_major(z_ref, stage_ref, y[:, :GDN_W], p * (PIECE // BATCH), PIECE // BATCH)
        s = y[:, GDN_W:]
        t = s + dtb_ref[...]
        softplus = jnp.maximum(t, 0.0) + jnp.log1p(jnp.exp(-jnp.abs(t)))
        g = -jnp.exp(alog_ref[...]) * softplus
        lane = lax.broadcasted_iota(jnp.int32, s.shape, 1)
        bg_ref[rows, :] = jnp.where(lane < 2 * GDN_HEADS, _sigmoid(s), g)


def _zbg_proj(x, mod, w, alog, dtb, *, tm):
    rows = x.shape[0]
    return pl.pallas_call(
        _zbg_kernel,
        grid=(rows // tm,),
        in_specs=[pl.BlockSpec((tm, D_MODEL), lambda i: (i, 0)), mod.spec(SC1), mod.spec(SH1),
                  _const_spec((D_MODEL, GDN_W + LANES)), _const_spec((1, LANES)), _const_spec((1, LANES))],
        out_specs=[pl.BlockSpec((BATCH, tm // BATCH, GDN_W), lambda i: (0, i, 0)),
                   pl.BlockSpec((tm, LANES), lambda i: (i, 0))],
        out_shape=[jax.ShapeDtypeStruct((BATCH, rows // BATCH, GDN_W), F32), jax.ShapeDtypeStruct((rows, LANES), F32)],
        scratch_shapes=[pltpu.VMEM((GDN_W // LANES, PIECE, LANES), F32)],
        compiler_params=_params(1), name="zbg_proj",
    )(x, mod.arr, mod.arr, w, alog, dtb)


def _gdn_decays(beta_row, g_row, rev):
    c = GDN_CK
    ri = lax.broadcasted_iota(jnp.int32, (c, c), 0)
    ci = lax.broadcasted_iota(jnp.int32, (c, c), 1)
    eye = ri == ci
    incl = (ri <= ci) if rev else (ri >= ci)
    incl_t = (ci <= ri) if rev else (ci >= ri)
    strict = (ri < ci) if rev else (ri > ci)
    g_b = jnp.broadcast_to(g_row, (c, c))
    beta_b = jnp.broadcast_to(beta_row, (c, c))
    g_col = jnp.sum(jnp.where(eye, g_b, 0.0), axis=1, keepdims=True)
    beta_col = jnp.sum(jnp.where(eye, beta_b, 0.0), axis=1, keepdims=True)
    gc_col = jnp.sum(jnp.where(incl, g_b, 0.0), axis=1, keepdims=True)
    gc_row = jnp.sum(jnp.where(incl_t, g_col, 0.0), axis=0, keepdims=True)
    g_sum = jnp.sum(g_row, axis=1, keepdims=True)
    decay = jnp.exp(jnp.where(incl, gc_col - gc_row, -1e30))
    return dict(strict=strict, beta_col=beta_col, decay=decay, e_col=jnp.exp(gc_col),
                e_end=jnp.exp(g_sum - gc_col), g_tot=jnp.exp(g_sum))


def _gdn_chunk_terms(chains, side_tasks=()):
    side_tasks = list(side_tasks)

    def side():
        if side_tasks:
            side_tasks.pop(0)()

    c = GDN_CK
    ri = lax.broadcasted_iota(jnp.int32, (c, c), 0)
    ci = lax.broadcasted_iota(jnp.int32, (c, c), 1)
    same = lambda s: (ri // s) == (ci // s)
    nils = [jnp.where(ch["strict"], -(ch["beta_col"] * ch["kk"] * ch["decay"]), 0.0) for ch in chains]
    nds = [jnp.where(same(GDN_BASE), n, 0.0) for n in nils]
    invs = [jnp.where(ri == ci, 1.0, 0.0) + nd for nd in nds]
    p16s = [nd.astype(BF16) for nd in nds]
    side()
    for _ in range(int(math.log2(GDN_BASE)) - 1):
        p16s = [_dot(p16, p16).astype(BF16) for p16 in p16s]
        side()
        invs = [inv + _dot(p16, inv.astype(BF16)) for p16, inv in zip(p16s, invs)]
    s = GDN_BASE
    while s < c:
        off = jnp.logical_and(same(2 * s), jnp.logical_not(same(s)))
        inv16s = [inv.astype(BF16) for inv in invs]
        side()
        m16s = [_dot(inv16, jnp.where(off, n, 0.0).astype(BF16)).astype(BF16) for inv16, n in zip(inv16s, nils)]
        side()
        invs = [inv + _dot(m16, inv16) for inv, m16, inv16 in zip(invs, m16s, inv16s)]
        s *= 2
    side()
    y16s = [_dot(inv.astype(BF16),
                 jnp.concatenate([ch["v"] * ch["beta_col"], ch["k"] * (ch["beta_col"] * ch["e_col"])], axis=-1).astype(BF16)
                 ).astype(BF16) for inv, ch in zip(invs, chains)]
    side()
    kts = [_dot_tn((ch["k"] * ch["e_end"]).astype(BF16), y16) for ch, y16 in zip(chains, y16s)]
    qys = [_dot((ch["qk_raw"] * ch["decay"]).astype(BF16), y16) for ch, y16 in zip(chains, y16s)]
    while side_tasks:
        side()
    out = []
    for ch, kt, qy in zip(chains, kts, qys):
        q_eff = ch["q"] * ch["e_col"] - qy[:, GDN_DV:]
        out.append(((-kt[:, GDN_DV:]).astype(BF16), kt[:, :GDN_DV], q_eff.astype(BF16), qy[:, :GDN_DV],
                    jnp.broadcast_to(ch["g_tot"], (1, GDN_DV))))
    return out


def _gdn_kernel(qc, kc, vc, zc, ql, kl, vl, zl, bg, ng_ref, oc_ref, ol_ref,
                q_all, k_all, v_all, o_dir, a_ref, b_ref, qe_ref, oz_ref, gt_ref, s_ref):
    ncc = CTX_LEN // GDN_CK
    for src, dst in ((qc, q_all), (kc, k_all), (vc, v_all)):
        dst[0:CTX_LEN, :] = src[...]
    for src, dst in ((ql, q_all), (kl, k_all), (vl, v_all)):
        for r in range(SEQ // PIECE):
            dst[CTX_LEN + r * PIECE:CTX_LEN + (r + 1) * PIECE, :] = src[r * PIECE:(r + 1) * PIECE, :]

    def chunk_at(d, p):
        if d == 0:
            return p
        return jnp.where(p < ncc, ncc - 1 - p, GDN_NCHUNK + ncc - 1 - p)

    def rows_of(cc):
        if isinstance(cc, int):
            return pl.ds(cc * GDN_CK, GDN_CK)
        return pl.ds(pl.multiple_of(cc * GDN_CK, GDN_CK), GDN_CK)

    def prepare(g, side_tasks=()):
        chains = []
        for jj in range(GDN_GROUP):
            for d in range(2):
                cc = chunk_at(d, g * GDN_GROUP + jj)
                rows = rows_of(cc)
                qv, kv, vv = q_all[rows, :], k_all[rows, :], v_all[rows, :]
                k16 = kv.astype(BF16)
                kq = _dot_nt(jnp.concatenate([k16, qv.astype(BF16)], axis=0), k16)
                chains.append(dict(q=qv, k=kv, v=vv, kk=kq[:GDN_CK], qk_raw=kq[GDN_CK:],
                                   **_gdn_decays(bg[d, pl.ds(cc, 1), :], bg[2 + d, pl.ds(cc, 1), :], rev=(d == 1))))
        terms = _gdn_chunk_terms(chains, side_tasks)
        for n, (a_neg, b_mat, q_eff, o_zero, g_tot) in enumerate(terms):
            jj, d = divmod(n, 2)
            a_ref[d, jj] = a_neg
            b_ref[d, jj] = b_mat
            qe_ref[d, jj] = q_eff
            oz_ref[d, jj] = o_zero
            gt_ref[d, jj] = g_tot

    def advance_tasks(g):
        def step(jj):
            for d in range(2):
                rows = rows_of(chunk_at(d, g * GDN_GROUP + jj))
                s = s_ref[d]
                s16 = s.astype(BF16)
                o_dir[d, rows, :] = oz_ref[d, jj] + _dot(qe_ref[d, jj], s16)
                s_ref[d] = s * gt_ref[d, jj] + _dot(a_ref[d, jj], s16) + b_ref[d, jj]
        return [functools.partial(step, jj) for jj in range(GDN_GROUP)]

    s_ref[...] = jnp.zeros(s_ref.shape, F32)
    n_groups = GDN_NCHUNK // GDN_GROUP
    prepare(0)

    def body(g, carry):
        prepare(g, advance_tasks(g - 1))
        return carry
    lax.fori_loop(1, n_groups, body, 0)
    for task in advance_tasks(n_groups - 1):
        task()

    blk = 256
    for z, o, tok0 in ((zc, oc_ref, 0), (zl, ol_ref, CTX_LEN)):
        for r in range(z.shape[0] // blk):
            rows = slice(r * blk, (r + 1) * blk)
            src = slice(tok0 + r * blk, tok0 + (r + 1) * blk)
            ov = o_dir[0, src, :] + o_dir[1, src, :]
            ov = ov * lax.rsqrt(jnp.mean(ov * ov, axis=-1, keepdims=True) + RMS_EPS) * ng_ref[...]
            o[rows, :] = ov * _silu(z[rows, :])


def _gdn(qkv_c, z_c, qkv_l, z_l, bg, norm_g):
    def tok_spec(n_tok, col0):
        return pl.BlockSpec((None, n_tok, GDN_DK), lambda b, h: (b, 0, col0 + h))

    in_specs = []
    for n_tok in (CTX_LEN, SEQ):
        in_specs += [tok_spec(n_tok, 0), tok_spec(n_tok, GDN_HEADS), tok_spec(n_tok, 2 * GDN_HEADS), tok_spec(n_tok, 0)]
    in_specs += [pl.BlockSpec((None, None, 4, GDN_NCHUNK, GDN_CK), lambda b, h: (b, h, 0, 0, 0)),
                 _const_spec((1, GDN_DV))]
    n_tok = CTX_LEN + SEQ
    return pl.pallas_call(
        _gdn_kernel,
        grid=(BATCH, GDN_HEADS),
        in_specs=in_specs,
        out_specs=[tok_spec(CTX_LEN, 0), tok_spec(SEQ, 0)],
        out_shape=[jax.ShapeDtypeStruct((BATCH, CTX_LEN, GDN_W), F32),
                   jax.ShapeDtypeStruct((BATCH, SEQ, GDN_W), F32)],
        scratch_shapes=[pltpu.VMEM((n_tok, GDN_DK), F32), pltpu.VMEM((n_tok, GDN_DK), F32),
                        pltpu.VMEM((n_tok, GDN_DV), F32), pltpu.VMEM((2, n_tok, GDN_DV), F32),
                        pltpu.VMEM((2, GDN_GROUP, GDN_DK, GDN_DV), BF16),
                        pltpu.VMEM((2, GDN_GROUP, GDN_DK, GDN_DV), F32),
                        pltpu.VMEM((2, GDN_GROUP, GDN_CK, GDN_DK), BF16),
                        pltpu.VMEM((2, GDN_GROUP, GDN_CK, GDN_DV), F32),
                        pltpu.VMEM((2, GDN_GROUP, 1, GDN_DV), F32),
                        pltpu.VMEM((2, GDN_DK, GDN_DV), F32)],
        compiler_params=_params(2), name="gdn",
    )(qkv_c, qkv_c, qkv_c, z_c, qkv_l, qkv_l, qkv_l, z_l, bg, norm_g)


def _residual_norm(x, branch, gate, g, b):
    return _layer_norm(DN_ALPHA * x + _per_sample(branch, gate, lambda a, c: a * c), g, b)


def _out_even_kernel(x_ref, a_ref, o_ref, g1_ref, wa_ref, wo_ref, lng_ref, lnb_ref, out_ref, stage_ref):
    tm = x_ref.shape[0]
    for p in range(tm // PIECE):
        rows = slice(p * PIECE, (p + 1) * PIECE)
        o = _load_token_major(o_ref, stage_ref, p * (PIECE // BATCH), PIECE // BATCH)
        m = _dot(a_ref[rows, :].astype(BF16), wa_ref[...]) + _dot(o.astype(BF16), wo_ref[...])
        out_ref[rows, :] = _residual_norm(x_ref[rows, :], m, g1_ref[...], lng_ref[...], lnb_ref[...])


def _out_even(x, a, o, mod, wa, wo, ln_g, ln_b, *, tm):
    rows = x.shape[0]
    row = lambda c: pl.BlockSpec((tm, c), lambda i: (i, 0))
    return pl.pallas_call(
        _out_even_kernel,
        grid=(rows // tm,),
        in_specs=[row(D_MODEL), row(CF_W), pl.BlockSpec((BATCH, tm // BATCH, GDN_W), lambda i: (0, i, 0)),
                  mod.spec(G1),
                  _const_spec((CF_W, D_MODEL)), _const_spec((GDN_W, D_MODEL)),
                  _const_spec((1, D_MODEL)), _const_spec((1, D_MODEL))],
        out_specs=row(D_MODEL),
        out_shape=jax.ShapeDtypeStruct((rows, D_MODEL), F32),
        scratch_shapes=[pltpu.VMEM((GDN_W // LANES, PIECE, LANES), F32)],
        compiler_params=_params(1), name="out_even",
    )(x, a, o, mod.arr, wa, wo, ln_g, ln_b)


def _gelu_tanh(x):
    return 0.5 * x * (1.0 + jnp.tanh(math.sqrt(2.0 / math.pi) * (x + 0.044715 * (x * x * x))))


def _out_odd_kernel(x_ref, u_ref, yf_ref, yb_ref, s_ref, g1_ref, d_ref, gw_ref, gb_ref, wa_ref, wo_ref,
                    lng_ref, lnb_ref, out_ref):
    tm = x_ref.shape[0]
    for p in range(tm // PIECE):
        rows = slice(p * PIECE, (p + 1) * PIECE)
        y = d_ref[...] * u_ref[rows, :] + yf_ref[rows, :] + yb_ref[rows, :]
        zg = _gelu_tanh(y)
        s5 = zg * _sigmoid(_dot(zg.astype(BF16), gw_ref[...]) + gb_ref[...])
        m = _dot(s5.astype(BF16), wa_ref[...]) + _dot(s_ref[rows, :].astype(BF16), wo_ref[...])
        out_ref[rows, :] = _residual_norm(x_ref[rows, :], m, g1_ref[...], lng_ref[...], lnb_ref[...])


def _out_odd(x, u, yf, yb, s, mod, d_skip, glu_w, glu_b, wa, wo, ln_g, ln_b, *, tm):
    rows = x.shape[0]
    row = lambda c: pl.BlockSpec((tm, c), lambda i: (i, 0))
    return pl.pallas_call(
        _out_odd_kernel,
        grid=(rows // tm,),
        in_specs=[row(D_MODEL), row(S5_W), row(S5_W), row(S5_W), row(SC_W), mod.spec(G1),
                  _const_spec((1, S5_W)), _const_spec((S5_W, S5_W)), _const_spec((1, S5_W)),
                  _const_spec((S5_W, D_MODEL)), _const_spec((SC_W, D_MODEL)),
                  _const_spec((1, D_MODEL)), _const_spec((1, D_MODEL))],
        out_specs=row(D_MODEL),
        out_shape=jax.ShapeDtypeStruct((rows, D_MODEL), F32),
        compiler_params=_params(1), name="out_odd",
    )(x, u, yf, yb, s, mod.arr, d_skip, glu_w, glu_b, wa, wo, ln_g, ln_b)


def _conv3(pad_ref, s, r0, n, cw_ref, shift):
    return (cw_ref[0:1, :] * pad_ref[s, r0:r0 + n, :]
            + cw_ref[1:2, :] * pad_ref[s, r0 + shift:r0 + shift + n, :]
            + cw_ref[2:3, :] * pad_ref[s, r0 + 2 * shift:r0 + 2 * shift + n, :])


def _ffn_kernel(x_ref, sc_ref, sh_ref, g2_ref, wu_ref, cw_ref, wd_ref, lng_ref, lnb_ref, out_ref,
                h_ref, val_ref, pad_ref, *, tm, shift, seg):
    j = pl.program_id(1)
    n_pieces = tm // PIECE
    tf = val_ref.shape[1]

    @pl.when(j == 0)
    def _():
        zeros = jnp.zeros((shift, tf), F32)
        for s in range(tm // seg):
            pad_ref[s, 0:shift, :] = zeros
            pad_ref[s, shift + seg:shift + seg + shift, :] = zeros
        for p in range(n_pieces):
            x = _load_rows(x_ref, p * PIECE, PIECE)
            h_ref[p * PIECE:(p + 1) * PIECE, :] = _modulate(x, sc_ref[...], sh_ref[...]).astype(BF16)

    for p in range(n_pieces):
        h = h_ref[p * PIECE:(p + 1) * PIECE, :]
        up = _dot(h, wu_ref[...])
        val_ref[p * PIECE:(p + 1) * PIECE, :] = up[:, :tf]
        s, off = divmod(p * PIECE, seg)
        pad_ref[s, shift + off:shift + off + PIECE, :] = up[:, tf:]

    for p in range(n_pieces):
        s, off = divmod(p * PIECE, seg)
        gate = _conv3(pad_ref, s, off, PIECE, cw_ref, shift)
        act = (val_ref[p * PIECE:(p + 1) * PIECE, :] * _silu(gate)).astype(BF16)
        part = _dot(act, wd_ref[...])

        @pl.when(j == 0)
        def _():
            _store_rows(out_ref, p * PIECE, part)

        @pl.when(j > 0)
        def _():
            _store_rows(out_ref, p * PIECE, _load_rows(out_ref, p * PIECE, PIECE) + part)

    @pl.when(j == pl.num_programs(1) - 1)
    def _():
        for p in range(n_pieces):
            x = _load_rows(x_ref, p * PIECE, PIECE)
            y = _residual_norm(x, _load_rows(out_ref, p * PIECE, PIECE), g2_ref[...], lng_ref[...], lnb_ref[...])
            _store_rows(out_ref, p * PIECE, y)


def _ffn(x, mod, w_up, cw, w_down, ln_g, ln_b, *, tiling, shift, seg, tf):
    tm = tiling.tm
    n_f = FFN_F // tf
    out = pl.pallas_call(
        functools.partial(_ffn_kernel, tm=tm, shift=shift, seg=seg),
        grid=(tiling.n_tiles, n_f),
        in_specs=[tiling.spec(D_MODEL), mod.spec(SC2), mod.spec(SH2), mod.spec(G2),
                  pl.BlockSpec((D_MODEL, 2 * tf), lambda i, j: (0, j)),
                  pl.BlockSpec((3, tf), lambda i, j: (0, j)),
                  pl.BlockSpec((tf, D_MODEL), lambda i, j: (j, 0)),
                  _const_spec((1, D_MODEL)), _const_spec((1, D_MODEL))],
        out_specs=tiling.spec(D_MODEL),
        out_shape=jax.ShapeDtypeStruct(tiling.shape(D_MODEL), F32),
        scratch_shapes=[pltpu.VMEM((tm, D_MODEL), BF16), pltpu.VMEM((tm, tf), F32),
                        pltpu.VMEM((tm // seg, seg + 2 * shift, tf), F32)],
        compiler_params=_params(2), name="conv_ffn",
    )(tiling.view(x), mod.arr, mod.arr, mod.arr, w_up, cw, w_down, ln_g, ln_b)
    return tiling.unview(out)


def _odd_in_kernel(x_ref, sc_ref, sh_ref, wu_ref, wb_ref, wc_ref, wx_ref, cw_ref, u_ref, s_ref,
                   h_ref, b_ref, pad_ref, *, tm, shift, seg):
    j = pl.program_id(1)
    n_pieces = tm // PIECE
    cb = u_ref.shape[-1]

    @pl.when(j == 0)
    def _():
        zeros = jnp.zeros((shift, cb), F32)
        for s in range(tm // seg):
            pad_ref[s, 0:shift, :] = zeros
            pad_ref[s, shift + seg:shift + seg + shift, :] = zeros
        for p in range(n_pieces):
            x = _load_rows(x_ref, p * PIECE, PIECE)
            h_ref[p * PIECE:(p + 1) * PIECE, :] = _modulate(x, sc_ref[...], sh_ref[...]).astype(BF16)

    for p in range(n_pieces):
        h = h_ref[p * PIECE:(p + 1) * PIECE, :]
        _store_rows(u_ref, p * PIECE, _dot(h, wu_ref[...]))
        b_ref[p * PIECE:(p + 1) * PIECE, :] = _dot(h, wb_ref[...])
        s, off = divmod(p * PIECE, seg)
        pad_ref[s, shift + off:shift + off + PIECE, :] = _dot(h, wc_ref[...]) * _dot(h, wx_ref[...])

    for p in range(n_pieces):
        s, off = divmod(p * PIECE, seg)
        _store_rows(s_ref, p * PIECE, b_ref[p * PIECE:(p + 1) * PIECE, :] * _conv3(pad_ref, s, off, PIECE, cw_ref, shift))


def _odd_in(x, mod, w_in, cw, *, tiling, shift, seg):
    tm = tiling.tm
    cb = 2 * LANES
    nb = S5_W // cb
    wspec = lambda k: pl.BlockSpec((D_MODEL, cb), lambda i, j: (0, k * nb + j))
    u, s = pl.pallas_call(
        functools.partial(_odd_in_kernel, tm=tm, shift=shift, seg=seg),
        grid=(tiling.n_tiles, nb),
        in_specs=[tiling.spec(D_MODEL), mod.spec(SC1), mod.spec(SH1),
                  wspec(0), wspec(1), wspec(2), wspec(3),
                  pl.BlockSpec((3, cb), lambda i, j: (0, j))],
        out_specs=[tiling.spec(cb, lambda i, j: j), tiling.spec(cb, lambda i, j: j)],
        out_shape=[jax.ShapeDtypeStruct(tiling.shape(S5_W), F32), jax.ShapeDtypeStruct(tiling.shape(SC_W), F32)],
        scratch_shapes=[pltpu.VMEM((tm, D_MODEL), BF16), pltpu.VMEM((tm, cb), F32),
                        pltpu.VMEM((tm // seg, seg + 2 * shift, cb), F32)],
        compiler_params=_params(2), name="odd_in",
    )(tiling.view(x), mod.arr, mod.arr, w_in, w_in, w_in, w_in, cw)
    return tiling.unview(u), tiling.unview(s)


def _s5_kernel(uf_ref, ub_ref, wb_ref, wc_ref, a_ref, x0_ref, yf_ref, yb_ref, xfin_ref, bu_ref, st_ref):
    t = pl.program_id(0)
    rows = uf_ref.shape[0]
    n_tok = rows // SUBLANES
    hs = S5_HSTATE

    @pl.when(t == 0)
    def _():
        st_ref[...] = x0_ref[...]

    for d, u_ref in enumerate((uf_ref, ub_ref)):
        for hf in range(2):
            u = u_ref[:, hf * S5_HALF:(hf + 1) * S5_HALF].astype(BF16)
            bu_ref[d, :, hf * 2 * hs:(hf + 1) * 2 * hs] = _dot(u, wb_ref[d, hf])

    for d, y_ref in enumerate((yf_ref, yb_ref)):
        for hf in range(2):
            c_re = hf * 2 * hs
            c_im = c_re + hs
            a_re = jnp.broadcast_to(a_ref[d, hf, 0:1, :], (SUBLANES, hs))
            a_im = jnp.broadcast_to(a_ref[d, hf, 1:2, :], (SUBLANES, hs))
            xr, xi = st_ref[d, :, c_re:c_re + hs], st_ref[d, :, c_im:c_im + hs]
            for i in range(n_tok):
                tok = i if d == 0 else n_tok - 1 - i
                r = slice(tok * SUBLANES, (tok + 1) * SUBLANES)
                xr, xi = (a_re * xr - a_im * xi + bu_ref[d, r, c_re:c_re + hs],
                          a_re * xi + a_im * xr + bu_ref[d, r, c_im:c_im + hs])
                bu_ref[d, r, c_re:c_re + hs] = xr
                bu_ref[d, r, c_im:c_im + hs] = xi
            st_ref[d, :, c_re:c_re + hs] = xr
            st_ref[d, :, c_im:c_im + hs] = xi
        for hf in range(2):
            xs = bu_ref[d, :, hf * 2 * hs:(hf + 1) * 2 * hs].astype(BF16)
            y_ref[:, hf * S5_HALF:(hf + 1) * S5_HALF] = _dot(xs, wc_ref[d, hf])

    @pl.when(t == pl.num_programs(0) - 1)
    def _():
        xfin_ref[...] = st_ref[...]


def _s5(u, wb, wc, a, x0):
    rows = u.shape[0]
    tr = S5_TOK * SUBLANES
    nt = rows // tr
    state = (2, SUBLANES, 4 * S5_HSTATE)
    return pl.pallas_call(
        _s5_kernel,
        grid=(nt,),
        in_specs=[pl.BlockSpec((tr, S5_W), lambda t: (t, 0)),
                  pl.BlockSpec((tr, S5_W), lambda t: (nt - 1 - t, 0)),
                  _const_spec((2, 2, S5_HALF, 2 * S5_HSTATE)), _const_spec((2, 2, 2 * S5_HSTATE, S5_HALF)),
                  _const_spec((2, 2, 2, S5_HSTATE)), _const_spec(state)],
        out_specs=[pl.BlockSpec((tr, S5_W), lambda t: (t, 0)),
                   pl.BlockSpec((tr, S5_W), lambda t: (nt - 1 - t, 0)),
                   _const_spec(state)],
        out_shape=[jax.ShapeDtypeStruct((rows, S5_W), F32), jax.ShapeDtypeStruct((rows, S5_W), F32),
                   jax.ShapeDtypeStruct(state, F32)],
        scratch_shapes=[pltpu.VMEM((2, tr, 4 * S5_HSTATE), F32), pltpu.VMEM(state, F32)],
        compiler_params=_params(1), name="s5_scan",
    )(u, u, wb, wc, a, x0)


def _s5_weights(lam_re, lam_im, log_dt, b_re, b_im, c_re, c_im):
    eye = jnp.eye(S5_G // 2, dtype=F32)
    wbs, wcs, avs = [], [], []
    for di in range(2):
        lr, li = lam_re[di], lam_im[di]
        dt = jnp.exp(log_dt[di])[:, None]
        mag = jnp.exp(lr * dt)
        ar, ai = mag * jnp.cos(li * dt), mag * jnp.sin(li * dt)
        den = lr * lr + li * li
        fr = ((ar - 1.0) * lr + ai * li) / den
        fi = (ai * lr - (ar - 1.0) * li) / den
        bbr = fr[..., None] * b_re - fi[..., None] * b_im
        bbi = fr[..., None] * b_im + fi[..., None] * b_re

        def in_block(t):
            t = t.reshape(2, S5_G // 2, S5_N, S5_P)
            return jnp.einsum("hgnp,gk->hgpkn", t, eye).reshape(2, S5_HALF, S5_HSTATE)

        def out_block(t):
            t = t.reshape(2, S5_G // 2, S5_P, S5_N)
            return jnp.einsum("hgpn,gk->hgnkp", t, eye).reshape(2, S5_HSTATE, S5_HALF)

        wbs.append(jnp.concatenate([in_block(bbr), in_block(bbi)], axis=-1))
        wcs.append(jnp.concatenate([out_block(c_re), out_block(-c_im)], axis=1))
        avs.append(jnp.stack([ar.reshape(2, S5_HSTATE), ai.reshape(2, S5_HSTATE)], axis=1))
    return jnp.stack(wbs).astype(BF16), jnp.stack(wcs).astype(BF16), jnp.stack(avs)


def _bg_rows(bg, n_tok):
    t = bg.reshape(n_tok, BATCH, LANES)[:, :, :4 * GDN_HEADS].reshape(n_tok, BATCH, 4, GDN_HEADS)
    return t.transpose(1, 3, 2, 0).reshape(BATCH, GDN_HEADS, 4, n_tok // GDN_CK, GDN_CK)


def _to_batch_major(a, n_tok):
    return a.reshape(n_tok, BATCH, a.shape[-1]).transpose(1, 0, 2)


def _to_token_major(a):
    return a.transpose(1, 0, 2).reshape(a.shape[0] * a.shape[1], a.shape[2])


def kernel(x, c, ctx, c_ctx, ada_w, ada_b, ln_g, ln_b, ev_w_in, ev_w_out, cf_conv, cf_ln_g, cf_ln_b, gdn_conv,
           gdn_a_log, gdn_dt_bias, gdn_norm_g, od_w_in, od_w_out, s5_lam_re, s5_lam_im, s5_log_dt, s5_b_re,
           s5_b_im, s5_c_re, s5_c_im, s5_d, s5_glu_w, s5_glu_b, sc_conv, ffn_w_up, ffn_conv, ffn_w_down):
    xl = _to_token_major(x)
    cl = _to_token_major(ctx)

    cvec = jnp.concatenate([c, jnp.broadcast_to(c_ctx[None, :], (SUBLANES, D_MODEL))], axis=0)
    mods = _ada(cvec, ada_w, ada_b).reshape(DEPTH, 2, SUBLANES, 6 * D_MODEL)

    lat_row = _Tiling("row", ROWS_LAT, 2 * SEG_ROW)
    lat_col = _Tiling("col", ROWS_LAT, 2 * SEG_ROW)
    ctx_row = _Tiling("row", ROWS_CTX, ROWS_CTX)
    lat_along_row = dict(tiling=lat_row, shift=SUBLANES, seg=SEG_ROW)
    lat_along_col = dict(tiling=lat_col, shift=lat_col.rg, seg=lat_col.tm)
    ctx_conv = dict(tiling=ctx_row, shift=SUBLANES, seg=ROWS_CTX)

    row2 = lambda v: v.reshape(1, -1)
    for i in range(DEPTH):
        j = i // 2
        last = i == DEPTH - 1
        m_lat, m_ctx = _Mod(mods, i, 0), _Mod(mods, i, 1)
        lng1, lnb1, lng2, lnb2 = row2(ln_g[i, 0]), row2(ln_b[i, 0]), row2(ln_g[i, 1]), row2(ln_b[i, 1])
        w_down, f_cw = ffn_w_down[i].astype(BF16), ffn_conv[i]

        def up_blocks(tf, w_up=ffn_w_up[i]):
            cols = [w_up[:, k * FFN_F + j * tf:k * FFN_F + (j + 1) * tf] for j in range(FFN_F // tf) for k in range(2)]
            return jnp.concatenate(cols, axis=1).astype(BF16)

        if i % 2 == 0:
            w_in = ev_w_in[j]
            w_cf = w_in[:, :2 * CF_W].astype(BF16)
            w_qkv = w_in[:, 2 * CF_W:2 * CF_W + GDN_CONV_W].astype(BF16)
            w_zbg = jnp.pad(w_in[:, 2 * CF_W + GDN_CONV_W:], ((0, 0), (0, LANES - 4 * GDN_HEADS))).astype(BF16)
            pad16 = lambda v: jnp.pad(v.reshape(1, -1), ((0, 0), (2 * GDN_HEADS, LANES - 4 * GDN_HEADS)))
            alog, dtb = pad16(gdn_a_log[j]), pad16(gdn_dt_bias[j])
            w_out = ev_w_out[j].astype(BF16)
            parts = []
            for xs, m, tm, seg, n_tok in ((cl, m_ctx, ROWS_CTX, ROWS_CTX, CTX_LEN), (xl, m_lat, 4 * SEG_ROW, SEG_ROW, SEQ)):
                a = _cf_mixer(xs, m, w_cf, cf_conv[j], row2(cf_ln_g[j]), row2(cf_ln_b[j]), tm=min(tm, 2 * seg), seg=seg)
                qkv = _qkv_proj(xs, m, w_qkv, gdn_conv[j], tm=tm, seg=seg)
                z, bg = _zbg_proj(xs, m, w_zbg, alog, dtb, tm=tm)
                parts.append((a, qkv, z, _bg_rows(bg, n_tok)))
            (a_c, qkv_c, z_c, bg_c), (a_l, qkv_l, z_l, bg_l) = parts
            o_c, o_l = _gdn(qkv_c, z_c, qkv_l, z_l, jnp.concatenate([bg_c, bg_l], axis=3), row2(gdn_norm_g[j]))
            x1 = _out_even(xl, a_l, o_l, m_lat, w_out[:CF_W], w_out[CF_W:], lng1, lnb1, tm=2 * SEG_ROW)
            if not last:
                c1 = _out_even(cl, a_c, o_c, m_ctx, w_out[:CF_W], w_out[CF_W:], lng1, lnb1, tm=2 * SEG_ROW)
            lat_ffn = lat_along_col
        else:
            w_in = od_w_in[j].astype(BF16)
            w_out = od_w_out[j].astype(BF16)
            wb, wc, av = _s5_weights(s5_lam_re[j], s5_lam_im[j], s5_log_dt[j], s5_b_re[j], s5_b_im[j],
                                     s5_c_re[j], s5_c_im[j])
            u_c, s_c = _odd_in(cl, m_ctx, w_in, sc_conv[j], **ctx_conv)
            u_l, s_l = _odd_in(xl, m_lat, w_in, sc_conv[j], **lat_along_col)
            zero_state = jnp.zeros((2, SUBLANES, 4 * S5_HSTATE), F32)
            yf_c, yb_c, fin_c = _s5(u_c, wb, wc, av, zero_state)
            yf_l, yb_l, _ = _s5(u_l, wb, wc, av, fin_c)
            odd_w = (row2(s5_d[j]), s5_glu_w[j].astype(BF16), row2(s5_glu_b[j]), w_out[:S5_W], w_out[S5_W:])
            x1 = _out_odd(xl, u_l, yf_l, yb_l, s_l, m_lat, *odd_w, lng1, lnb1, tm=2 * SEG_ROW)
            if not last:
                c1 = _out_odd(cl, u_c, yf_c, yb_c, s_c, m_ctx, *odd_w, lng1, lnb1, tm=2 * SEG_ROW)
            lat_ffn = lat_along_row

        xl = _ffn(x1, m_lat, up_blocks(FFN_TF_LAT), f_cw, w_down, lng2, lnb2,
                  tf=FFN_TF_LAT, **lat_ffn)
        if not last:
            cl = _ffn(c1, m_ctx, up_blocks(FFN_TF_CTX), f_cw, w_down, lng2, lnb2,
                      tf=FFN_TF_CTX, **ctx_conv)

    return xl.reshape(SEQ, BATCH, D_MODEL).transpose(1, 0, 2)
```

```python
import functools
import math

import jax
import jax.numpy as jnp
from jax import lax
from jax.experimental import pallas as pl
from jax.experimental.pallas import tpu as pltpu

D_MODEL = 1024
BATCH = 8
SEQ = 2048
DEPTH = 4
GRID_W = 64
GRID_H = SEQ // GRID_W
CTX_LEN = 256
CF_W = 512
CF_CONV = 31
GDN_HEADS = 4
GDN_DK = 128
GDN_DV = 128
GDN_QK = GDN_HEADS * GDN_DK
GDN_W = GDN_HEADS * GDN_DV
GDN_CONV_W = 2 * GDN_QK + GDN_W
S5_W = 512
S5_P = 16
S5_G = S5_W // S5_P
S5_N = 64
SC_W = 512
FFN_F = 2816
DN_ALPHA = (2 * DEPTH) ** 0.25
LN_EPS = 1e-5
RMS_EPS = 1e-6

SUBLANES = 8
LANES = 128
VMEM_LIMIT_BYTES = 56 * 1024 * 1024

ROWS_LAT = SEQ * BATCH
ROWS_CTX = CTX_LEN * BATCH
SEG_ROW = GRID_W * BATCH
PIECE = 512
CONV_ROWS = 64
FFN_TF_LAT = FFN_F // 2
FFN_TF_CTX = 256
S5_HALF = S5_W // 2
S5_HSTATE = (S5_G // 2) * S5_N
S5_TOK = 64
GDN_CK = 64
GDN_BASE = 16
GDN_NCHUNK = (CTX_LEN + SEQ) // GDN_CK
GDN_GROUP = 9

F32 = jnp.float32
BF16 = jnp.bfloat16

assert BATCH == SUBLANES


def _dot(a, b):
    return jnp.dot(a, b, preferred_element_type=F32)


def _dot_nt(a, b):
    return lax.dot_general(a, b, (((1,), (1,)), ((), ())), preferred_element_type=F32)


def _dot_tn(a, b):
    return lax.dot_general(a, b, (((0,), (0,)), ((), ())), preferred_element_type=F32)


def _sigmoid(x):
    return 1.0 / (1.0 + jnp.exp(-x))


def _silu(x):
    return x * _sigmoid(x)


def _per_sample(x, vec, op):
    r, c = x.shape
    x3 = x.reshape(r // SUBLANES, SUBLANES, c)
    return op(x3, vec[None]).reshape(r, c)


def _modulate(x, scale, shift):
    y = _per_sample(x, 1.0 + scale, lambda a, b: a * b)
    return _per_sample(y, shift, lambda a, b: a + b)


def _layer_norm(x, g, b):
    mu = jnp.mean(x, axis=-1, keepdims=True)
    xc = x - mu
    var = jnp.mean(xc * xc, axis=-1, keepdims=True)
    return xc * lax.rsqrt(var + LN_EPS) * g + b


def _load_rows(ref, start, n):
    if len(ref.shape) == 2:
        return ref[start:start + n, :]
    rg = ref.shape[1]
    return ref[start // rg:(start + n) // rg, :, :].reshape(n, ref.shape[2])


def _store_rows(ref, start, value):
    n = value.shape[0]
    if len(ref.shape) == 2:
        ref[start:start + n, :] = value
    else:
        rg = ref.shape[1]
        ref[start // rg:(start + n) // rg, :, :] = value.reshape(n // rg, rg, value.shape[1])


def _params(n_axes):
    return pltpu.CompilerParams(dimension_semantics=("arbitrary",) * n_axes,
                                vmem_limit_bytes=VMEM_LIMIT_BYTES)


class _Tiling:
    def __init__(self, kind, rows, tm):
        self.kind, self.rows, self.tm = kind, rows, tm
        self.n_tiles = rows // tm
        if kind == "col":
            self.rg = tm // GRID_H

    def view(self, a):
        if self.kind == "row":
            return a
        return a.reshape(GRID_H, SEG_ROW, a.shape[-1])

    def unview(self, a):
        return a.reshape(self.rows, a.shape[-1])

    def shape(self, c):
        return (self.rows, c) if self.kind == "row" else (GRID_H, SEG_ROW, c)

    def spec(self, c, col=None):
        col = col or (lambda *ij: 0)
        if self.kind == "row":
            return pl.BlockSpec((self.tm, c), lambda *ij: (ij[0], col(*ij)))
        return pl.BlockSpec((GRID_H, self.rg, c), lambda *ij: (0, ij[0], col(*ij)))


def _const_spec(shape):
    nd = len(shape)
    return pl.BlockSpec(shape, lambda *ij: (0,) * nd)


SH1, SC1, G1, SH2, SC2, G2 = range(6)


class _Mod:
    def __init__(self, arr, layer, stream):
        self.arr, self.layer, self.stream = arr, layer, stream

    def spec(self, col):
        layer, stream = self.layer, self.stream
        return pl.BlockSpec((None, None, SUBLANES, D_MODEL), lambda *ij: (layer, stream, 0, col))


def _ada_kernel(c_ref, w_ref, b_ref, o_ref):
    s = _silu(c_ref[...]).astype(BF16)
    o_ref[...] = _dot(s, w_ref[...].astype(BF16)) + b_ref[...]


def _ada(cvec, ada_w, ada_b):
    return pl.pallas_call(
        _ada_kernel,
        grid=(DEPTH, 6),
        in_specs=[_const_spec((2 * SUBLANES, D_MODEL)),
                  pl.BlockSpec((None, D_MODEL, D_MODEL), lambda i, j: (i, 0, j)),
                  pl.BlockSpec((None, 1, D_MODEL), lambda i, j: (i, 0, j))],
        out_specs=pl.BlockSpec((None, 2 * SUBLANES, D_MODEL), lambda i, j: (i, 0, j)),
        out_shape=jax.ShapeDtypeStruct((DEPTH, 2 * SUBLANES, 6 * D_MODEL), F32),
        compiler_params=_params(2), name="ada",
    )(cvec, ada_w, ada_b.reshape(DEPTH, 1, 6 * D_MODEL))


def _cf_kernel(x_ref, sc_ref, sh_ref, w_ref, cw_ref, g_ref, b_ref, o_ref, pad_ref, *, seg):
    tm = x_ref.shape[0]
    halo = (CF_CONV // 2) * SUBLANES
    zeros = jnp.zeros((halo, CF_W), F32)
    for s in range(tm // seg):
        pad_ref[s, 0:halo, :] = zeros
        pad_ref[s, halo + seg:halo + seg + halo, :] = zeros
    def project(p):
        h = _modulate(x_ref[p * PIECE:(p + 1) * PIECE, :], sc_ref[...], sh_ref[...]).astype(BF16)
        y = _dot(h, w_ref[...])
        glu = y[:, :CF_W] * _sigmoid(y[:, CF_W:])
        s, off = divmod(p * PIECE, seg)
        pad_ref[s, halo + off:halo + off + PIECE, :] = glu

    for p in range(tm // PIECE):
        project(p)
    for s in range(tm // seg):
        def body(i, carry, s=s):
            for r0 in (pl.multiple_of(i * 2 * CONV_ROWS, CONV_ROWS), pl.multiple_of((i * 2 + 1) * CONV_ROWS, CONV_ROWS)):
                acc = cw_ref[0:1, :] * pad_ref[s, pl.ds(r0, CONV_ROWS), :]
                for k in range(1, CF_CONV):
                    acc = acc + cw_ref[k:k + 1, :] * pad_ref[s, pl.ds(r0 + k * SUBLANES, CONV_ROWS), :]
                o_ref[pl.ds(s * seg + r0, CONV_ROWS), :] = _silu(_layer_norm(acc, g_ref[...], b_ref[...]))
            return carry
        lax.fori_loop(0, seg // (2 * CONV_ROWS), body, 0)


def _cf_mixer(x, mod, w, cw, ln_g, ln_b, *, tm, seg):
    rows = x.shape[0]
    halo = (CF_CONV // 2) * SUBLANES
    return pl.pallas_call(
        functools.partial(_cf_kernel, seg=seg),
        grid=(rows // tm,),
        in_specs=[pl.BlockSpec((tm, D_MODEL), lambda i: (i, 0)), mod.spec(SC1), mod.spec(SH1),
                  _const_spec((D_MODEL, 2 * CF_W)), _const_spec((CF_CONV, CF_W)),
                  _const_spec((1, CF_W)), _const_spec((1, CF_W))],
        out_specs=pl.BlockSpec((tm, CF_W), lambda i: (i, 0)),
        out_shape=jax.ShapeDtypeStruct((rows, CF_W), F32),
        scratch_shapes=[pltpu.VMEM((tm // seg, seg + 2 * halo, CF_W), F32)],
        compiler_params=_params(1), name="cf_mixer",
    )(x, mod.arr, mod.arr, w, cw, ln_g, ln_b)


def _store_batch_major(dst_ref, stage_ref, value, tok0, n_tok, stage_row0=0):
    for c in range(stage_ref.shape[0]):
        lanes = slice(c * LANES, (c + 1) * LANES)
        stage_ref[c, stage_row0:stage_row0 + n_tok * BATCH, :] = value[:, lanes]
        for b in range(BATCH):
            dst_ref[b, tok0:tok0 + n_tok, lanes] = stage_ref[c, pl.ds(stage_row0 + b, n_tok, stride=BATCH), :]


def _load_token_major(src_ref, stage_ref, tok0, n_tok):
    cols = []
    for c in range(stage_ref.shape[0]):
        lanes = slice(c * LANES, (c + 1) * LANES)
        for b in range(BATCH):
            stage_ref[c, pl.ds(b, n_tok, stride=BATCH), :] = src_ref[b, pl.ds(tok0, n_tok), lanes]
        cols.append(stage_ref[c])
    return jnp.concatenate(cols, axis=-1)


def _qkv_kernel(x_ref, sc_ref, sh_ref, w_ref, cw_ref, o_ref, h_ref, pad_ref, stage_ref, *, seg):
    j = pl.program_id(1)
    tm = x_ref.shape[0]
    halo = SUBLANES

    @pl.when(j == 0)
    def _():
        zeros = jnp.zeros((halo, GDN_QK), F32)
        for s in range(tm // seg):
            pad_ref[s, 0:halo, :] = zeros
            pad_ref[s, halo + seg:halo + seg + halo, :] = zeros
        for p in range(tm // PIECE):
            rows = slice(p * PIECE, (p + 1) * PIECE)
            h_ref[rows, :] = _modulate(x_ref[rows, :], sc_ref[...], sh_ref[...]).astype(BF16)

    is_qk = j < 2
    scale = jnp.where(j == 0, GDN_DK ** -0.5, 1.0).astype(F32)
    n_pieces = tm // PIECE
    blk = 2 * CONV_ROWS

    def project(p):
        s, off = divmod(p * PIECE, seg)
        pad_ref[s, halo + off:halo + off + PIECE, :] = _dot(h_ref[p * PIECE:(p + 1) * PIECE, :], w_ref[...])

    def finish(p):
        s, off = divmod(p * PIECE, seg)
        for r0 in range(off, off + PIECE, blk):
            y = (cw_ref[0:1, :] * pad_ref[s, r0:r0 + blk, :]
                 + cw_ref[1:2, :] * pad_ref[s, r0 + halo:r0 + halo + blk, :]
                 + cw_ref[2:3, :] * pad_ref[s, r0 + 2 * halo:r0 + 2 * halo + blk, :])
            y = _silu(y)
            heads = []
            for hh in range(GDN_HEADS):
                t = y[:, hh * GDN_DK:(hh + 1) * GDN_DK]
                nrm = t * (lax.rsqrt(jnp.sum(t * t, axis=-1, keepdims=True) + RMS_EPS) * scale)
                heads.append(jnp.where(is_qk, nrm, t))
            row0 = s * seg + r0
            _store_batch_major(o_ref, stage_ref, jnp.concatenate(heads, axis=-1), row0 // SUBLANES, blk // SUBLANES,
                               stage_row0=row0)

    project(0)
    for p in range(n_pieces):
        if p + 1 < n_pieces:
            project(p + 1)
        finish(p)


def _qkv_proj(x, mod, w, cw, *, tm, seg):
    rows = x.shape[0]
    return pl.pallas_call(
        functools.partial(_qkv_kernel, seg=seg),
        grid=(rows // tm, 3),
        in_specs=[pl.BlockSpec((tm, D_MODEL), lambda i, j: (i, 0)), mod.spec(SC1), mod.spec(SH1),
                  pl.BlockSpec((D_MODEL, GDN_QK), lambda i, j: (0, j)),
                  pl.BlockSpec((3, GDN_QK), lambda i, j: (0, j))],
        out_specs=pl.BlockSpec((BATCH, tm // BATCH, GDN_QK), lambda i, j: (0, i, j)),
        out_shape=jax.ShapeDtypeStruct((BATCH, rows // BATCH, GDN_CONV_W), F32),
        scratch_shapes=[pltpu.VMEM((tm, D_MODEL), BF16),
                        pltpu.VMEM((tm // seg, seg + 2 * SUBLANES, GDN_QK), F32),
                        pltpu.VMEM((GDN_QK // LANES, tm, LANES), F32)],
        compiler_params=_params(2), name="qkv_proj",
    )(x, mod.arr, mod.arr, w, cw)


def _zbg_kernel(x_ref, sc_ref, sh_ref, w_ref, alog_ref, dtb_ref, z_ref, bg_ref, stage_ref):
    tm = x_ref.shape[0]
    for p in range(tm // PIECE):
        rows = slice(p * PIECE, (p + 1) * PIECE)
        h = _modulate(x_ref[rows, :], sc_ref[...], sh_ref[...]).astype(BF16)
        y = _dot(h, w_ref[...])
        _store_batch_major(z_ref, stage_ref, y[:, :GDN_W], p * (PIECE // BATCH), PIECE // BATCH)
        s = y[:, GDN_W:]
        t = s + dtb_ref[...]
        softplus = jnp.maximum(t, 0.0) + jnp.log1p(jnp.exp(-jnp.abs(t)))
        g = -jnp.exp(alog_ref[...]) * softplus
        lane = lax.broadcasted_iota(jnp.int32, s.shape, 1)
        bg_ref[rows, :] = jnp.where(lane < 2 * GDN_HEADS, _sigmoid(s), g)


def _even_in_kernel(x_ref, sc_ref, sh_ref, wcf_ref, wqkv_ref, wzbg_ref, cfw_ref, lng_ref, lnb_ref, gcw_ref,
                    alog_ref, dtb_ref, a_ref, qkv_ref, z_ref, bg_ref, cfpad_ref, qpad_ref, stage_ref):
    cf_halo = (CF_CONV // 2) * SUBLANES
    q_halo = SUBLANES
    cfpad_ref[0:cf_halo, :] = jnp.zeros((cf_halo, CF_W), F32)
    cfpad_ref[cf_halo + SEG_ROW:, :] = jnp.zeros((cf_halo, CF_W), F32)
    qpad_ref[0:q_halo, :] = jnp.zeros((q_halo, GDN_CONV_W), F32)
    qpad_ref[q_halo + SEG_ROW:, :] = jnp.zeros((q_halo, GDN_CONV_W), F32)

    h = _modulate(x_ref[...], sc_ref[...], sh_ref[...]).astype(BF16)
    y = _dot(h, wcf_ref[...])
    cfpad_ref[cf_halo:cf_halo + SEG_ROW, :] = y[:, :CF_W] * _sigmoid(y[:, CF_W:])
    for j in range(3):
        cols = slice(j * GDN_QK, (j + 1) * GDN_QK)
        qpad_ref[q_halo:q_halo + SEG_ROW, cols] = _dot(h, wqkv_ref[:, cols])
    yz = _dot(h, wzbg_ref[...])

    for r0 in range(0, SEG_ROW, CONV_ROWS):
        acc = cfw_ref[0:1, :] * cfpad_ref[r0:r0 + CONV_ROWS, :]
        for k in range(1, CF_CONV):
            acc = acc + cfw_ref[k:k + 1, :] * cfpad_ref[r0 + k * SUBLANES:r0 + k * SUBLANES + CONV_ROWS, :]
        a_ref[r0:r0 + CONV_ROWS, :] = _silu(_layer_norm(acc, lng_ref[...], lnb_ref[...]))

    _store_batch_major(z_ref, stage_ref.at[3], yz[:, :GDN_W], 0, SEG_ROW // BATCH)
    s = yz[:, GDN_W:]
    t = s + dtb_ref[...]
    softplus = jnp.maximum(t, 0.0) + jnp.log1p(jnp.exp(-jnp.abs(t)))
    lane = lax.broadcasted_iota(jnp.int32, s.shape, 1)
    bg_ref[...] = jnp.where(lane < 2 * GDN_HEADS, _sigmoid(s), -jnp.exp(alog_ref[...]) * softplus)

    blk = 2 * CONV_ROWS
    for j in range(3):
        cols = slice(j * GDN_QK, (j + 1) * GDN_QK)
        for r0 in range(0, SEG_ROW, blk):
            yq = (gcw_ref[0:1, cols] * qpad_ref[r0:r0 + blk, cols]
                  + gcw_ref[1:2, cols] * qpad_ref[r0 + q_halo:r0 + q_halo + blk, cols]
                  + gcw_ref[2:3, cols] * qpad_ref[r0 + 2 * q_halo:r0 + 2 * q_halo + blk, cols])
            yq = _silu(yq)
            if j < 2:
                scale = GDN_DK ** -0.5 if j == 0 else 1.0
                heads = []
                for hh in range(GDN_HEADS):
                    th = yq[:, hh * GDN_DK:(hh + 1) * GDN_DK]
                    heads.append(th * (lax.rsqrt(jnp.sum(th * th, axis=-1, keepdims=True) + RMS_EPS) * scale))
                yq = jnp.concatenate(heads, axis=-1)
            _store_batch_major(qkv_ref.at[:, :, cols], stage_ref.at[j], yq, r0 // SUBLANES, blk // SUBLANES, stage_row0=r0)


def _even_in(x, mod, w_cf, w_qkv, w_zbg, cf_w, ln_g, ln_b, gdn_w, alog, dtb):
    rows = x.shape[0]
    n_tok = SEG_ROW // BATCH
    return pl.pallas_call(
        _even_in_kernel,
        grid=(rows // SEG_ROW,),
        in_specs=[pl.BlockSpec((SEG_ROW, D_MODEL), lambda i: (i, 0)), mod.spec(SC1), mod.spec(SH1),
                  _const_spec((D_MODEL, 2 * CF_W)), _const_spec((D_MODEL, GDN_CONV_W)), _const_spec((D_MODEL, GDN_W + LANES)),
                  _const_spec((CF_CONV, CF_W)), _const_spec((1, CF_W)), _const_spec((1, CF_W)),
                  _const_spec((3, GDN_CONV_W)), _const_spec((1, LANES)), _const_spec((1, LANES))],
        out_specs=[pl.BlockSpec((SEG_ROW, CF_W), lambda i: (i, 0)),
                   pl.BlockSpec((BATCH, n_tok, GDN_CONV_W), lambda i: (0, i, 0)),
                   pl.BlockSpec((BATCH, n_tok, GDN_W), lambda i: (0, i, 0)),
                   pl.BlockSpec((SEG_ROW, LANES), lambda i: (i, 0))],
        out_shape=[jax.ShapeDtypeStruct((rows, CF_W), F32),
                   jax.ShapeDtypeStruct((BATCH, rows // BATCH, GDN_CONV_W), F32),
                   jax.ShapeDtypeStruct((BATCH, rows // BATCH, GDN_W), F32),
                   jax.ShapeDtypeStruct((rows, LANES), F32)],
        scratch_shapes=[pltpu.VMEM((SEG_ROW + 2 * (CF_CONV // 2) * SUBLANES, CF_W), F32),
                        pltpu.VMEM((SEG_ROW + 2 * SUBLANES, GDN_CONV_W), F32),
                        pltpu.VMEM((4, GDN_QK // LANES, SEG_ROW, LANES), F32)],
        compiler_params=_params(1), name="even_in",
    )(x, mod.arr, mod.arr, w_cf, w_qkv, w_zbg, cf_w, ln_g, ln_b, gdn_w, alog, dtb)


def _zbg_proj(x, mod, w, alog, dtb, *, tm):
    rows = x.shape[0]
    return pl.pallas_call(
        _zbg_kernel,
        grid=(rows // tm,),
        in_specs=[pl.BlockSpec((tm, D_MODEL), lambda i: (i, 0)), mod.spec(SC1), mod.spec(SH1),
                  _const_spec((D_MODEL, GDN_W + LANES)), _const_spec((1, LANES)), _const_spec((1, LANES))],
        out_specs=[pl.BlockSpec((BATCH, tm // BATCH, GDN_W), lambda i: (0, i, 0)),
                   pl.BlockSpec((tm, LANES), lambda i: (i, 0))],
        out_shape=[jax.ShapeDtypeStruct((BATCH, rows // BATCH, GDN_W), F32), jax.ShapeDtypeStruct((rows, LANES), F32)],
        scratch_shapes=[pltpu.VMEM((GDN_W // LANES, PIECE, LANES), F32)],
        compiler_params=_params(1), name="zbg_proj",
    )(x, mod.arr, mod.arr, w, alog, dtb)


def _gdn_decays(beta_row, g_row, rev):
    c = GDN_CK
    ri = lax.broadcasted_iota(jnp.int32, (c, c), 0)
    ci = lax.broadcasted_iota(jnp.int32, (c, c), 1)
    eye = ri == ci
    incl = (ri <= ci) if rev else (ri >= ci)
    incl_t = (ci <= ri) if rev else (ci >= ri)
    strict = (ri < ci) if rev else (ri > ci)
    g_b = jnp.broadcast_to(g_row, (c, c))
    beta_b = jnp.broadcast_to(beta_row, (c, c))
    g_col = jnp.sum(jnp.where(eye, g_b, 0.0), axis=1, keepdims=True)
    beta_col = jnp.sum(jnp.where(eye, beta_b, 0.0), axis=1, keepdims=True)
    gc_col = jnp.sum(jnp.where(incl, g_b, 0.0), axis=1, keepdims=True)
    gc_row = jnp.sum(jnp.where(incl_t, g_col, 0.0), axis=0, keepdims=True)
    g_sum = jnp.sum(g_row, axis=1, keepdims=True)
    decay = jnp.exp(jnp.where(incl, gc_col - gc_row, -1e30))
    return dict(strict=strict, beta_col=beta_col, decay=decay, e_col=jnp.exp(gc_col),
                e_end=jnp.exp(g_sum - gc_col), g_tot=jnp.exp(g_sum))


def _gdn_chunk_terms(chains, side_tasks=()):
    side_tasks = list(side_tasks)

    def side():
        if side_tasks:
            side_tasks.pop(0)()

    c = GDN_CK
    ri = lax.broadcasted_iota(jnp.int32, (c, c), 0)
    ci = lax.broadcasted_iota(jnp.int32, (c, c), 1)
    same = lambda s: (ri // s) == (ci // s)
    nils = [jnp.where(ch["strict"], -(ch["beta_col"] * ch["kk"] * ch["decay"]), 0.0) for ch in chains]
    nds = [jnp.where(same(GDN_BASE), n, 0.0) for n in nils]
    invs = [jnp.where(ri == ci, 1.0, 0.0) + nd for nd in nds]
    p16s = [nd.astype(BF16) for nd in nds]
    side()
    for _ in range(int(math.log2(GDN_BASE)) - 1):
        p16s = [_dot(p16, p16).astype(BF16) for p16 in p16s]
        side()
        invs = [inv + _dot(p16, inv.astype(BF16)) for p16, inv in zip(p16s, invs)]
    s = GDN_BASE
    while s < c:
        off = jnp.logical_and(same(2 * s), jnp.logical_not(same(s)))
        inv16s = [inv.astype(BF16) for inv in invs]
        side()
        m16s = [_dot(inv16, jnp.where(off, n, 0.0).astype(BF16)).astype(BF16) for inv16, n in zip(inv16s, nils)]
        side()
        invs = [inv + _dot(m16, inv16) for inv, m16, inv16 in zip(invs, m16s, inv16s)]
        s *= 2
    side()
    y16s = [_dot(inv.astype(BF16),
                 jnp.concatenate([ch["v"] * ch["beta_col"], ch["k"] * (ch["beta_col"] * ch["e_col"])], axis=-1).astype(BF16)
                 ).astype(BF16) for inv, ch in zip(invs, chains)]
    side()
    kts = [_dot_tn((ch["k"] * ch["e_end"]).astype(BF16), y16) for ch, y16 in zip(chains, y16s)]
    qys = [_dot((ch["qk_raw"] * ch["decay"]).astype(BF16), y16) for ch, y16 in zip(chains, y16s)]
    while side_tasks:
        side()
    out = []
    for ch, kt, qy in zip(chains, kts, qys):
        q_eff = ch["q"] * ch["e_col"] - qy[:, GDN_DV:]
        out.append(((-kt[:, GDN_DV:]).astype(BF16), kt[:, :GDN_DV], q_eff.astype(BF16), qy[:, :GDN_DV],
                    jnp.broadcast_to(ch["g_tot"], (1, GDN_DV))))
    return out


def _gdn_kernel(qc, kc, vc, zc, ql, kl, vl, zl, bg, ng_ref, oc_ref, ol_ref,
                q_all, k_all, v_all, o_dir, a_ref, b_ref, qe_ref, oz_ref, gt_ref, s_ref):
    ncc = CTX_LEN // GDN_CK
    for src, dst in ((qc, q_all), (kc, k_all), (vc, v_all)):
        dst[0:CTX_LEN, :] = src[...]
    for src, dst in ((ql, q_all), (kl, k_all), (vl, v_all)):
        for r in range(SEQ // PIECE):
            dst[CTX_LEN + r * PIECE:CTX_LEN + (r + 1) * PIECE, :] = src[r * PIECE:(r + 1) * PIECE, :]

    def chunk_at(d, p):
        if d == 0:
            return p
        return jnp.where(p < ncc, ncc - 1 - p, GDN_NCHUNK + ncc - 1 - p)

    def rows_of(cc):
        if isinstance(cc, int):
            return pl.ds(cc * GDN_CK, GDN_CK)
        return pl.ds(pl.multiple_of(cc * GDN_CK, GDN_CK), GDN_CK)

    def prepare(g, side_tasks=()):
        chains = []
        for jj in range(GDN_GROUP):
            for d in range(2):
                cc = chunk_at(d, g * GDN_GROUP + jj)
                rows = rows_of(cc)
                qv, kv, vv = q_all[rows, :], k_all[rows, :], v_all[rows, :]
                k16 = kv.astype(BF16)
                kq = _dot_nt(jnp.concatenate([k16, qv.astype(BF16)], axis=0), k16)
                chains.append(dict(q=qv, k=kv, v=vv, kk=kq[:GDN_CK], qk_raw=kq[GDN_CK:],
                                   **_gdn_decays(bg[d, pl.ds(cc, 1), :], bg[2 + d, pl.ds(cc, 1), :], rev=(d == 1))))
        terms = _gdn_chunk_terms(chains, side_tasks)
        for n, (a_neg, b_mat, q_eff, o_zero, g_tot) in enumerate(terms):
            jj, d = divmod(n, 2)
            a_ref[d, jj] = a_neg
            b_ref[d, jj] = b_mat
            qe_ref[d, jj] = q_eff
            oz_ref[d, jj] = o_zero
            gt_ref[d, jj] = g_tot

    def advance_tasks(g):
        def step(jj):
            for d in range(2):
                rows = rows_of(chunk_at(d, g * GDN_GROUP + jj))
                s = s_ref[d]
                s16 = s.astype(BF16)
                o_dir[d, rows, :] = oz_ref[d, jj] + _dot(qe_ref[d, jj], s16)
                s_ref[d] = s * gt_ref[d, jj] + _dot(a_ref[d, jj], s16) + b_ref[d, jj]
        return [functools.partial(step, jj) for jj in range(GDN_GROUP)]

    s_ref[...] = jnp.zeros(s_ref.shape, F32)
    n_groups = GDN_NCHUNK // GDN_GROUP
    prepare(0)

    def body(g, carry):
        prepare(g, advance_tasks(g - 1))
        return carry
    lax.fori_loop(1, n_groups, body, 0)
    for task in advance_tasks(n_groups - 1):
        task()

    blk = 256
    for z, o, tok0 in ((zc, oc_ref, 0), (zl, ol_ref, CTX_LEN)):
        for r in range(z.shape[0] // blk):
            rows = slice(r * blk, (r + 1) * blk)
            src = slice(tok0 + r * blk, tok0 + (r + 1) * blk)
            ov = o_dir[0, src, :] + o_dir[1, src, :]
            ov = ov * lax.rsqrt(jnp.mean(ov * ov, axis=-1, keepdims=True) + RMS_EPS) * ng_ref[...]
            o[rows, :] = ov * _silu(z[rows, :])


def _gdn(qkv_c, z_c, qkv_l, z_l, bg, norm_g):
    def tok_spec(n_tok, col0):
        return pl.BlockSpec((None, n_tok, GDN_DK), lambda b, h: (b, 0, col0 + h))

    in_specs = []
    for n_tok in (CTX_LEN, SEQ):
        in_specs += [tok_spec(n_tok, 0), tok_spec(n_tok, GDN_HEADS), tok_spec(n_tok, 2 * GDN_HEADS), tok_spec(n_tok, 0)]
    in_specs += [pl.BlockSpec((None, None, 4, GDN_NCHUNK, GDN_CK), lambda b, h: (b, h, 0, 0, 0)),
                 _const_spec((1, GDN_DV))]
    n_tok = CTX_LEN + SEQ
    return pl.pallas_call(
        _gdn_kernel,
        grid=(BATCH, GDN_HEADS),
        in_specs=in_specs,
        out_specs=[tok_spec(CTX_LEN, 0), tok_spec(SEQ, 0)],
        out_shape=[jax.ShapeDtypeStruct((BATCH, CTX_LEN, GDN_W), F32),
                   jax.ShapeDtypeStruct((BATCH, SEQ, GDN_W), F32)],
        scratch_shapes=[pltpu.VMEM((n_tok, GDN_DK), F32), pltpu.VMEM((n_tok, GDN_DK), F32),
                        pltpu.VMEM((n_tok, GDN_DV), F32), pltpu.VMEM((2, n_tok, GDN_DV), F32),
                        pltpu.VMEM((2, GDN_GROUP, GDN_DK, GDN_DV), BF16),
                        pltpu.VMEM((2, GDN_GROUP, GDN_DK, GDN_DV), F32),
                        pltpu.VMEM((2, GDN_GROUP, GDN_CK, GDN_DK), BF16),
                        pltpu.VMEM((2, GDN_GROUP, GDN_CK, GDN_DV), F32),
                        pltpu.VMEM((2, GDN_GROUP, 1, GDN_DV), F32),
                        pltpu.VMEM((2, GDN_DK, GDN_DV), F32)],
        compiler_params=_params(2), name="gdn",
    )(qkv_c, qkv_c, qkv_c, z_c, qkv_l, qkv_l, qkv_l, z_l, bg, norm_g)


def _residual_norm(x, branch, gate, g, b):
    return _layer_norm(DN_ALPHA * x + _per_sample(branch, gate, lambda a, c: a * c), g, b)


def _out_even_kernel(x_ref, a_ref, o_ref, g1_ref, wa_ref, wo_ref, lng_ref, lnb_ref, out_ref, stage_ref):
    tm = x_ref.shape[0]
    for p in range(tm // PIECE):
        rows = slice(p * PIECE, (p + 1) * PIECE)
        o = _load_token_major(o_ref, stage_ref, p * (PIECE // BATCH), PIECE // BATCH)
        m = _dot(a_ref[rows, :].astype(BF16), wa_ref[...]) + _dot(o.astype(BF16), wo_ref[...])
        out_ref[rows, :] = _residual_norm(x_ref[rows, :], m, g1_ref[...], lng_ref[...], lnb_ref[...])


def _out_even(x, a, o, mod, wa, wo, ln_g, ln_b, *, tm):
    rows = x.shape[0]
    row = lambda c: pl.BlockSpec((tm, c), lambda i: (i, 0))
    return pl.pallas_call(
        _out_even_kernel,
        grid=(rows // tm,),
        in_specs=[row(D_MODEL), row(CF_W), pl.BlockSpec((BATCH, tm // BATCH, GDN_W), lambda i: (0, i, 0)),
                  mod.spec(G1),
                  _const_spec((CF_W, D_MODEL)), _const_spec((GDN_W, D_MODEL)),
                  _const_spec((1, D_MODEL)), _const_spec((1, D_MODEL))],
        out_specs=row(D_MODEL),
        out_shape=jax.ShapeDtypeStruct((rows, D_MODEL), F32),
        scratch_shapes=[pltpu.VMEM((GDN_W // LANES, PIECE, LANES), F32)],
        compiler_params=_params(1), name="out_even",
    )(x, a, o, mod.arr, wa, wo, ln_g, ln_b)


def _gelu_tanh(x):
    return 0.5 * x * (1.0 + jnp.tanh(math.sqrt(2.0 / math.pi) * (x + 0.044715 * (x * x * x))))


def _out_odd_kernel(x_ref, u_ref, yf_ref, yb_ref, s_ref, g1_ref, d_ref, gw_ref, gb_ref, wa_ref, wo_ref,
                    lng_ref, lnb_ref, out_ref):
    tm = x_ref.shape[0]
    for p in range(tm // PIECE):
        rows = slice(p * PIECE, (p + 1) * PIECE)
        y = d_ref[...] * u_ref[rows, :] + yf_ref[rows, :] + yb_ref[rows, :]
        zg = _gelu_tanh(y)
        s5 = zg * _sigmoid(_dot(zg.astype(BF16), gw_ref[...]) + gb_ref[...])
        m = _dot(s5.astype(BF16), wa_ref[...]) + _dot(s_ref[rows, :].astype(BF16), wo_ref[...])
        out_ref[rows, :] = _residual_norm(x_ref[rows, :], m, g1_ref[...], lng_ref[...], lnb_ref[...])


def _out_odd(x, u, yf, yb, s, mod, d_skip, glu_w, glu_b, wa, wo, ln_g, ln_b, *, tm):
    rows = x.shape[0]
    row = lambda c: pl.BlockSpec((tm, c), lambda i: (i, 0))
    return pl.pallas_call(
        _out_odd_kernel,
        grid=(rows // tm,),
        in_specs=[row(D_MODEL), row(S5_W), row(S5_W), row(S5_W), row(SC_W), mod.spec(G1),
                  _const_spec((1, S5_W)), _const_spec((S5_W, S5_W)), _const_spec((1, S5_W)),
                  _const_spec((S5_W, D_MODEL)), _const_spec((SC_W, D_MODEL)),
                  _const_spec((1, D_MODEL)), _const_spec((1, D_MODEL))],
        out_specs=row(D_MODEL),
        out_shape=jax.ShapeDtypeStruct((rows, D_MODEL), F32),
        compiler_params=_params(1), name="out_odd",
    )(x, u, yf, yb, s, mod.arr, d_skip, glu_w, glu_b, wa, wo, ln_g, ln_b)


def _conv3(pad_ref, s, r0, n, cw_ref, shift):
    return (cw_ref[0:1, :] * pad_ref[s, r0:r0 + n, :]
            + cw_ref[1:2, :] * pad_ref[s, r0 + shift:r0 + shift + n, :]
            + cw_ref[2:3, :] * pad_ref[s, r0 + 2 * shift:r0 + 2 * shift + n, :])


def _ffn_kernel(x_ref, sc_ref, sh_ref, g2_ref, wu_ref, cw_ref, wd_ref, lng_ref, lnb_ref, out_ref,
                h_ref, val_ref, pad_ref, *, tm, shift, seg):
    j = pl.program_id(1)
    n_pieces = tm // PIECE
    tf = val_ref.shape[1]

    @pl.when(j == 0)
    def _():
        zeros = jnp.zeros((shift, tf), F32)
        for s in range(tm // seg):
            pad_ref[s, 0:shift, :] = zeros
            pad_ref[s, shift + seg:shift + seg + shift, :] = zeros
        for p in range(n_pieces):
            x = _load_rows(x_ref, p * PIECE, PIECE)
            h_ref[p * PIECE:(p + 1) * PIECE, :] = _modulate(x, sc_ref[...], sh_ref[...]).astype(BF16)

    for p in range(n_pieces):
        h = h_ref[p * PIECE:(p + 1) * PIECE, :]
        up = _dot(h, wu_ref[...])
        val_ref[p * PIECE:(p + 1) * PIECE, :] = up[:, :tf]
        s, off = divmod(p * PIECE, seg)
        pad_ref[s, shift + off:shift + off + PIECE, :] = up[:, tf:]

    for p in range(n_pieces):
        s, off = divmod(p * PIECE, seg)
        gate = _conv3(pad_ref, s, off, PIECE, cw_ref, shift)
        act = (val_ref[p * PIECE:(p + 1) * PIECE, :] * _silu(gate)).astype(BF16)
        part = _dot(act, wd_ref[...])

        @pl.when(j == 0)
        def _():
            _store_rows(out_ref, p * PIECE, part)

        @pl.when(j > 0)
        def _():
            _store_rows(out_ref, p * PIECE, _load_rows(out_ref, p * PIECE, PIECE) + part)

    @pl.when(j == pl.num_programs(1) - 1)
    def _():
        for p in range(n_pieces):
            x = _load_rows(x_ref, p * PIECE, PIECE)
            y = _residual_norm(x, _load_rows(out_ref, p * PIECE, PIECE), g2_ref[...], lng_ref[...], lnb_ref[...])
            _store_rows(out_ref, p * PIECE, y)


def _ffn(x, mod, w_up, cw, w_down, ln_g, ln_b, *, tiling, shift, seg, tf):
    tm = tiling.tm
    n_f = FFN_F // tf
    out = pl.pallas_call(
        functools.partial(_ffn_kernel, tm=tm, shift=shift, seg=seg),
        grid=(tiling.n_tiles, n_f),
        in_specs=[tiling.spec(D_MODEL), mod.spec(SC2), mod.spec(SH2), mod.spec(G2),
                  pl.BlockSpec((D_MODEL, 2 * tf), lambda i, j: (0, j)),
                  pl.BlockSpec((3, tf), lambda i, j: (0, j)),
                  pl.BlockSpec((tf, D_MODEL), lambda i, j: (j, 0)),
                  _const_spec((1, D_MODEL)), _const_spec((1, D_MODEL))],
        out_specs=tiling.spec(D_MODEL),
        out_shape=jax.ShapeDtypeStruct(tiling.shape(D_MODEL), F32),
        scratch_shapes=[pltpu.VMEM((tm, D_MODEL), BF16), pltpu.VMEM((tm, tf), F32),
                        pltpu.VMEM((tm // seg, seg + 2 * shift, tf), F32)],
        compiler_params=_params(2), name="conv_ffn",
    )(tiling.view(x), mod.arr, mod.arr, mod.arr, w_up, cw, w_down, ln_g, ln_b)
    return tiling.unview(out)


def _odd_in_kernel(x_ref, sc_ref, sh_ref, wu_ref, wb_ref, wc_ref, wx_ref, cw_ref, u_ref, s_ref,
                   h_ref, b_ref, pad_ref, *, tm, shift, seg):
    j = pl.program_id(1)
    n_pieces = tm // PIECE
    cb = u_ref.shape[-1]

    @pl.when(j == 0)
    def _():
        zeros = jnp.zeros((shift, cb), F32)
        for s in range(tm // seg):
            pad_ref[s, 0:shift, :] = zeros
            pad_ref[s, shift + seg:shift + seg + shift, :] = zeros
        for p in range(n_pieces):
            x = _load_rows(x_ref, p * PIECE, PIECE)
            h_ref[p * PIECE:(p + 1) * PIECE, :] = _modulate(x, sc_ref[...], sh_ref[...]).astype(BF16)

    for p in range(n_pieces):
        h = h_ref[p * PIECE:(p + 1) * PIECE, :]
        _store_rows(u_ref, p * PIECE, _dot(h, wu_ref[...]))
        b_ref[p * PIECE:(p + 1) * PIECE, :] = _dot(h, wb_ref[...])
        s, off = divmod(p * PIECE, seg)
        pad_ref[s, shift + off:shift + off + PIECE, :] = _dot(h, wc_ref[...]) * _dot(h, wx_ref[...])

    for p in range(n_pieces):
        s, off = divmod(p * PIECE, seg)
        _store_rows(s_ref, p * PIECE, b_ref[p * PIECE:(p + 1) * PIECE, :] * _conv3(pad_ref, s, off, PIECE, cw_ref, shift))


def _odd_in(x, mod, w_in, cw, *, tiling, shift, seg):
    tm = tiling.tm
    cb = 2 * LANES
    nb = S5_W // cb
    wspec = lambda k: pl.BlockSpec((D_MODEL, cb), lambda i, j: (0, k * nb + j))
    u, s = pl.pallas_call(
        functools.partial(_odd_in_kernel, tm=tm, shift=shift, seg=seg),
        grid=(tiling.n_tiles, nb),
        in_specs=[tiling.spec(D_MODEL), mod.spec(SC1), mod.spec(SH1),
                  wspec(0), wspec(1), wspec(2), wspec(3),
                  pl.BlockSpec((3, cb), lambda i, j: (0, j))],
        out_specs=[tiling.spec(cb, lambda i, j: j), tiling.spec(cb, lambda i, j: j)],
        out_shape=[jax.ShapeDtypeStruct(tiling.shape(S5_W), F32), jax.ShapeDtypeStruct(tiling.shape(SC_W), F32)],
        scratch_shapes=[pltpu.VMEM((tm, D_MODEL), BF16), pltpu.VMEM((tm, cb), F32),
                        pltpu.VMEM((tm // seg, seg + 2 * shift, cb), F32)],
        compiler_params=_params(2), name="odd_in",
    )(tiling.view(x), mod.arr, mod.arr, w_in, w_in, w_in, w_in, cw)
    return tiling.unview(u), tiling.unview(s)


def _s5_kernel(uf_ref, ub_ref, wb_ref, wc_ref, a_ref, x0_ref, yf_ref, yb_ref, xfin_ref, bu_ref, st_ref):
    t = pl.program_id(0)
    rows = uf_ref.shape[0]
    n_tok = rows // SUBLANES
    hs = S5_HSTATE

    @pl.when(t == 0)
    def _():
        st_ref[...] = x0_ref[...]

    for d, u_ref in enumerate((uf_ref, ub_ref)):
        for hf in range(2):
            u = u_ref[:, hf * S5_HALF:(hf + 1) * S5_HALF].astype(BF16)
            bu_ref[d, :, hf * 2 * hs:(hf + 1) * 2 * hs] = _dot(u, wb_ref[d, hf])

    for d, y_ref in enumerate((yf_ref, yb_ref)):
        for hf in range(2):
            c_re = hf * 2 * hs
            c_im = c_re + hs
            a_re = jnp.broadcast_to(a_ref[d, hf, 0:1, :], (SUBLANES, hs))
            a_im = jnp.broadcast_to(a_ref[d, hf, 1:2, :], (SUBLANES, hs))
            xr, xi = st_ref[d, :, c_re:c_re + hs], st_ref[d, :, c_im:c_im + hs]
            for i in range(n_tok):
                tok = i if d == 0 else n_tok - 1 - i
                r = slice(tok * SUBLANES, (tok + 1) * SUBLANES)
                xr, xi = (a_re * xr - a_im * xi + bu_ref[d, r, c_re:c_re + hs],
                          a_re * xi + a_im * xr + bu_ref[d, r, c_im:c_im + hs])
                bu_ref[d, r, c_re:c_re + hs] = xr
                bu_ref[d, r, c_im:c_im + hs] = xi
            st_ref[d, :, c_re:c_re + hs] = xr
            st_ref[d, :, c_im:c_im + hs] = xi
        for hf in range(2):
            xs = bu_ref[d, :, hf * 2 * hs:(hf + 1) * 2 * hs].astype(BF16)
            y_ref[:, hf * S5_HALF:(hf + 1) * S5_HALF] = _dot(xs, wc_ref[d, hf])

    @pl.when(t == pl.num_programs(0) - 1)
    def _():
        xfin_ref[...] = st_ref[...]


def _s5(u, wb, wc, a, x0):
    rows = u.shape[0]
    tr = S5_TOK * SUBLANES
    nt = rows // tr
    state = (2, SUBLANES, 4 * S5_HSTATE)
    return pl.pallas_call(
        _s5_kernel,
        grid=(nt,),
        in_specs=[pl.BlockSpec((tr, S5_W), lambda t: (t, 0)),
                  pl.BlockSpec((tr, S5_W), lambda t: (nt - 1 - t, 0)),
                  _const_spec((2, 2, S5_HALF, 2 * S5_HSTATE)), _const_spec((2, 2, 2 * S5_HSTATE, S5_HALF)),
                  _const_spec((2, 2, 2, S5_HSTATE)), _const_spec(state)],
        out_specs=[pl.BlockSpec((tr, S5_W), lambda t: (t, 0)),
                   pl.BlockSpec((tr, S5_W), lambda t: (nt - 1 - t, 0)),
                   _const_spec(state)],
        out_shape=[jax.ShapeDtypeStruct((rows, S5_W), F32), jax.ShapeDtypeStruct((rows, S5_W), F32),
                   jax.ShapeDtypeStruct(state, F32)],
        scratch_shapes=[pltpu.VMEM((2, tr, 4 * S5_HSTATE), F32), pltpu.VMEM(state, F32)],
        compiler_params=_params(1), name="s5_scan",
    )(u, u, wb, wc, a, x0)


def _s5_weights(lam_re, lam_im, log_dt, b_re, b_im, c_re, c_im):
    eye = jnp.eye(S5_G // 2, dtype=F32)
    wbs, wcs, avs = [], [], []
    for di in range(2):
        lr, li = lam_re[di], lam_im[di]
        dt = jnp.exp(log_dt[di])[:, None]
        mag = jnp.exp(lr * dt)
        ar, ai = mag * jnp.cos(li * dt), mag * jnp.sin(li * dt)
        den = lr * lr + li * li
        fr = ((ar - 1.0) * lr + ai * li) / den
        fi = (ai * lr - (ar - 1.0) * li) / den
        bbr = fr[..., None] * b_re - fi[..., None] * b_im
        bbi = fr[..., None] * b_im + fi[..., None] * b_re

        def in_block(t):
            t = t.reshape(2, S5_G // 2, S5_N, S5_P)
            return jnp.einsum("hgnp,gk->hgpkn", t, eye).reshape(2, S5_HALF, S5_HSTATE)

        def out_block(t):
            t = t.reshape(2, S5_G // 2, S5_P, S5_N)
            return jnp.einsum("hgpn,gk->hgnkp", t, eye).reshape(2, S5_HSTATE, S5_HALF)

        wbs.append(jnp.concatenate([in_block(bbr), in_block(bbi)], axis=-1))
        wcs.append(jnp.concatenate([out_block(c_re), out_block(-c_im)], axis=1))
        avs.append(jnp.stack([ar.reshape(2, S5_HSTATE), ai.reshape(2, S5_HSTATE)], axis=1))
    return jnp.stack(wbs).astype(BF16), jnp.stack(wcs).astype(BF16), jnp.stack(avs)


def _bg_rows(bg, n_tok):
    t = bg.reshape(n_tok, BATCH, LANES)[:, :, :4 * GDN_HEADS].reshape(n_tok, BATCH, 4, GDN_HEADS)
    return t.transpose(1, 3, 2, 0).reshape(BATCH, GDN_HEADS, 4, n_tok // GDN_CK, GDN_CK)


def _to_batch_major(a, n_tok):
    return a.reshape(n_tok, BATCH, a.shape[-1]).transpose(1, 0, 2)


def _to_token_major(a):
    return a.transpose(1, 0, 2).reshape(a.shape[0] * a.shape[1], a.shape[2])


def kernel(x, c, ctx, c_ctx, ada_w, ada_b, ln_g, ln_b, ev_w_in, ev_w_out, cf_conv, cf_ln_g, cf_ln_b, gdn_conv,
           gdn_a_log, gdn_dt_bias, gdn_norm_g, od_w_in, od_w_out, s5_lam_re, s5_lam_im, s5_log_dt, s5_b_re,
           s5_b_im, s5_c_re, s5_c_im, s5_d, s5_glu_w, s5_glu_b, sc_conv, ffn_w_up, ffn_conv, ffn_w_down):
    xl = _to_token_major(x)
    cl = _to_token_major(ctx)

    cvec = jnp.concatenate([c, jnp.broadcast_to(c_ctx[None, :], (SUBLANES, D_MODEL))], axis=0)
    mods = _ada(cvec, ada_w, ada_b).reshape(DEPTH, 2, SUBLANES, 6 * D_MODEL)

    lat_row = _Tiling("row", ROWS_LAT, 2 * SEG_ROW)
    lat_col = _Tiling("col", ROWS_LAT, 2 * SEG_ROW)
    ctx_row = _Tiling("row", ROWS_CTX, ROWS_CTX)
    lat_along_row = dict(tiling=lat_row, shift=SUBLANES, seg=SEG_ROW)
    lat_along_col = dict(tiling=lat_col, shift=lat_col.rg, seg=lat_col.tm)
    ctx_conv = dict(tiling=ctx_row, shift=SUBLANES, seg=ROWS_CTX)

    row2 = lambda v: v.reshape(1, -1)
    for i in range(DEPTH):
        j = i // 2
        last = i == DEPTH - 1
        m_lat, m_ctx = _Mod(mods, i, 0), _Mod(mods, i, 1)
        lng1, lnb1, lng2, lnb2 = row2(ln_g[i, 0]), row2(ln_b[i, 0]), row2(ln_g[i, 1]), row2(ln_b[i, 1])
        w_down, f_cw = ffn_w_down[i].astype(BF16), ffn_conv[i]

        def up_blocks(tf, w_up=ffn_w_up[i]):
            cols = [w_up[:, k * FFN_F + j * tf:k * FFN_F + (j + 1) * tf] for j in range(FFN_F // tf) for k in range(2)]
            return jnp.concatenate(cols, axis=1).astype(BF16)

        if i % 2 == 0:
            w_in = ev_w_in[j]
            w_cf = w_in[:, :2 * CF_W].astype(BF16)
            w_qkv = w_in[:, 2 * CF_W:2 * CF_W + GDN_CONV_W].astype(BF16)
            w_zbg = jnp.pad(w_in[:, 2 * CF_W + GDN_CONV_W:], ((0, 0), (0, LANES - 4 * GDN_HEADS))).astype(BF16)
            pad16 = lambda v: jnp.pad(v.reshape(1, -1), ((0, 0), (2 * GDN_HEADS, LANES - 4 * GDN_HEADS)))
            alog, dtb = pad16(gdn_a_log[j]), pad16(gdn_dt_bias[j])
            w_out = ev_w_out[j].astype(BF16)
            cf_ln = (row2(cf_ln_g[j]), row2(cf_ln_b[j]))
            a_c = _cf_mixer(cl, m_ctx, w_cf, cf_conv[j], *cf_ln, tm=ROWS_CTX, seg=ROWS_CTX)
            qkv_c = _qkv_proj(cl, m_ctx, w_qkv, gdn_conv[j], tm=ROWS_CTX, seg=ROWS_CTX)
            z_c, bg_c = _zbg_proj(cl, m_ctx, w_zbg, alog, dtb, tm=ROWS_CTX)
            a_l, qkv_l, z_l, bg_l = _even_in(xl, m_lat, w_cf, w_qkv, w_zbg, cf_conv[j], *cf_ln, gdn_conv[j], alog, dtb)
            bg = jnp.concatenate([_bg_rows(bg_c, CTX_LEN), _bg_rows(bg_l, SEQ)], axis=3)
            o_c, o_l = _gdn(qkv_c, z_c, qkv_l, z_l, bg, row2(gdn_norm_g[j]))
            x1 = _out_even(xl, a_l, o_l, m_lat, w_out[:CF_W], w_out[CF_W:], lng1, lnb1, tm=2 * SEG_ROW)
            if not last:
                c1 = _out_even(cl, a_c, o_c, m_ctx, w_out[:CF_W], w_out[CF_W:], lng1, lnb1, tm=2 * SEG_ROW)
            lat_ffn = lat_along_col
        else:
            w_in = od_w_in[j].astype(BF16)
            w_out = od_w_out[j].astype(BF16)
            wb, wc, av = _s5_weights(s5_lam_re[j], s5_lam_im[j], s5_log_dt[j], s5_b_re[j], s5_b_im[j],
                                     s5_c_re[j], s5_c_im[j])
            u_c, s_c = _odd_in(cl, m_ctx, w_in, sc_conv[j], **ctx_conv)
            u_l, s_l = _odd_in(xl, m_lat, w_in, sc_conv[j], **lat_along_col)
            zero_state = jnp.zeros((2, SUBLANES, 4 * S5_HSTATE), F32)
            yf_c, yb_c, fin_c = _s5(u_c, wb, wc, av, zero_state)
            yf_l, yb_l, _ = _s5(u_l, wb, wc, av, fin_c)
            odd_w = (row2(s5_d[j]), s5_glu_w[j].astype(BF16), row2(s5_glu_b[j]), w_out[:S5_W], w_out[S5_W:])
            x1 = _out_odd(xl, u_l, yf_l, yb_l, s_l, m_lat, *odd_w, lng1, lnb1, tm=2 * SEG_ROW)
            if not last:
                c1 = _out_odd(cl, u_c, yf_c, yb_c, s_c, m_ctx, *odd_w, lng1, lnb1, tm=2 * SEG_ROW)
            lat_ffn = lat_along_row

        xl = _ffn(x1, m_lat, up_blocks(FFN_TF_LAT), f_cw, w_down, lng2, lnb2,
                  tf=FFN_TF_LAT, **lat_ffn)
        if not last:
            cl = _ffn(c1, m_ctx, up_blocks(FFN_TF_CTX), f_cw, w_down, lng2, lnb2,
                      tf=FFN_TF_CTX, **ctx_conv)

    return xl.reshape(SEQ, BATCH, D_MODEL).transpose(1, 0, 2)
```

```python
import functools
import math

import jax
import jax.numpy as jnp
from jax import lax
from jax.experimental import pallas as pl
from jax.experimental.pallas import tpu as pltpu

D_MODEL = 1024
BATCH = 8
SEQ = 2048
DEPTH = 4
GRID_W = 64
GRID_H = SEQ // GRID_W
CTX_LEN = 256
CF_W = 512
CF_CONV = 31
GDN_HEADS = 4
GDN_DK = 128
GDN_DV = 128
GDN_QK = GDN_HEADS * GDN_DK
GDN_W = GDN_HEADS * GDN_DV
GDN_CONV_W = 2 * GDN_QK + GDN_W
S5_W = 512
S5_P = 16
S5_G = S5_W // S5_P
S5_N = 64
SC_W = 512
FFN_F = 2816
DN_ALPHA = (2 * DEPTH) ** 0.25
LN_EPS = 1e-5
RMS_EPS = 1e-6

SUBLANES = 8
LANES = 128
VMEM_LIMIT_BYTES = 56 * 1024 * 1024

ROWS_LAT = SEQ * BATCH
ROWS_CTX = CTX_LEN * BATCH
SEG_ROW = GRID_W * BATCH
PIECE = 512
CONV_ROWS = 64
FFN_TF_LAT = FFN_F // 2
FFN_TF_CTX = 256
S5_HALF = S5_W // 2
S5_HSTATE = (S5_G // 2) * S5_N
S5_TOK = 64
GDN_CK = 64
GDN_BASE = 16
GDN_NCHUNK = (CTX_LEN + SEQ) // GDN_CK
GDN_GROUP = 9

F32 = jnp.float32
BF16 = jnp.bfloat16

assert BATCH == SUBLANES


def _dot(a, b):
    return jnp.dot(a, b, preferred_element_type=F32)


def _dot_nt(a, b):
    return lax.dot_general(a, b, (((1,), (1,)), ((), ())), preferred_element_type=F32)


def _dot_tn(a, b):
    return lax.dot_general(a, b, (((0,), (0,)), ((), ())), preferred_element_type=F32)


def _sigmoid(x):
    return 1.0 / (1.0 + jnp.exp(-x))


def _silu(x):
    return x * _sigmoid(x)


def _per_sample(x, vec, op):
    r, c = x.shape
    x3 = x.reshape(r // SUBLANES, SUBLANES, c)
    return op(x3, vec[None]).reshape(r, c)


def _modulate(x, scale, shift):
    y = _per_sample(x, 1.0 + scale, lambda a, b: a * b)
    return _per_sample(y, shift, lambda a, b: a + b)


def _layer_norm(x, g, b):
    mu = jnp.mean(x, axis=-1, keepdims=True)
    xc = x - mu
    var = jnp.mean(xc * xc, axis=-1, keepdims=True)
    return xc * lax.rsqrt(var + LN_EPS) * g + b


def _load_rows(ref, start, n):
    if len(ref.shape) == 2:
        return ref[start:start + n, :]
    rg = ref.shape[1]
    return ref[start // rg:(start + n) // rg, :, :].reshape(n, ref.shape[2])


def _store_rows(ref, start, value):
    n = value.shape[0]
    if len(ref.shape) == 2:
        ref[start:start + n, :] = value
    else:
        rg = ref.shape[1]
        ref[start // rg:(start + n) // rg, :, :] = value.reshape(n // rg, rg, value.shape[1])


def _params(n_axes):
    return pltpu.CompilerParams(dimension_semantics=("arbitrary",) * n_axes,
                                vmem_limit_bytes=VMEM_LIMIT_BYTES)


class _Tiling:
    def __init__(self, kind, rows, tm):
        self.kind, self.rows, self.tm = kind, rows, tm
        self.n_tiles = rows // tm
        if kind == "col":
            self.rg = tm // GRID_H

    def view(self, a):
        if self.kind == "row":
            return a
        return a.reshape(GRID_H, SEG_ROW, a.shape[-1])

    def unview(self, a):
        return a.reshape(self.rows, a.shape[-1])

    def shape(self, c):
        return (self.rows, c) if self.kind == "row" else (GRID_H, SEG_ROW, c)

    def spec(self, c, col=None):
        col = col or (lambda *ij: 0)
        if self.kind == "row":
            return pl.BlockSpec((self.tm, c), lambda *ij: (ij[0], col(*ij)))
        return pl.BlockSpec((GRID_H, self.rg, c), lambda *ij: (0, ij[0], col(*ij)))


def _const_spec(shape):
    nd = len(shape)
    return pl.BlockSpec(shape, lambda *ij: (0,) * nd)


SH1, SC1, G1, SH2, SC2, G2 = range(6)


class _Mod:
    def __init__(self, arr, layer, stream):
        self.arr, self.layer, self.stream = arr, layer, stream

    def spec(self, col):
        layer, stream = self.layer, self.stream
        return pl.BlockSpec((None, None, SUBLANES, D_MODEL), lambda *ij: (layer, stream, 0, col))


def _ada_kernel(c_ref, w_ref, b_ref, o_ref):
    s = _silu(c_ref[...]).astype(BF16)
    o_ref[...] = _dot(s, w_ref[...].astype(BF16)) + b_ref[...]


def _ada(cvec, ada_w, ada_b):
    return pl.pallas_call(
        _ada_kernel,
        grid=(DEPTH, 6),
        in_specs=[_const_spec((2 * SUBLANES, D_MODEL)),
                  pl.BlockSpec((None, D_MODEL, D_MODEL), lambda i, j: (i, 0, j)),
                  pl.BlockSpec((None, 1, D_MODEL), lambda i, j: (i, 0, j))],
        out_specs=pl.BlockSpec((None, 2 * SUBLANES, D_MODEL), lambda i, j: (i, 0, j)),
        out_shape=jax.ShapeDtypeStruct((DEPTH, 2 * SUBLANES, 6 * D_MODEL), F32),
        compiler_params=_params(2), name="ada",
    )(cvec, ada_w, ada_b.reshape(DEPTH, 1, 6 * D_MODEL))


def _cf_kernel(x_ref, sc_ref, sh_ref, w_ref, cw_ref, g_ref, b_ref, o_ref, pad_ref, *, seg):
    tm = x_ref.shape[0]
    halo = (CF_CONV // 2) * SUBLANES
    zeros = jnp.zeros((halo, CF_W), F32)
    for s in range(tm // seg):
        pad_ref[s, 0:halo, :] = zeros
        pad_ref[s, halo + seg:halo + seg + halo, :] = zeros
    def project(p):
        h = _modulate(x_ref[p * PIECE:(p + 1) * PIECE, :], sc_ref[...], sh_ref[...]).astype(BF16)
        y = _dot(h, w_ref[...])
        glu = y[:, :CF_W] * _sigmoid(y[:, CF_W:])
        s, off = divmod(p * PIECE, seg)
        pad_ref[s, halo + off:halo + off + PIECE, :] = glu

    for p in range(tm // PIECE):
        project(p)
    for s in range(tm // seg):
        def body(i, carry, s=s):
            for r0 in (pl.multiple_of(i * 2 * CONV_ROWS, CONV_ROWS), pl.multiple_of((i * 2 + 1) * CONV_ROWS, CONV_ROWS)):
                acc = cw_ref[0:1, :] * pad_ref[s, pl.ds(r0, CONV_ROWS), :]
                for k in range(1, CF_CONV):
                    acc = acc + cw_ref[k:k + 1, :] * pad_ref[s, pl.ds(r0 + k * SUBLANES, CONV_ROWS), :]
                o_ref[pl.ds(s * seg + r0, CONV_ROWS), :] = _silu(_layer_norm(acc, g_ref[...], b_ref[...]))
            return carry
        lax.fori_loop(0, seg // (2 * CONV_ROWS), body, 0)


def _cf_mixer(x, mod, w, cw, ln_g, ln_b, *, tm, seg):
    rows = x.shape[0]
    halo = (CF_CONV // 2) * SUBLANES
    return pl.pallas_call(
        functools.partial(_cf_kernel, seg=seg),
        grid=(rows // tm,),
        in_specs=[pl.BlockSpec((tm, D_MODEL), lambda i: (i, 0)), mod.spec(SC1), mod.spec(SH1),
                  _const_spec((D_MODEL, 2 * CF_W)), _const_spec((CF_CONV, CF_W)),
                  _const_spec((1, CF_W)), _const_spec((1, CF_W))],
        out_specs=pl.BlockSpec((tm, CF_W), lambda i: (i, 0)),
        out_shape=jax.ShapeDtypeStruct((rows, CF_W), F32),
        scratch_shapes=[pltpu.VMEM((tm // seg, seg + 2 * halo, CF_W), F32)],
        compiler_params=_params(1), name="cf_mixer",
    )(x, mod.arr, mod.arr, w, cw, ln_g, ln_b)


def _store_batch_major(dst_ref, stage_ref, value, tok0, n_tok, stage_row0=0):
    for c in range(stage_ref.shape[0]):
        lanes = slice(c * LANES, (c + 1) * LANES)
        stage_ref[c, stage_row0:stage_row0 + n_tok * BATCH, :] = value[:, lanes]
        for b in range(BATCH):
            dst_ref[b, tok0:tok0 + n_tok, lanes] = stage_ref[c, pl.ds(stage_row0 + b, n_tok, stride=BATCH), :]


def _load_token_major(src_ref, stage_ref, tok0, n_tok):
    cols = []
    for c in range(stage_ref.shape[0]):
        lanes = slice(c * LANES, (c + 1) * LANES)
        for b in range(BATCH):
            stage_ref[c, pl.ds(b, n_tok, stride=BATCH), :] = src_ref[b, pl.ds(tok0, n_tok), lanes]
        cols.append(stage_ref[c])
    return jnp.concatenate(cols, axis=-1)


def _qkv_kernel(x_ref, sc_ref, sh_ref, w_ref, cw_ref, o_ref, h_ref, pad_ref, stage_ref, *, seg):
    j = pl.program_id(1)
    tm = x_ref.shape[0]
    halo = SUBLANES

    @pl.when(j == 0)
    def _():
        zeros = jnp.zeros((halo, GDN_QK), F32)
        for s in range(tm // seg):
            pad_ref[s, 0:halo, :] = zeros
            pad_ref[s, halo + seg:halo + seg + halo, :] = zeros
        for p in range(tm // PIECE):
            rows = slice(p * PIECE, (p + 1) * PIECE)
            h_ref[rows, :] = _modulate(x_ref[rows, :], sc_ref[...], sh_ref[...]).astype(BF16)

    is_qk = j < 2
    scale = jnp.where(j == 0, GDN_DK ** -0.5, 1.0).astype(F32)
    n_pieces = tm // PIECE
    blk = 2 * CONV_ROWS

    def project(p):
        s, off = divmod(p * PIECE, seg)
        pad_ref[s, halo + off:halo + off + PIECE, :] = _dot(h_ref[p * PIECE:(p + 1) * PIECE, :], w_ref[...])

    def finish(p):
        s, off = divmod(p * PIECE, seg)
        for r0 in range(off, off + PIECE, blk):
            y = (cw_ref[0:1, :] * pad_ref[s, r0:r0 + blk, :]
                 + cw_ref[1:2, :] * pad_ref[s, r0 + halo:r0 + halo + blk, :]
                 + cw_ref[2:3, :] * pad_ref[s, r0 + 2 * halo:r0 + 2 * halo + blk, :])
            y = _silu(y)
            heads = []
            for hh in range(GDN_HEADS):
                t = y[:, hh * GDN_DK:(hh + 1) * GDN_DK]
                nrm = t * (lax.rsqrt(jnp.sum(t * t, axis=-1, keepdims=True) + RMS_EPS) * scale)
                heads.append(jnp.where(is_qk, nrm, t))
            row0 = s * seg + r0
            _store_batch_major(o_ref, stage_ref, jnp.concatenate(heads, axis=-1), row0 // SUBLANES, blk // SUBLANES,
                               stage_row0=row0)

    project(0)
    for p in range(n_pieces):
        if p + 1 < n_pieces:
            project(p + 1)
        finish(p)


def _qkv_proj(x, mod, w, cw, *, tm, seg):
    rows = x.shape[0]
    return pl.pallas_call(
        functools.partial(_qkv_kernel, seg=seg),
        grid=(rows // tm, 3),
        in_specs=[pl.BlockSpec((tm, D_MODEL), lambda i, j: (i, 0)), mod.spec(SC1), mod.spec(SH1),
                  pl.BlockSpec((D_MODEL, GDN_QK), lambda i, j: (0, j)),
                  pl.BlockSpec((3, GDN_QK), lambda i, j: (0, j))],
        out_specs=pl.BlockSpec((BATCH, tm // BATCH, GDN_QK), lambda i, j: (0, i, j)),
        out_shape=jax.ShapeDtypeStruct((BATCH, rows // BATCH, GDN_CONV_W), F32),
        scratch_shapes=[pltpu.VMEM((tm, D_MODEL), BF16),
                        pltpu.VMEM((tm // seg, seg + 2 * SUBLANES, GDN_QK), F32),
                        pltpu.VMEM((GDN_QK // LANES, tm, LANES), F32)],
        compiler_params=_params(2), name="qkv_proj",
    )(x, mod.arr, mod.arr, w, cw)


def _zbg_kernel(x_ref, sc_ref, sh_ref, w_ref, alog_ref, dtb_ref, z_ref, bg_ref, stage_ref):
    tm = x_ref.shape[0]
    for p in range(tm // PIECE):
        rows = slice(p * PIECE, (p + 1) * PIECE)
        h = _modulate(x_ref[rows, :], sc_ref[...], sh_ref[...]).astype(BF16)
        y = _dot(h, w_ref[...])
        _store_batch_major(z_ref, stage_ref, y[:, :GDN_W], p * (PIECE // BATCH), PIECE // BATCH)
        s = y[:, GDN_W:]
        t = s + dtb_ref[...]
        softplus = jnp.maximum(t, 0.0) + jnp.log1p(jnp.exp(-jnp.abs(t)))
        g = -jnp.exp(alog_ref[...]) * softplus
        lane = lax.broadcasted_iota(jnp.int32, s.shape, 1)
        bg_ref[rows, :] = jnp.where(lane < 2 * GDN_HEADS, _sigmoid(s), g)


def _even_in_kernel(x_ref, sc_ref, sh_ref, wcf_ref, wqkv_ref, wzbg_ref, cfw_ref, lng_ref, lnb_ref, gcw_ref,
                    alog_ref, dtb_ref, a_ref, qkv_ref, z_ref, bg_ref, cfpad_ref, qpad_ref, stage_ref):
    cf_halo = (CF_CONV // 2) * SUBLANES
    q_halo = SUBLANES
    cfpad_ref[0:cf_halo, :] = jnp.zeros((cf_halo, CF_W), F32)
    cfpad_ref[cf_halo + SEG_ROW:, :] = jnp.zeros((cf_halo, CF_W), F32)
    qpad_ref[0:q_halo, :] = jnp.zeros((q_halo, GDN_CONV_W), F32)
    qpad_ref[q_halo + SEG_ROW:, :] = jnp.zeros((q_halo, GDN_CONV_W), F32)

    h = _modulate(x_ref[...], sc_ref[...], sh_ref[...]).astype(BF16)
    y = _dot(h, wcf_ref[...])
    cfpad_ref[cf_halo:cf_halo + SEG_ROW, :] = y[:, :CF_W] * _sigmoid(y[:, CF_W:])
    for j in range(3):
        cols = slice(j * GDN_QK, (j + 1) * GDN_QK)
        qpad_ref[q_halo:q_halo + SEG_ROW, cols] = _dot(h, wqkv_ref[:, cols])
    yz = _dot(h, wzbg_ref[...])

    for r0 in range(0, SEG_ROW, CONV_ROWS):
        acc = cfw_ref[0:1, :] * cfpad_ref[r0:r0 + CONV_ROWS, :]
        for k in range(1, CF_CONV):
            acc = acc + cfw_ref[k:k + 1, :] * cfpad_ref[r0 + k * SUBLANES:r0 + k * SUBLANES + CONV_ROWS, :]
        a_ref[r0:r0 + CONV_ROWS, :] = _silu(_layer_norm(acc, lng_ref[...], lnb_ref[...]))

    _store_batch_major(z_ref, stage_ref.at[3], yz[:, :GDN_W], 0, SEG_ROW // BATCH)
    s = yz[:, GDN_W:]
    t = s + dtb_ref[...]
    softplus = jnp.maximum(t, 0.0) + jnp.log1p(jnp.exp(-jnp.abs(t)))
    lane = lax.broadcasted_iota(jnp.int32, s.shape, 1)
    bg_ref[...] = jnp.where(lane < 2 * GDN_HEADS, _sigmoid(s), -jnp.exp(alog_ref[...]) * softplus)

    blk = 2 * CONV_ROWS
    for j in range(3):
        cols = slice(j * GDN_QK, (j + 1) * GDN_QK)
        for r0 in range(0, SEG_ROW, blk):
            yq = (gcw_ref[0:1, cols] * qpad_ref[r0:r0 + blk, cols]
                  + gcw_ref[1:2, cols] * qpad_ref[r0 + q_halo:r0 + q_halo + blk, cols]
                  + gcw_ref[2:3, cols] * qpad_ref[r0 + 2 * q_halo:r0 + 2 * q_halo + blk, cols])
            yq = _silu(yq)
            if j < 2:
                scale = GDN_DK ** -0.5 if j == 0 else 1.0
                heads = []
                for hh in range(GDN_HEADS):
                    th = yq[:, hh * GDN_DK:(hh + 1) * GDN_DK]
                    heads.append(th * (lax.rsqrt(jnp.sum(th * th, axis=-1, keepdims=True) + RMS_EPS) * scale))
                yq = jnp.concatenate(heads, axis=-1)
            _store_batch_major(qkv_ref.at[:, :, cols], stage_ref.at[j], yq, r0 // SUBLANES, blk // SUBLANES, stage_row0=r0)


def _even_in(x, mod, w_cf, w_qkv, w_zbg, cf_w, ln_g, ln_b, gdn_w, alog, dtb):
    rows = x.shape[0]
    n_tok = SEG_ROW // BATCH
    return pl.pallas_call(
        _even_in_kernel,
        grid=(rows // SEG_ROW,),
        in_specs=[pl.BlockSpec((SEG_ROW, D_MODEL), lambda i: (i, 0)), mod.spec(SC1), mod.spec(SH1),
                  _const_spec((D_MODEL, 2 * CF_W)), _const_spec((D_MODEL, GDN_CONV_W)), _const_spec((D_MODEL, GDN_W + LANES)),
                  _const_spec((CF_CONV, CF_W)), _const_spec((1, CF_W)), _const_spec((1, CF_W)),
                  _const_spec((3, GDN_CONV_W)), _const_spec((1, LANES)), _const_spec((1, LANES))],
        out_specs=[pl.BlockSpec((SEG_ROW, CF_W), lambda i: (i, 0)),
                   pl.BlockSpec((BATCH, n_tok, GDN_CONV_W), lambda i: (0, i, 0)),
                   pl.BlockSpec((BATCH, n_tok, GDN_W), lambda i: (0, i, 0)),
                   pl.BlockSpec((SEG_ROW, LANES), lambda i: (i, 0))],
        out_shape=[jax.ShapeDtypeStruct((rows, CF_W), F32),
                   jax.ShapeDtypeStruct((BATCH, rows // BATCH, GDN_CONV_W), F32),
                   jax.ShapeDtypeStruct((BATCH, rows // BATCH, GDN_W), F32),
                   jax.ShapeDtypeStruct((rows, LANES), F32)],
        scratch_shapes=[pltpu.VMEM((SEG_ROW + 2 * (CF_CONV // 2) * SUBLANES, CF_W), F32),
                        pltpu.VMEM((SEG_ROW + 2 * SUBLANES, GDN_CONV_W), F32),
                        pltpu.VMEM((4, GDN_QK // LANES, SEG_ROW, LANES), F32)],
        compiler_params=_params(1), name="even_in",
    )(x, mod.arr, mod.arr, w_cf, w_qkv, w_zbg, cf_w, ln_g, ln_b, gdn_w, alog, dtb)


def _zbg_proj(x, mod, w, alog, dtb, *, tm):
    rows = x.shape[0]
    return pl.pallas_call(
        _zbg_kernel,
        grid=(rows // tm,),
        in_specs=[pl.BlockSpec((tm, D_MODEL), lambda i: (i, 0)), mod.spec(SC1), mod.spec(SH1),
                  _const_spec((D_MODEL, GDN_W + LANES)), _const_spec((1, LANES)), _const_spec((1, LANES))],
        out_specs=[pl.BlockSpec((BATCH, tm // BATCH, GDN_W), lambda i: (0, i, 0)),
                   pl.BlockSpec((tm, LANES), lambda i: (i, 0))],
        out_shape=[jax.ShapeDtypeStruct((BATCH, rows // BATCH, GDN_W), F32), jax.ShapeDtypeStruct((rows, LANES), F32)],
        scratch_shapes=[pltpu.VMEM((GDN_W // LANES, PIECE, LANES), F32)],
        compiler_params=_params(1), name="zbg_proj",
    )(x, mod.arr, mod.arr, w, alog, dtb)


def _gdn_decays(beta_row, g_row, rev):
    c = GDN_CK
    ri = lax.broadcasted_iota(jnp.int32, (c, c), 0)
    ci = lax.broadcasted_iota(jnp.int32, (c, c), 1)
    eye = ri == ci
    incl = (ri <= ci) if rev else (ri >= ci)
    incl_t = (ci <= ri) if rev else (ci >= ri)
    strict = (ri < ci) if rev else (ri > ci)
    g_b = jnp.broadcast_to(g_row, (c, c))
    beta_b = jnp.broadcast_to(beta_row, (c, c))
    g_col = jnp.sum(jnp.where(eye, g_b, 0.0), axis=1, keepdims=True)
    beta_col = jnp.sum(jnp.where(eye, beta_b, 0.0), axis=1, keepdims=True)
    gc_col = jnp.sum(jnp.where(incl, g_b, 0.0), axis=1, keepdims=True)
    gc_row = jnp.sum(jnp.where(incl_t, g_col, 0.0), axis=0, keepdims=True)
    g_sum = jnp.sum(g_row, axis=1, keepdims=True)
    decay = jnp.exp(jnp.where(incl, gc_col - gc_row, -1e30))
    return dict(strict=strict, beta_col=beta_col, decay=decay, e_col=jnp.exp(gc_col),
                e_end=jnp.exp(g_sum - gc_col), g_tot=jnp.exp(g_sum))


def _gdn_chunk_terms(chains, side_tasks=()):
    side_tasks = list(side_tasks)

    def side():
        if side_tasks:
            side_tasks.pop(0)()

    c = GDN_CK
    ri = lax.broadcasted_iota(jnp.int32, (c, c), 0)
    ci = lax.broadcasted_iota(jnp.int32, (c, c), 1)
    same = lambda s: (ri // s) == (ci // s)
    nils = [jnp.where(ch["strict"], -(ch["beta_col"] * ch["kk"] * ch["decay"]), 0.0) for ch in chains]
    nds = [jnp.where(same(GDN_BASE), n, 0.0) for n in nils]
    invs = [jnp.where(ri == ci, 1.0, 0.0) + nd for nd in nds]
    p16s = [nd.astype(BF16) for nd in nds]
    side()
    for _ in range(int(math.log2(GDN_BASE)) - 1):
        p16s = [_dot(p16, p16).astype(BF16) for p16 in p16s]
        side()
        invs = [inv + _dot(p16, inv.astype(BF16)) for p16, inv in zip(p16s, invs)]
    s = GDN_BASE
    while s < c:
        off = jnp.logical_and(same(2 * s), jnp.logical_not(same(s)))
        inv16s = [inv.astype(BF16) for inv in invs]
        side()
        m16s = [_dot(inv16, jnp.where(off, n, 0.0).astype(BF16)).astype(BF16) for inv16, n in zip(inv16s, nils)]
        side()
        invs = [inv + _dot(m16, inv16) for inv, m16, inv16 in zip(invs, m16s, inv16s)]
        s *= 2
    side()
    y16s = [_dot(inv.astype(BF16),
                 jnp.concatenate([ch["v"] * ch["beta_col"], ch["k"] * (ch["beta_col"] * ch["e_col"])], axis=-1).astype(BF16)
                 ).astype(BF16) for inv, ch in zip(invs, chains)]
    side()
    kts = [_dot_tn((ch["k"] * ch["e_end"]).astype(BF16), y16) for ch, y16 in zip(chains, y16s)]
    qys = [_dot((ch["qk_raw"] * ch["decay"]).astype(BF16), y16) for ch, y16 in zip(chains, y16s)]
    while side_tasks:
        side()
    out = []
    for ch, kt, qy in zip(chains, kts, qys):
        q_eff = ch["q"] * ch["e_col"] - qy[:, GDN_DV:]
        out.append(((-kt[:, GDN_DV:]).astype(BF16), kt[:, :GDN_DV], q_eff.astype(BF16), qy[:, :GDN_DV],
                    jnp.broadcast_to(ch["g_tot"], (1, GDN_DV))))
    return out


def _gdn_kernel(qc, kc, vc, zc, ql, kl, vl, zl, bg, ng_ref, oc_ref, ol_ref,
                q_all, k_all, v_all, o_dir, a_ref, b_ref, qe_ref, oz_ref, gt_ref, s_ref):
    ncc = CTX_LEN // GDN_CK
    for src, dst in ((qc, q_all), (kc, k_all), (vc, v_all)):
        dst[0:CTX_LEN, :] = src[...]
    for src, dst in ((ql, q_all), (kl, k_all), (vl, v_all)):
        for r in range(SEQ // PIECE):
            dst[CTX_LEN + r * PIECE:CTX_LEN + (r + 1) * PIECE, :] = src[r * PIECE:(r + 1) * PIECE, :]

    def chunk_at(d, p):
        if d == 0:
            return p
        return jnp.where(p < ncc, ncc - 1 - p, GDN_NCHUNK + ncc - 1 - p)

    def rows_of(cc):
        if isinstance(cc, int):
            return pl.ds(cc * GDN_CK, GDN_CK)
        return pl.ds(pl.multiple_of(cc * GDN_CK, GDN_CK), GDN_CK)

    def prepare(g, side_tasks=()):
        chains = []
        for jj in range(GDN_GROUP):
            for d in range(2):
                cc = chunk_at(d, g * GDN_GROUP + jj)
                rows = rows_of(cc)
                qv, kv, vv = q_all[rows, :], k_all[rows, :], v_all[rows, :]
                k16 = kv.astype(BF16)
                kq = _dot_nt(jnp.concatenate([k16, qv.astype(BF16)], axis=0), k16)
                chains.append(dict(q=qv, k=kv, v=vv, kk=kq[:GDN_CK], qk_raw=kq[GDN_CK:],
                                   **_gdn_decays(bg[d, pl.ds(cc, 1), :], bg[2 + d, pl.ds(cc, 1), :], rev=(d == 1))))
        terms = _gdn_chunk_terms(chains, side_tasks)
        for n, (a_neg, b_mat, q_eff, o_zero, g_tot) in enumerate(terms):
            jj, d = divmod(n, 2)
            a_ref[d, jj] = a_neg
            b_ref[d, jj] = b_mat
            qe_ref[d, jj] = q_eff
            oz_ref[d, jj] = o_zero
            gt_ref[d, jj] = g_tot

    def advance_tasks(g):
        def step(jj):
            for d in range(2):
                rows = rows_of(chunk_at(d, g * GDN_GROUP + jj))
                s = s_ref[d]
                s16 = s.astype(BF16)
                o_dir[d, rows, :] = oz_ref[d, jj] + _dot(qe_ref[d, jj], s16)
                s_ref[d] = s * gt_ref[d, jj] + _dot(a_ref[d, jj], s16) + b_ref[d, jj]
        return [functools.partial(step, jj) for jj in range(GDN_GROUP)]

    s_ref[...] = jnp.zeros(s_ref.shape, F32)
    n_groups = GDN_NCHUNK // GDN_GROUP
    prepare(0)

    def body(g, carry):
        prepare(g, advance_tasks(g - 1))
        return carry
    lax.fori_loop(1, n_groups, body, 0)
    for task in advance_tasks(n_groups - 1):
        task()

    blk = 256
    for z, o, tok0 in ((zc, oc_ref, 0), (zl, ol_ref, CTX_LEN)):
        for r in range(z.shape[0] // blk):
            rows = slice(r * blk, (r + 1) * blk)
            src = slice(tok0 + r * blk, tok0 + (r + 1) * blk)
            ov = o_dir[0, src, :] + o_dir[1, src, :]
            ov = ov * lax.rsqrt(jnp.mean(ov * ov, axis=-1, keepdims=True) + RMS_EPS) * ng_ref[...]
            o[rows, :] = ov * _silu(z[rows, :])


def _gdn(qkv_c, z_c, qkv_l, z_l, bg, norm_g):
    def tok_spec(n_tok, col0):
        return pl.BlockSpec((None, n_tok, GDN_DK), lambda b, h: (b, 0, col0 + h))

    in_specs = []
    for n_tok in (CTX_LEN, SEQ):
        in_specs += [tok_spec(n_tok, 0), tok_spec(n_tok, GDN_HEADS), tok_spec(n_tok, 2 * GDN_HEADS), tok_spec(n_tok, 0)]
    in_specs += [pl.BlockSpec((None, None, 4, GDN_NCHUNK, GDN_CK), lambda b, h: (b, h, 0, 0, 0)),
                 _const_spec((1, GDN_DV))]
    n_tok = CTX_LEN + SEQ
    return pl.pallas_call(
        _gdn_kernel,
        grid=(BATCH, GDN_HEADS),
        in_specs=in_specs,
        out_specs=[tok_spec(CTX_LEN, 0), tok_spec(SEQ, 0)],
        out_shape=[jax.ShapeDtypeStruct((BATCH, CTX_LEN, GDN_W), F32),
                   jax.ShapeDtypeStruct((BATCH, SEQ, GDN_W), F32)],
        scratch_shapes=[pltpu.VMEM((n_tok, GDN_DK), F32), pltpu.VMEM((n_tok, GDN_DK), F32),
                        pltpu.VMEM((n_tok, GDN_DV), F32), pltpu.VMEM((2, n_tok, GDN_DV), F32),
                        pltpu.VMEM((2, GDN_GROUP, GDN_DK, GDN_DV), BF16),
                        pltpu.VMEM((2, GDN_GROUP, GDN_DK, GDN_DV), F32),
                        pltpu.VMEM((2, GDN_GROUP, GDN_CK, GDN_DK), BF16),
                        pltpu.VMEM((2, GDN_GROUP, GDN_CK, GDN_DV), F32),
                        pltpu.VMEM((2, GDN_GROUP, 1, GDN_DV), F32),
                        pltpu.VMEM((2, GDN_DK, GDN_DV), F32)],
        compiler_params=_params(2), name="gdn",
    )(qkv_c, qkv_c, qkv_c, z_c, qkv_l, qkv_l, qkv_l, z_l, bg, norm_g)


def _residual_norm(x, branch, gate, g, b):
    return _layer_norm(DN_ALPHA * x + _per_sample(branch, gate, lambda a, c: a * c), g, b)


def _out_even_kernel(x_ref, a_ref, o_ref, g1_ref, wa_ref, wo_ref, lng_ref, lnb_ref, out_ref, stage_ref):
    tm = x_ref.shape[0]
    for p in range(tm // PIECE):
        rows = slice(p * PIECE, (p + 1) * PIECE)
        o = _load_token_major(o_ref, stage_ref, p * (PIECE // BATCH), PIECE // BATCH)
        m = _dot(a_ref[rows, :].astype(BF16), wa_ref[...]) + _dot(o.astype(BF16), wo_ref[...])
        out_ref[rows, :] = _residual_norm(x_ref[rows, :], m, g1_ref[...], lng_ref[...], lnb_ref[...])


def _out_even(x, a, o, mod, wa, wo, ln_g, ln_b, *, tm):
    rows = x.shape[0]
    row = lambda c: pl.BlockSpec((tm, c), lambda i: (i, 0))
    return pl.pallas_call(
        _out_even_kernel,
        grid=(rows // tm,),
        in_specs=[row(D_MODEL), row(CF_W), pl.BlockSpec((BATCH, tm // BATCH, GDN_W), lambda i: (0, i, 0)),
                  mod.spec(G1),
                  _const_spec((CF_W, D_MODEL)), _const_spec((GDN_W, D_MODEL)),
                  _const_spec((1, D_MODEL)), _const_spec((1, D_MODEL))],
        out_specs=row(D_MODEL),
        out_shape=jax.ShapeDtypeStruct((rows, D_MODEL), F32),
        scratch_shapes=[pltpu.VMEM((GDN_W // LANES, PIECE, LANES), F32)],
        compiler_params=_params(1), name="out_even",
    )(x, a, o, mod.arr, wa, wo, ln_g, ln_b)


def _gelu_tanh(x):
    return 0.5 * x * (1.0 + jnp.tanh(math.sqrt(2.0 / math.pi) * (x + 0.044715 * (x * x * x))))


def _out_odd_kernel(x_ref, u_ref, yf_ref, yb_ref, s_ref, g1_ref, d_ref, gw_ref, gb_ref, wa_ref, wo_ref,
                    lng_ref, lnb_ref, out_ref):
    tm = x_ref.shape[0]
    for p in range(tm // PIECE):
        rows = slice(p * PIECE, (p + 1) * PIECE)
        y = d_ref[...] * u_ref[rows, :] + yf_ref[rows, :] + yb_ref[rows, :]
        zg = _gelu_tanh(y)
        s5 = zg * _sigmoid(_dot(zg.astype(BF16), gw_ref[...]) + gb_ref[...])
        m = _dot(s5.astype(BF16), wa_ref[...]) + _dot(s_ref[rows, :].astype(BF16), wo_ref[...])
        out_ref[rows, :] = _residual_norm(x_ref[rows, :], m, g1_ref[...], lng_ref[...], lnb_ref[...])


def _out_odd(x, u, yf, yb, s, mod, d_skip, glu_w, glu_b, wa, wo, ln_g, ln_b, *, tm):
    rows = x.shape[0]
    row = lambda c: pl.BlockSpec((tm, c), lambda i: (i, 0))
    return pl.pallas_call(
        _out_odd_kernel,
        grid=(rows // tm,),
        in_specs=[row(D_MODEL), row(S5_W), row(S5_W), row(S5_W), row(SC_W), mod.spec(G1),
                  _const_spec((1, S5_W)), _const_spec((S5_W, S5_W)), _const_spec((1, S5_W)),
                  _const_spec((S5_W, D_MODEL)), _const_spec((SC_W, D_MODEL)),
                  _const_spec((1, D_MODEL)), _const_spec((1, D_MODEL))],
        out_specs=row(D_MODEL),
        out_shape=jax.ShapeDtypeStruct((rows, D_MODEL), F32),
        compiler_params=_params(1), name="out_odd",
    )(x, u, yf, yb, s, mod.arr, d_skip, glu_w, glu_b, wa, wo, ln_g, ln_b)


def _conv3(pad_ref, s, r0, n, cw_ref, shift):
    return (cw_ref[0:1, :] * pad_ref[s, r0:r0 + n, :]
            + cw_ref[1:2, :] * pad_ref[s, r0 + shift:r0 + shift + n, :]
            + cw_ref[2:3, :] * pad_ref[s, r0 + 2 * shift:r0 + 2 * shift + n, :])


def _ffn_kernel(x_ref, sc_ref, sh_ref, g2_ref, wu_ref, cw_ref, wd_ref, lng_ref, lnb_ref, out_ref,
                h_ref, val_ref, pad_ref, *, tm, shift, seg):
    j = pl.program_id(1)
    n_pieces = tm // PIECE
    tf = val_ref.shape[1]

    @pl.when(j == 0)
    def _():
        zeros = jnp.zeros((shift, tf), F32)
        for s in range(tm // seg):
            pad_ref[s, 0:shift, :] = zeros
            pad_ref[s, shift + seg:shift + seg + shift, :] = zeros
        for p in range(n_pieces):
            x = _load_rows(x_ref, p * PIECE, PIECE)
            h_ref[p * PIECE:(p + 1) * PIECE, :] = _modulate(x, sc_ref[...], sh_ref[...]).astype(BF16)

    for p in range(n_pieces):
        h = h_ref[p * PIECE:(p + 1) * PIECE, :]
        up = _dot(h, wu_ref[...])
        val_ref[p * PIECE:(p + 1) * PIECE, :] = up[:, :tf]
        s, off = divmod(p * PIECE, seg)
        pad_ref[s, shift + off:shift + off + PIECE, :] = up[:, tf:]

    for p in range(n_pieces):
        s, off = divmod(p * PIECE, seg)
        gate = _conv3(pad_ref, s, off, PIECE, cw_ref, shift)
        act = (val_ref[p * PIECE:(p + 1) * PIECE, :] * _silu(gate)).astype(BF16)
        part = _dot(act, wd_ref[...])

        @pl.when(j == 0)
        def _():
            _store_rows(out_ref, p * PIECE, part)

        @pl.when(j > 0)
        def _():
            _store_rows(out_ref, p * PIECE, _load_rows(out_ref, p * PIECE, PIECE) + part)

    @pl.when(j == pl.num_programs(1) - 1)
    def _():
        for p in range(n_pieces):
            x = _load_rows(x_ref, p * PIECE, PIECE)
            y = _residual_norm(x, _load_rows(out_ref, p * PIECE, PIECE), g2_ref[...], lng_ref[...], lnb_ref[...])
            _store_rows(out_ref, p * PIECE, y)


def _ffn(x, mod, w_up, cw, w_down, ln_g, ln_b, *, tiling, shift, seg, tf):
    tm = tiling.tm
    n_f = FFN_F // tf
    out = pl.pallas_call(
        functools.partial(_ffn_kernel, tm=tm, shift=shift, seg=seg),
        grid=(tiling.n_tiles, n_f),
        in_specs=[tiling.spec(D_MODEL), mod.spec(SC2), mod.spec(SH2), mod.spec(G2),
                  pl.BlockSpec((D_MODEL, 2 * tf), lambda i, j: (0, j)),
                  pl.BlockSpec((3, tf), lambda i, j: (0, j)),
                  pl.BlockSpec((tf, D_MODEL), lambda i, j: (j, 0)),
                  _const_spec((1, D_MODEL)), _const_spec((1, D_MODEL))],
        out_specs=tiling.spec(D_MODEL),
        out_shape=jax.ShapeDtypeStruct(tiling.shape(D_MODEL), F32),
        scratch_shapes=[pltpu.VMEM((tm, D_MODEL), BF16), pltpu.VMEM((tm, tf), F32),
                        pltpu.VMEM((tm // seg, seg + 2 * shift, tf), F32)],
        compiler_params=_params(2), name="conv_ffn",
    )(tiling.view(x), mod.arr, mod.arr, mod.arr, w_up, cw, w_down, ln_g, ln_b)
    return tiling.unview(out)


def _ffn_resident_kernel(x_ref, sc_ref, sh_ref, g2_ref, wu_ref, cw_ref, wd_ref, lng_ref, lnb_ref, out_ref,
                         h_ref, val_ref, pad_ref, *, tm, shift, seg, tf):
    n_pieces = tm // PIECE
    zeros = jnp.zeros((shift, tf), F32)
    for s in range(tm // seg):
        pad_ref[s, 0:shift, :] = zeros
        pad_ref[s, shift + seg:shift + seg + shift, :] = zeros
    for p in range(n_pieces):
        x = _load_rows(x_ref, p * PIECE, PIECE)
        h_ref[p * PIECE:(p + 1) * PIECE, :] = _modulate(x, sc_ref[...], sh_ref[...]).astype(BF16)

    for j in range(FFN_F // tf):
        for p in range(n_pieces):
            up = _dot(h_ref[p * PIECE:(p + 1) * PIECE, :], wu_ref[:, 2 * j * tf:2 * (j + 1) * tf])
            val_ref[p * PIECE:(p + 1) * PIECE, :] = up[:, :tf]
            s, off = divmod(p * PIECE, seg)
            pad_ref[s, shift + off:shift + off + PIECE, :] = up[:, tf:]
        for p in range(n_pieces):
            s, off = divmod(p * PIECE, seg)
            gate = _conv3(pad_ref, s, off, PIECE, cw_ref.at[:, j * tf:(j + 1) * tf], shift)
            act = (val_ref[p * PIECE:(p + 1) * PIECE, :] * _silu(gate)).astype(BF16)
            part = _dot(act, wd_ref[j * tf:(j + 1) * tf, :])
            if j > 0:
                part = _load_rows(out_ref, p * PIECE, PIECE) + part
            _store_rows(out_ref, p * PIECE, part)

    for p in range(n_pieces):
        x = _load_rows(x_ref, p * PIECE, PIECE)
        y = _residual_norm(x, _load_rows(out_ref, p * PIECE, PIECE), g2_ref[...], lng_ref[...], lnb_ref[...])
        _store_rows(out_ref, p * PIECE, y)


def _ffn_resident(x, mod, w_up, cw, w_down, ln_g, ln_b, *, tiling, shift, seg, tf):
    tm = tiling.tm
    resident = lambda shape: pl.BlockSpec(shape, lambda i: (0, 0), pipeline_mode=pl.Buffered(1))
    out = pl.pallas_call(
        functools.partial(_ffn_resident_kernel, tm=tm, shift=shift, seg=seg, tf=tf),
        grid=(tiling.n_tiles,),
        in_specs=[tiling.spec(D_MODEL), mod.spec(SC2), mod.spec(SH2), mod.spec(G2),
                  resident((D_MODEL, 2 * FFN_F)), _const_spec((3, FFN_F)), resident((FFN_F, D_MODEL)),
                  _const_spec((1, D_MODEL)), _const_spec((1, D_MODEL))],
        out_specs=tiling.spec(D_MODEL),
        out_shape=jax.ShapeDtypeStruct(tiling.shape(D_MODEL), F32),
        scratch_shapes=[pltpu.VMEM((tm, D_MODEL), BF16), pltpu.VMEM((tm, tf), F32),
                        pltpu.VMEM((tm // seg, seg + 2 * shift, tf), F32)],
        compiler_params=_params(1), name="conv_ffn_resident",
    )(tiling.view(x), mod.arr, mod.arr, mod.arr, w_up, cw, w_down, ln_g, ln_b)
    return tiling.unview(out)


def _odd_in_kernel(x_ref, sc_ref, sh_ref, wu_ref, wb_ref, wc_ref, wx_ref, cw_ref, u_ref, s_ref,
                   h_ref, b_ref, pad_ref, *, tm, shift, seg):
    j = pl.program_id(1)
    n_pieces = tm // PIECE
    cb = u_ref.shape[-1]

    @pl.when(j == 0)
    def _():
        zeros = jnp.zeros((shift, cb), F32)
        for s in range(tm // seg):
            pad_ref[s, 0:shift, :] = zeros
            pad_ref[s, shift + seg:shift + seg + shift, :] = zeros
        for p in range(n_pieces):
            x = _load_rows(x_ref, p * PIECE, PIECE)
            h_ref[p * PIECE:(p + 1) * PIECE, :] = _modulate(x, sc_ref[...], sh_ref[...]).astype(BF16)

    for p in range(n_pieces):
        h = h_ref[p * PIECE:(p + 1) * PIECE, :]
        _store_rows(u_ref, p * PIECE, _dot(h, wu_ref[...]))
        b_ref[p * PIECE:(p + 1) * PIECE, :] = _dot(h, wb_ref[...])
        s, off = divmod(p * PIECE, seg)
        pad_ref[s, shift + off:shift + off + PIECE, :] = _dot(h, wc_ref[...]) * _dot(h, wx_ref[...])

    for p in range(n_pieces):
        s, off = divmod(p * PIECE, seg)
        _store_rows(s_ref, p * PIECE, b_ref[p * PIECE:(p + 1) * PIECE, :] * _conv3(pad_ref, s, off, PIECE, cw_ref, shift))


def _odd_in(x, mod, w_in, cw, *, tiling, shift, seg):
    tm = tiling.tm
    cb = 2 * LANES
    nb = S5_W // cb
    wspec = lambda k: pl.BlockSpec((D_MODEL, cb), lambda i, j: (0, k * nb + j))
    u, s = pl.pallas_call(
        functools.partial(_odd_in_kernel, tm=tm, shift=shift, seg=seg),
        grid=(tiling.n_tiles, nb),
        in_specs=[tiling.spec(D_MODEL), mod.spec(SC1), mod.spec(SH1),
                  wspec(0), wspec(1), wspec(2), wspec(3),
                  pl.BlockSpec((3, cb), lambda i, j: (0, j))],
        out_specs=[tiling.spec(cb, lambda i, j: j), tiling.spec(cb, lambda i, j: j)],
        out_shape=[jax.ShapeDtypeStruct(tiling.shape(S5_W), F32), jax.ShapeDtypeStruct(tiling.shape(SC_W), F32)],
        scratch_shapes=[pltpu.VMEM((tm, D_MODEL), BF16), pltpu.VMEM((tm, cb), F32),
                        pltpu.VMEM((tm // seg, seg + 2 * shift, cb), F32)],
        compiler_params=_params(2), name="odd_in",
    )(tiling.view(x), mod.arr, mod.arr, w_in, w_in, w_in, w_in, cw)
    return tiling.unview(u), tiling.unview(s)


def _s5_kernel(uf_ref, ub_ref, wb_ref, wc_ref, a_ref, x0_ref, yf_ref, yb_ref, xfin_ref, bu_ref, st_ref):
    t = pl.program_id(0)
    rows = uf_ref.shape[0]
    n_tok = rows // SUBLANES
    hs = S5_HSTATE

    @pl.when(t == 0)
    def _():
        st_ref[...] = x0_ref[...]

    for d, u_ref in enumerate((uf_ref, ub_ref)):
        for hf in range(2):
            u = u_ref[:, hf * S5_HALF:(hf + 1) * S5_HALF].astype(BF16)
            bu_ref[d, :, hf * 2 * hs:(hf + 1) * 2 * hs] = _dot(u, wb_ref[d, hf])

    for d, y_ref in enumerate((yf_ref, yb_ref)):
        for hf in range(2):
            c_re = hf * 2 * hs
            c_im = c_re + hs
            a_re = jnp.broadcast_to(a_ref[d, hf, 0:1, :], (SUBLANES, hs))
            a_im = jnp.broadcast_to(a_ref[d, hf, 1:2, :], (SUBLANES, hs))
            xr, xi = st_ref[d, :, c_re:c_re + hs], st_ref[d, :, c_im:c_im + hs]
            for i in range(n_tok):
                tok = i if d == 0 else n_tok - 1 - i
                r = slice(tok * SUBLANES, (tok + 1) * SUBLANES)
                xr, xi = (a_re * xr - a_im * xi + bu_ref[d, r, c_re:c_re + hs],
                          a_re * xi + a_im * xr + bu_ref[d, r, c_im:c_im + hs])
                bu_ref[d, r, c_re:c_re + hs] = xr
                bu_ref[d, r, c_im:c_im + hs] = xi
            st_ref[d, :, c_re:c_re + hs] = xr
            st_ref[d, :, c_im:c_im + hs] = xi
        for hf in range(2):
            xs = bu_ref[d, :, hf * 2 * hs:(hf + 1) * 2 * hs].astype(BF16)
            y_ref[:, hf * S5_HALF:(hf + 1) * S5_HALF] = _dot(xs, wc_ref[d, hf])

    @pl.when(t == pl.num_programs(0) - 1)
    def _():
        xfin_ref[...] = st_ref[...]


def _s5(u, wb, wc, a, x0):
    rows = u.shape[0]
    tr = S5_TOK * SUBLANES
    nt = rows // tr
    state = (2, SUBLANES, 4 * S5_HSTATE)
    return pl.pallas_call(
        _s5_kernel,
        grid=(nt,),
        in_specs=[pl.BlockSpec((tr, S5_W), lambda t: (t, 0)),
                  pl.BlockSpec((tr, S5_W), lambda t: (nt - 1 - t, 0)),
                  _const_spec((2, 2, S5_HALF, 2 * S5_HSTATE)), _const_spec((2, 2, 2 * S5_HSTATE, S5_HALF)),
                  _const_spec((2, 2, 2, S5_HSTATE)), _const_spec(state)],
        out_specs=[pl.BlockSpec((tr, S5_W), lambda t: (t, 0)),
                   pl.BlockSpec((tr, S5_W), lambda t: (nt - 1 - t, 0)),
                   _const_spec(state)],
        out_shape=[jax.ShapeDtypeStruct((rows, S5_W), F32), jax.ShapeDtypeStruct((rows, S5_W), F32),
                   jax.ShapeDtypeStruct(state, F32)],
        scratch_shapes=[pltpu.VMEM((2, tr, 4 * S5_HSTATE), F32), pltpu.VMEM(state, F32)],
        compiler_params=_params(1), name="s5_scan",
    )(u, u, wb, wc, a, x0)


def _s5_weights(lam_re, lam_im, log_dt, b_re, b_im, c_re, c_im):
    eye = jnp.eye(S5_G // 2, dtype=F32)
    wbs, wcs, avs = [], [], []
    for di in range(2):
        lr, li = lam_re[di], lam_im[di]
        dt = jnp.exp(log_dt[di])[:, None]
        mag = jnp.exp(lr * dt)
        ar, ai = mag * jnp.cos(li * dt), mag * jnp.sin(li * dt)
        den = lr * lr + li * li
        fr = ((ar - 1.0) * lr + ai * li) / den
        fi = (ai * lr - (ar - 1.0) * li) / den
        bbr = fr[..., None] * b_re - fi[..., None] * b_im
        bbi = fr[..., None] * b_im + fi[..., None] * b_re

        def in_block(t):
            t = t.reshape(2, S5_G // 2, S5_N, S5_P)
            return jnp.einsum("hgnp,gk->hgpkn", t, eye).reshape(2, S5_HALF, S5_HSTATE)

        def out_block(t):
            t = t.reshape(2, S5_G // 2, S5_P, S5_N)
            return jnp.einsum("hgpn,gk->hgnkp", t, eye).reshape(2, S5_HSTATE, S5_HALF)

        wbs.append(jnp.concatenate([in_block(bbr), in_block(bbi)], axis=-1))
        wcs.append(jnp.concatenate([out_block(c_re), out_block(-c_im)], axis=1))
        avs.append(jnp.stack([ar.reshape(2, S5_HSTATE), ai.reshape(2, S5_HSTATE)], axis=1))
    return jnp.stack(wbs).astype(BF16), jnp.stack(wcs).astype(BF16), jnp.stack(avs)


def _bg_rows(bg, n_tok):
    t = bg.reshape(n_tok, BATCH, LANES)[:, :, :4 * GDN_HEADS].reshape(n_tok, BATCH, 4, GDN_HEADS)
    return t.transpose(1, 3, 2, 0).reshape(BATCH, GDN_HEADS, 4, n_tok // GDN_CK, GDN_CK)


def _to_batch_major(a, n_tok):
    return a.reshape(n_tok, BATCH, a.shape[-1]).transpose(1, 0, 2)


def _to_token_major(a):
    return a.transpose(1, 0, 2).reshape(a.shape[0] * a.shape[1], a.shape[2])


def kernel(x, c, ctx, c_ctx, ada_w, ada_b, ln_g, ln_b, ev_w_in, ev_w_out, cf_conv, cf_ln_g, cf_ln_b, gdn_conv,
           gdn_a_log, gdn_dt_bias, gdn_norm_g, od_w_in, od_w_out, s5_lam_re, s5_lam_im, s5_log_dt, s5_b_re,
           s5_b_im, s5_c_re, s5_c_im, s5_d, s5_glu_w, s5_glu_b, sc_conv, ffn_w_up, ffn_conv, ffn_w_down):
    xl = _to_token_major(x)
    cl = _to_token_major(ctx)

    cvec = jnp.concatenate([c, jnp.broadcast_to(c_ctx[None, :], (SUBLANES, D_MODEL))], axis=0)
    mods = _ada(cvec, ada_w, ada_b).reshape(DEPTH, 2, SUBLANES, 6 * D_MODEL)

    lat_row = _Tiling("row", ROWS_LAT, 2 * SEG_ROW)
    lat_col = _Tiling("col", ROWS_LAT, 2 * SEG_ROW)
    ctx_row = _Tiling("row", ROWS_CTX, ROWS_CTX)
    lat_along_row = dict(tiling=lat_row, shift=SUBLANES, seg=SEG_ROW)
    lat_along_col = dict(tiling=lat_col, shift=lat_col.rg, seg=lat_col.tm)
    ctx_conv = dict(tiling=ctx_row, shift=SUBLANES, seg=ROWS_CTX)

    row2 = lambda v: v.reshape(1, -1)
    for i in range(DEPTH):
        j = i // 2
        last = i == DEPTH - 1
        m_lat, m_ctx = _Mod(mods, i, 0), _Mod(mods, i, 1)
        lng1, lnb1, lng2, lnb2 = row2(ln_g[i, 0]), row2(ln_b[i, 0]), row2(ln_g[i, 1]), row2(ln_b[i, 1])
        w_down, f_cw = ffn_w_down[i].astype(BF16), ffn_conv[i]

        def up_blocks(tf, w_up=ffn_w_up[i]):
            cols = [w_up[:, k * FFN_F + j * tf:k * FFN_F + (j + 1) * tf] for j in range(FFN_F // tf) for k in range(2)]
            return jnp.concatenate(cols, axis=1).astype(BF16)

        if i % 2 == 0:
            w_in = ev_w_in[j]
            w_cf = w_in[:, :2 * CF_W].astype(BF16)
            w_qkv = w_in[:, 2 * CF_W:2 * CF_W + GDN_CONV_W].astype(BF16)
            w_zbg = jnp.pad(w_in[:, 2 * CF_W + GDN_CONV_W:], ((0, 0), (0, LANES - 4 * GDN_HEADS))).astype(BF16)
            pad16 = lambda v: jnp.pad(v.reshape(1, -1), ((0, 0), (2 * GDN_HEADS, LANES - 4 * GDN_HEADS)))
            alog, dtb = pad16(gdn_a_log[j]), pad16(gdn_dt_bias[j])
            w_out = ev_w_out[j].astype(BF16)
            cf_ln = (row2(cf_ln_g[j]), row2(cf_ln_b[j]))
            a_c = _cf_mixer(cl, m_ctx, w_cf, cf_conv[j], *cf_ln, tm=ROWS_CTX, seg=ROWS_CTX)
            qkv_c = _qkv_proj(cl, m_ctx, w_qkv, gdn_conv[j], tm=ROWS_CTX, seg=ROWS_CTX)
            z_c, bg_c = _zbg_proj(cl, m_ctx, w_zbg, alog, dtb, tm=ROWS_CTX)
            a_l, qkv_l, z_l, bg_l = _even_in(xl, m_lat, w_cf, w_qkv, w_zbg, cf_conv[j], *cf_ln, gdn_conv[j], alog, dtb)
            bg = jnp.concatenate([_bg_rows(bg_c, CTX_LEN), _bg_rows(bg_l, SEQ)], axis=3)
            o_c, o_l = _gdn(qkv_c, z_c, qkv_l, z_l, bg, row2(gdn_norm_g[j]))
            x1 = _out_even(xl, a_l, o_l, m_lat, w_out[:CF_W], w_out[CF_W:], lng1, lnb1, tm=2 * SEG_ROW)
            if not last:
                c1 = _out_even(cl, a_c, o_c, m_ctx, w_out[:CF_W], w_out[CF_W:], lng1, lnb1, tm=2 * SEG_ROW)
            lat_ffn = lat_along_col
        else:
            w_in = od_w_in[j].astype(BF16)
            w_out = od_w_out[j].astype(BF16)
            wb, wc, av = _s5_weights(s5_lam_re[j], s5_lam_im[j], s5_log_dt[j], s5_b_re[j], s5_b_im[j],
                                     s5_c_re[j], s5_c_im[j])
            u_c, s_c = _odd_in(cl, m_ctx, w_in, sc_conv[j], **ctx_conv)
            u_l, s_l = _odd_in(xl, m_lat, w_in, sc_conv[j], **lat_along_col)
            zero_state = jnp.zeros((2, SUBLANES, 4 * S5_HSTATE), F32)
            yf_c, yb_c, fin_c = _s5(u_c, wb, wc, av, zero_state)
            yf_l, yb_l, _ = _s5(u_l, wb, wc, av, fin_c)
            odd_w = (row2(s5_d[j]), s5_glu_w[j].astype(BF16), row2(s5_glu_b[j]), w_out[:S5_W], w_out[S5_W:])
            x1 = _out_odd(xl, u_l, yf_l, yb_l, s_l, m_lat, *odd_w, lng1, lnb1, tm=2 * SEG_ROW)
            if not last:
                c1 = _out_odd(cl, u_c, yf_c, yb_c, s_c, m_ctx, *odd_w, lng1, lnb1, tm=2 * SEG_ROW)
            lat_ffn = lat_along_row

        xl = _ffn_resident(x1, m_lat, up_blocks(FFN_TF_LAT), f_cw, w_down, lng2, lnb2,
                           tf=FFN_TF_LAT, **lat_ffn)
        if not last:
            cl = _ffn(c1, m_ctx, up_blocks(FFN_TF_CTX), f_cw, w_down, lng2, lnb2,
                      tf=FFN_TF_CTX, **ctx_conv)

    return xl.reshape(SEQ, BATCH, D_MODEL).transpose(1, 0, 2)
```

```python
import functools
import math

import jax
import jax.numpy as jnp
from jax import lax
from jax.experimental import pallas as pl
from jax.experimental.pallas import tpu as pltpu

D_MODEL = 1024
BATCH = 8
SEQ = 2048
DEPTH = 4
GRID_W = 64
GRID_H = SEQ // GRID_W
CTX_LEN = 256
CF_W = 512
CF_CONV = 31
GDN_HEADS = 4
GDN_DK = 128
GDN_DV = 128
GDN_QK = GDN_HEADS * GDN_DK
GDN_W = GDN_HEADS * GDN_DV
GDN_CONV_W = 2 * GDN_QK + GDN_W
S5_W = 512
S5_P = 16
S5_G = S5_W // S5_P
S5_N = 64
SC_W = 512
FFN_F = 2816
DN_ALPHA = (2 * DEPTH) ** 0.25
LN_EPS = 1e-5
RMS_EPS = 1e-6

SUBLANES = 8
LANES = 128
VMEM_LIMIT_BYTES = 56 * 1024 * 1024

ROWS_LAT = SEQ * BATCH
ROWS_CTX = CTX_LEN * BATCH
SEG_ROW = GRID_W * BATCH
PIECE = 512
CONV_ROWS = 64
FFN_TF_LAT = FFN_F // 2
FFN_TF_CTX = 256
S5_HALF = S5_W // 2
S5_HSTATE = (S5_G // 2) * S5_N
S5_TOK = 64
GDN_CK = 64
GDN_BASE = 16
GDN_NCHUNK = (CTX_LEN + SEQ) // GDN_CK
GDN_GROUP = 9

F32 = jnp.float32
BF16 = jnp.bfloat16

assert BATCH == SUBLANES


def _dot(a, b):
    return jnp.dot(a, b, preferred_element_type=F32)


def _dot_nt(a, b):
    return lax.dot_general(a, b, (((1,), (1,)), ((), ())), preferred_element_type=F32)


def _dot_tn(a, b):
    return lax.dot_general(a, b, (((0,), (0,)), ((), ())), preferred_element_type=F32)


def _sigmoid(x):
    return 1.0 / (1.0 + jnp.exp(-x))


def _silu(x):
    return x * _sigmoid(x)


def _per_sample(x, vec, op):
    r, c = x.shape
    x3 = x.reshape(r // SUBLANES, SUBLANES, c)
    return op(x3, vec[None]).reshape(r, c)


def _modulate(x, scale, shift):
    y = _per_sample(x, 1.0 + scale, lambda a, b: a * b)
    return _per_sample(y, shift, lambda a, b: a + b)


def _layer_norm(x, g, b):
    mu = jnp.mean(x, axis=-1, keepdims=True)
    xc = x - mu
    var = jnp.mean(xc * xc, axis=-1, keepdims=True)
    return xc * lax.rsqrt(var + LN_EPS) * g + b


def _load_rows(ref, start, n):
    if len(ref.shape) == 2:
        return ref[start:start + n, :]
    rg = ref.shape[1]
    return ref[start // rg:(start + n) // rg, :, :].reshape(n, ref.shape[2])


def _store_rows(ref, start, value):
    n = value.shape[0]
    if len(ref.shape) == 2:
        ref[start:start + n, :] = value
    else:
        rg = ref.shape[1]
        ref[start // rg:(start + n) // rg, :, :] = value.reshape(n // rg, rg, value.shape[1])


def _params(n_axes):
    return pltpu.CompilerParams(dimension_semantics=("arbitrary",) * n_axes,
                                vmem_limit_bytes=VMEM_LIMIT_BYTES)


class _Tiling:
    def __init__(self, kind, rows, tm):
        self.kind, self.rows, self.tm = kind, rows, tm
        self.n_tiles = rows // tm
        if kind == "col":
            self.rg = tm // GRID_H

    def view(self, a):
        if self.kind == "row":
            return a
        return a.reshape(GRID_H, SEG_ROW, a.shape[-1])

    def unview(self, a):
        return a.reshape(self.rows, a.shape[-1])

    def shape(self, c):
        return (self.rows, c) if self.kind == "row" else (GRID_H, SEG_ROW, c)

    def spec(self, c, col=None):
        col = col or (lambda *ij: 0)
        if self.kind == "row":
            return pl.BlockSpec((self.tm, c), lambda *ij: (ij[0], col(*ij)))
        return pl.BlockSpec((GRID_H, self.rg, c), lambda *ij: (0, ij[0], col(*ij)))


def _const_spec(shape):
    nd = len(shape)
    return pl.BlockSpec(shape, lambda *ij: (0,) * nd)


SH1, SC1, G1, SH2, SC2, G2 = range(6)


class _Mod:
    def __init__(self, arr, layer, stream):
        self.arr, self.layer, self.stream = arr, layer, stream

    def spec(self, col):
        layer, stream = self.layer, self.stream
        return pl.BlockSpec((None, None, SUBLANES, D_MODEL), lambda *ij: (layer, stream, 0, col))


def _ada_kernel(c_ref, w_ref, b_ref, o_ref):
    s = _silu(c_ref[...]).astype(BF16)
    o_ref[...] = _dot(s, w_ref[...].astype(BF16)) + b_ref[...]


def _ada(cvec, ada_w, ada_b):
    return pl.pallas_call(
        _ada_kernel,
        grid=(DEPTH, 6),
        in_specs=[_const_spec((2 * SUBLANES, D_MODEL)),
                  pl.BlockSpec((None, D_MODEL, D_MODEL), lambda i, j: (i, 0, j)),
                  pl.BlockSpec((None, 1, D_MODEL), lambda i, j: (i, 0, j))],
        out_specs=pl.BlockSpec((None, 2 * SUBLANES, D_MODEL), lambda i, j: (i, 0, j)),
        out_shape=jax.ShapeDtypeStruct((DEPTH, 2 * SUBLANES, 6 * D_MODEL), F32),
        compiler_params=_params(2), name="ada",
    )(cvec, ada_w, ada_b.reshape(DEPTH, 1, 6 * D_MODEL))


def _cf_kernel(x_ref, sc_ref, sh_ref, w_ref, cw_ref, g_ref, b_ref, o_ref, pad_ref, *, seg):
    tm = x_ref.shape[0]
    halo = (CF_CONV // 2) * SUBLANES
    zeros = jnp.zeros((halo, CF_W), F32)
    for s in range(tm // seg):
        pad_ref[s, 0:halo, :] = zeros
        pad_ref[s, halo + seg:halo + seg + halo, :] = zeros
    def project(p):
        h = _modulate(x_ref[p * PIECE:(p + 1) * PIECE, :], sc_ref[...], sh_ref[...]).astype(BF16)
        y = _dot(h, w_ref[...])
        glu = y[:, :CF_W] * _sigmoid(y[:, CF_W:])
        s, off = divmod(p * PIECE, seg)
        pad_ref[s, halo + off:halo + off + PIECE, :] = glu

    for p in range(tm // PIECE):
        project(p)
    for s in range(tm // seg):
        def body(i, carry, s=s):
            for r0 in (pl.multiple_of(i * 2 * CONV_ROWS, CONV_ROWS), pl.multiple_of((i * 2 + 1) * CONV_ROWS, CONV_ROWS)):
                acc = cw_ref[0:1, :] * pad_ref[s, pl.ds(r0, CONV_ROWS), :]
                for k in range(1, CF_CONV):
                    acc = acc + cw_ref[k:k + 1, :] * pad_ref[s, pl.ds(r0 + k * SUBLANES, CONV_ROWS), :]
                o_ref[pl.ds(s * seg + r0, CONV_ROWS), :] = _silu(_layer_norm(acc, g_ref[...], b_ref[...])).astype(BF16)
            return carry
        lax.fori_loop(0, seg // (2 * CONV_ROWS), body, 0)


def _cf_mixer(x, mod, w, cw, ln_g, ln_b, *, tm, seg):
    rows = x.shape[0]
    halo = (CF_CONV // 2) * SUBLANES
    return pl.pallas_call(
        functools.partial(_cf_kernel, seg=seg),
        grid=(rows // tm,),
        in_specs=[pl.BlockSpec((tm, D_MODEL), lambda i: (i, 0)), mod.spec(SC1), mod.spec(SH1),
                  _const_spec((D_MODEL, 2 * CF_W)), _const_spec((CF_CONV, CF_W)),
                  _const_spec((1, CF_W)), _const_spec((1, CF_W))],
        out_specs=pl.BlockSpec((tm, CF_W), lambda i: (i, 0)),
        out_shape=jax.ShapeDtypeStruct((rows, CF_W), BF16),
        scratch_shapes=[pltpu.VMEM((tm // seg, seg + 2 * halo, CF_W), F32)],
        compiler_params=_params(1), name="cf_mixer",
    )(x, mod.arr, mod.arr, w, cw, ln_g, ln_b)


def _store_batch_major(dst_ref, stage_ref, value, tok0, n_tok, stage_row0=0):
    for c in range(stage_ref.shape[0]):
        lanes = slice(c * LANES, (c + 1) * LANES)
        stage_ref[c, stage_row0:stage_row0 + n_tok * BATCH, :] = value[:, lanes]
        for b in range(BATCH):
            dst_ref[b, tok0:tok0 + n_tok, lanes] = stage_ref[c, pl.ds(stage_row0 + b, n_tok, stride=BATCH), :]


def _load_token_major(src_ref, stage_ref, tok0, n_tok):
    cols = []
    for c in range(stage_ref.shape[0]):
        lanes = slice(c * LANES, (c + 1) * LANES)
        for b in range(BATCH):
            stage_ref[c, pl.ds(b, n_tok, stride=BATCH), :] = src_ref[b, pl.ds(tok0, n_tok), lanes]
        cols.append(stage_ref[c])
    return jnp.concatenate(cols, axis=-1)


def _qkv_kernel(x_ref, sc_ref, sh_ref, w_ref, cw_ref, o_ref, h_ref, pad_ref, stage_ref, *, seg):
    j = pl.program_id(1)
    tm = x_ref.shape[0]
    halo = SUBLANES

    @pl.when(j == 0)
    def _():
        zeros = jnp.zeros((halo, GDN_QK), F32)
        for s in range(tm // seg):
            pad_ref[s, 0:halo, :] = zeros
            pad_ref[s, halo + seg:halo + seg + halo, :] = zeros
        for p in range(tm // PIECE):
            rows = slice(p * PIECE, (p + 1) * PIECE)
            h_ref[rows, :] = _modulate(x_ref[rows, :], sc_ref[...], sh_ref[...]).astype(BF16)

    is_qk = j < 2
    scale = jnp.where(j == 0, GDN_DK ** -0.5, 1.0).astype(F32)
    n_pieces = tm // PIECE
    blk = 2 * CONV_ROWS

    def project(p):
        s, off = divmod(p * PIECE, seg)
        pad_ref[s, halo + off:halo + off + PIECE, :] = _dot(h_ref[p * PIECE:(p + 1) * PIECE, :], w_ref[...])

    def finish(p):
        s, off = divmod(p * PIECE, seg)
        for r0 in range(off, off + PIECE, blk):
            y = (cw_ref[0:1, :] * pad_ref[s, r0:r0 + blk, :]
                 + cw_ref[1:2, :] * pad_ref[s, r0 + halo:r0 + halo + blk, :]
                 + cw_ref[2:3, :] * pad_ref[s, r0 + 2 * halo:r0 + 2 * halo + blk, :])
            y = _silu(y)
            heads = []
            for hh in range(GDN_HEADS):
                t = y[:, hh * GDN_DK:(hh + 1) * GDN_DK]
                nrm = t * (lax.rsqrt(jnp.sum(t * t, axis=-1, keepdims=True) + RMS_EPS) * scale)
                heads.append(jnp.where(is_qk, nrm, t))
            row0 = s * seg + r0
            _store_batch_major(o_ref, stage_ref, jnp.concatenate(heads, axis=-1), row0 // SUBLANES, blk // SUBLANES,
                               stage_row0=row0)

    project(0)
    for p in range(n_pieces):
        if p + 1 < n_pieces:
            project(p + 1)
        finish(p)


def _qkv_proj(x, mod, w, cw, *, tm, seg):
    rows = x.shape[0]
    return pl.pallas_call(
        functools.partial(_qkv_kernel, seg=seg),
        grid=(rows // tm, 3),
        in_specs=[pl.BlockSpec((tm, D_MODEL), lambda i, j: (i, 0)), mod.spec(SC1), mod.spec(SH1),
                  pl.BlockSpec((D_MODEL, GDN_QK), lambda i, j: (0, j)),
                  pl.BlockSpec((3, GDN_QK), lambda i, j: (0, j))],
        out_specs=pl.BlockSpec((BATCH, tm // BATCH, GDN_QK), lambda i, j: (0, i, j)),
        out_shape=jax.ShapeDtypeStruct((BATCH, rows // BATCH, GDN_CONV_W), F32),
        scratch_shapes=[pltpu.VMEM((tm, D_MODEL), BF16),
                        pltpu.VMEM((tm // seg, seg + 2 * SUBLANES, GDN_QK), F32),
                        pltpu.VMEM((GDN_QK // LANES, tm, LANES), F32)],
        compiler_params=_params(2), name="qkv_proj",
    )(x, mod.arr, mod.arr, w, cw)


def _zbg_kernel(x_ref, sc_ref, sh_ref, w_ref, alog_ref, dtb_ref, z_ref, bg_ref, stage_ref):
    tm = x_ref.shape[0]
    for p in range(tm // PIECE):
        rows = slice(p * PIECE, (p + 1) * PIECE)
        h = _modulate(x_ref[rows, :], sc_ref[...], sh_ref[...]).astype(BF16)
        y = _dot(h, w_ref[...])
        _store_batch_major(z_ref, stage_ref, y[:, :GDN_W], p * (PIECE // BATCH), PIECE // BATCH)
        s = y[:, GDN_W:]
        t = s + dtb_ref[...]
        softplus = jnp.maximum(t, 0.0) + jnp.log1p(jnp.exp(-jnp.abs(t)))
        g = -jnp.exp(alog_ref[...]) * softplus
        lane = lax.broadcasted_iota(jnp.int32, s.shape, 1)
        bg_ref[rows, :] = jnp.where(lane < 2 * GDN_HEADS, _sigmoid(s), g)


def _even_in_kernel(x_ref, sc_ref, sh_ref, wcf_ref, wqkv_ref, wzbg_ref, cfw_ref, lng_ref, lnb_ref, gcw_ref,
                    alog_ref, dtb_ref, a_ref, qkv_ref, z_ref, bg_ref, cfpad_ref, qpad_ref, stage_ref):
    cf_halo = (CF_CONV // 2) * SUBLANES
    q_halo = SUBLANES
    cfpad_ref[0:cf_halo, :] = jnp.zeros((cf_halo, CF_W), F32)
    cfpad_ref[cf_halo + SEG_ROW:, :] = jnp.zeros((cf_halo, CF_W), F32)
    qpad_ref[0:q_halo, :] = jnp.zeros((q_halo, GDN_CONV_W), F32)
    qpad_ref[q_halo + SEG_ROW:, :] = jnp.zeros((q_halo, GDN_CONV_W), F32)

    h = _modulate(x_ref[...], sc_ref[...], sh_ref[...]).astype(BF16)
    y = _dot(h, wcf_ref[...])
    cfpad_ref[cf_halo:cf_halo + SEG_ROW, :] = y[:, :CF_W] * _sigmoid(y[:, CF_W:])
    for j in range(3):
        cols = slice(j * GDN_QK, (j + 1) * GDN_QK)
        qpad_ref[q_halo:q_halo + SEG_ROW, cols] = _dot(h, wqkv_ref[:, cols])
    yz = _dot(h, wzbg_ref[...])

    for r0 in range(0, SEG_ROW, CONV_ROWS):
        acc = cfw_ref[0:1, :] * cfpad_ref[r0:r0 + CONV_ROWS, :]
        for k in range(1, CF_CONV):
            acc = acc + cfw_ref[k:k + 1, :] * cfpad_ref[r0 + k * SUBLANES:r0 + k * SUBLANES + CONV_ROWS, :]
        a_ref[r0:r0 + CONV_ROWS, :] = _silu(_layer_norm(acc, lng_ref[...], lnb_ref[...])).astype(BF16)

    _store_batch_major(z_ref, stage_ref.at[3], yz[:, :GDN_W], 0, SEG_ROW // BATCH)
    s = yz[:, GDN_W:]
    t = s + dtb_ref[...]
    softplus = jnp.maximum(t, 0.0) + jnp.log1p(jnp.exp(-jnp.abs(t)))
    lane = lax.broadcasted_iota(jnp.int32, s.shape, 1)
    bg_ref[...] = jnp.where(lane < 2 * GDN_HEADS, _sigmoid(s), -jnp.exp(alog_ref[...]) * softplus)

    blk = 2 * CONV_ROWS
    for j in range(3):
        cols = slice(j * GDN_QK, (j + 1) * GDN_QK)
        for r0 in range(0, SEG_ROW, blk):
            yq = (gcw_ref[0:1, cols] * qpad_ref[r0:r0 + blk, cols]
                  + gcw_ref[1:2, cols] * qpad_ref[r0 + q_halo:r0 + q_halo + blk, cols]
                  + gcw_ref[2:3, cols] * qpad_ref[r0 + 2 * q_halo:r0 + 2 * q_halo + blk, cols])
            yq = _silu(yq)
            if j < 2:
                scale = GDN_DK ** -0.5 if j == 0 else 1.0
                heads = []
                for hh in range(GDN_HEADS):
                    th = yq[:, hh * GDN_DK:(hh + 1) * GDN_DK]
                    heads.append(th * (lax.rsqrt(jnp.sum(th * th, axis=-1, keepdims=True) + RMS_EPS) * scale))
                yq = jnp.concatenate(heads, axis=-1)
            _store_batch_major(qkv_ref.at[:, :, cols], stage_ref.at[j], yq, r0 // SUBLANES, blk // SUBLANES, stage_row0=r0)


def _even_in(x, mod, w_cf, w_qkv, w_zbg, cf_w, ln_g, ln_b, gdn_w, alog, dtb):
    rows = x.shape[0]
    n_tok = SEG_ROW // BATCH
    return pl.pallas_call(
        _even_in_kernel,
        grid=(rows // SEG_ROW,),
        in_specs=[pl.BlockSpec((SEG_ROW, D_MODEL), lambda i: (i, 0)), mod.spec(SC1), mod.spec(SH1),
                  _const_spec((D_MODEL, 2 * CF_W)), _const_spec((D_MODEL, GDN_CONV_W)), _const_spec((D_MODEL, GDN_W + LANES)),
                  _const_spec((CF_CONV, CF_W)), _const_spec((1, CF_W)), _const_spec((1, CF_W)),
                  _const_spec((3, GDN_CONV_W)), _const_spec((1, LANES)), _const_spec((1, LANES))],
        out_specs=[pl.BlockSpec((SEG_ROW, CF_W), lambda i: (i, 0)),
                   pl.BlockSpec((BATCH, n_tok, GDN_CONV_W), lambda i: (0, i, 0)),
                   pl.BlockSpec((BATCH, n_tok, GDN_W), lambda i: (0, i, 0)),
                   pl.BlockSpec((SEG_ROW, LANES), lambda i: (i, 0))],
        out_shape=[jax.ShapeDtypeStruct((rows, CF_W), BF16),
                   jax.ShapeDtypeStruct((BATCH, rows // BATCH, GDN_CONV_W), F32),
                   jax.ShapeDtypeStruct((BATCH, rows // BATCH, GDN_W), F32),
                   jax.ShapeDtypeStruct((rows, LANES), F32)],
        scratch_shapes=[pltpu.VMEM((SEG_ROW + 2 * (CF_CONV // 2) * SUBLANES, CF_W), F32),
                        pltpu.VMEM((SEG_ROW + 2 * SUBLANES, GDN_CONV_W), F32),
                        pltpu.VMEM((4, GDN_QK // LANES, SEG_ROW, LANES), F32)],
        compiler_params=_params(1), name="even_in",
    )(x, mod.arr, mod.arr, w_cf, w_qkv, w_zbg, cf_w, ln_g, ln_b, gdn_w, alog, dtb)


def _zbg_proj(x, mod, w, alog, dtb, *, tm):
    rows = x.shape[0]
    return pl.pallas_call(
        _zbg_kernel,
        grid=(rows // tm,),
        in_specs=[pl.BlockSpec((tm, D_MODEL), lambda i: (i, 0)), mod.spec(SC1), mod.spec(SH1),
                  _const_spec((D_MODEL, GDN_W + LANES)), _const_spec((1, LANES)), _const_spec((1, LANES))],
        out_specs=[pl.BlockSpec((BATCH, tm // BATCH, GDN_W), lambda i: (0, i, 0)),
                   pl.BlockSpec((tm, LANES), lambda i: (i, 0))],
        out_shape=[jax.ShapeDtypeStruct((BATCH, rows // BATCH, GDN_W), F32), jax.ShapeDtypeStruct((rows, LANES), F32)],
        scratch_shapes=[pltpu.VMEM((GDN_W // LANES, PIECE, LANES), F32)],
        compiler_params=_params(1), name="zbg_proj",
    )(x, mod.arr, mod.arr, w, alog, dtb)


def _gdn_decays(beta_row, g_row, rev):
    c = GDN_CK
    ri = lax.broadcasted_iota(jnp.int32, (c, c), 0)
    ci = lax.broadcasted_iota(jnp.int32, (c, c), 1)
    eye = ri == ci
    incl = (ri <= ci) if rev else (ri >= ci)
    incl_t = (ci <= ri) if rev else (ci >= ri)
    strict = (ri < ci) if rev else (ri > ci)
    g_b = jnp.broadcast_to(g_row, (c, c))
    beta_b = jnp.broadcast_to(beta_row, (c, c))
    g_col = jnp.sum(jnp.where(eye, g_b, 0.0), axis=1, keepdims=True)
    beta_col = jnp.sum(jnp.where(eye, beta_b, 0.0), axis=1, keepdims=True)
    gc_col = jnp.sum(jnp.where(incl, g_b, 0.0), axis=1, keepdims=True)
    gc_row = jnp.sum(jnp.where(incl_t, g_col, 0.0), axis=0, keepdims=True)
    g_sum = jnp.sum(g_row, axis=1, keepdims=True)
    decay = jnp.exp(jnp.where(incl, gc_col - gc_row, -1e30))
    return dict(strict=strict, beta_col=beta_col, decay=decay, e_col=jnp.exp(gc_col),
                e_end=jnp.exp(g_sum - gc_col), g_tot=jnp.exp(g_sum))


def _gdn_chunk_terms(chains, side_tasks=()):
    side_tasks = list(side_tasks)

    def side():
        if side_tasks:
            side_tasks.pop(0)()

    c = GDN_CK
    ri = lax.broadcasted_iota(jnp.int32, (c, c), 0)
    ci = lax.broadcasted_iota(jnp.int32, (c, c), 1)
    same = lambda s: (ri // s) == (ci // s)
    nils = [jnp.where(ch["strict"], -(ch["beta_col"] * ch["kk"] * ch["decay"]), 0.0) for ch in chains]
    nds = [jnp.where(same(GDN_BASE), n, 0.0) for n in nils]
    invs = [jnp.where(ri == ci, 1.0, 0.0) + nd for nd in nds]
    p16s = [nd.astype(BF16) for nd in nds]
    side()
    for _ in range(int(math.log2(GDN_BASE)) - 1):
        p16s = [_dot(p16, p16).astype(BF16) for p16 in p16s]
        side()
        invs = [inv + _dot(p16, inv.astype(BF16)) for p16, inv in zip(p16s, invs)]
    s = GDN_BASE
    while s < c:
        off = jnp.logical_and(same(2 * s), jnp.logical_not(same(s)))
        inv16s = [inv.astype(BF16) for inv in invs]
        side()
        m16s = [_dot(inv16, jnp.where(off, n, 0.0).astype(BF16)).astype(BF16) for inv16, n in zip(inv16s, nils)]
        side()
        invs = [inv + _dot(m16, inv16) for inv, m16, inv16 in zip(invs, m16s, inv16s)]
        s *= 2
    side()
    y16s = [_dot(inv.astype(BF16),
                 jnp.concatenate([ch["v"] * ch["beta_col"], ch["k"] * (ch["beta_col"] * ch["e_col"])], axis=-1).astype(BF16)
                 ).astype(BF16) for inv, ch in zip(invs, chains)]
    side()
    kts = [_dot_tn((ch["k"] * ch["e_end"]).astype(BF16), y16) for ch, y16 in zip(chains, y16s)]
    qys = [_dot((ch["qk_raw"] * ch["decay"]).astype(BF16), y16) for ch, y16 in zip(chains, y16s)]
    while side_tasks:
        side()
    out = []
    for ch, kt, qy in zip(chains, kts, qys):
        q_eff = ch["q"] * ch["e_col"] - qy[:, GDN_DV:]
        out.append(((-kt[:, GDN_DV:]).astype(BF16), kt[:, :GDN_DV], q_eff.astype(BF16), qy[:, :GDN_DV],
                    jnp.broadcast_to(ch["g_tot"], (1, GDN_DV))))
    return out


def _gdn_kernel(qc, kc, vc, zc, ql, kl, vl, zl, bg, ng_ref, oc_ref, ol_ref,
                q_all, k_all, v_all, o_dir, a_ref, b_ref, qe_ref, oz_ref, gt_ref, s_ref):
    ncc = CTX_LEN // GDN_CK
    for src, dst in ((qc, q_all), (kc, k_all), (vc, v_all)):
        dst[0:CTX_LEN, :] = src[...]
    for src, dst in ((ql, q_all), (kl, k_all), (vl, v_all)):
        for r in range(SEQ // PIECE):
            dst[CTX_LEN + r * PIECE:CTX_LEN + (r + 1) * PIECE, :] = src[r * PIECE:(r + 1) * PIECE, :]

    def chunk_at(d, p):
        if d == 0:
            return p
        return jnp.where(p < ncc, ncc - 1 - p, GDN_NCHUNK + ncc - 1 - p)

    def rows_of(cc):
        if isinstance(cc, int):
            return pl.ds(cc * GDN_CK, GDN_CK)
        return pl.ds(pl.multiple_of(cc * GDN_CK, GDN_CK), GDN_CK)

    def prepare(g, side_tasks=()):
        chains = []
        for jj in range(GDN_GROUP):
            for d in range(2):
                cc = chunk_at(d, g * GDN_GROUP + jj)
                rows = rows_of(cc)
                qv, kv, vv = q_all[rows, :], k_all[rows, :], v_all[rows, :]
                k16 = kv.astype(BF16)
                kq = _dot_nt(jnp.concatenate([k16, qv.astype(BF16)], axis=0), k16)
                chains.append(dict(q=qv, k=kv, v=vv, kk=kq[:GDN_CK], qk_raw=kq[GDN_CK:],
                                   **_gdn_decays(bg[d, pl.ds(cc, 1), :], bg[2 + d, pl.ds(cc, 1), :], rev=(d == 1))))
        terms = _gdn_chunk_terms(chains, side_tasks)
        for n, (a_neg, b_mat, q_eff, o_zero, g_tot) in enumerate(terms):
            jj, d = divmod(n, 2)
            a_ref[d, jj] = a_neg
            b_ref[d, jj] = b_mat
            qe_ref[d, jj] = q_eff
            oz_ref[d, jj] = o_zero
            gt_ref[d, jj] = g_tot

    def advance_tasks(g):
        def step(jj):
            for d in range(2):
                rows = rows_of(chunk_at(d, g * GDN_GROUP + jj))
                s = s_ref[d]
                s16 = s.astype(BF16)
                o_dir[d, rows, :] = oz_ref[d, jj] + _dot(qe_ref[d, jj], s16)
                s_ref[d] = s * gt_ref[d, jj] + _dot(a_ref[d, jj], s16) + b_ref[d, jj]
        return [functools.partial(step, jj) for jj in range(GDN_GROUP)]

    s_ref[...] = jnp.zeros(s_ref.shape, F32)
    n_groups = GDN_NCHUNK // GDN_GROUP
    prepare(0)

    def body(g, carry):
        prepare(g, advance_tasks(g - 1))
        return carry
    lax.fori_loop(1, n_groups, body, 0)
    for task in advance_tasks(n_groups - 1):
        task()

    blk = 256
    for z, o, tok0 in ((zc, oc_ref, 0), (zl, ol_ref, CTX_LEN)):
        for r in range(z.shape[0] // blk):
            rows = slice(r * blk, (r + 1) * blk)
            src = slice(tok0 + r * blk, tok0 + (r + 1) * blk)
            ov = o_dir[0, src, :] + o_dir[1, src, :]
            ov = ov * lax.rsqrt(jnp.mean(ov * ov, axis=-1, keepdims=True) + RMS_EPS) * ng_ref[...]
            o[rows, :] = ov * _silu(z[rows, :])


def _gdn(qkv_c, z_c, qkv_l, z_l, bg, norm_g):
    def tok_spec(n_tok, col0):
        return pl.BlockSpec((None, n_tok, GDN_DK), lambda b, h: (b, 0, col0 + h))

    in_specs = []
    for n_tok in (CTX_LEN, SEQ):
        in_specs += [tok_spec(n_tok, 0), tok_spec(n_tok, GDN_HEADS), tok_spec(n_tok, 2 * GDN_HEADS), tok_spec(n_tok, 0)]
    in_specs += [pl.BlockSpec((None, None, 4, GDN_NCHUNK, GDN_CK), lambda b, h: (b, h, 0, 0, 0)),
                 _const_spec((1, GDN_DV))]
    n_tok = CTX_LEN + SEQ
    return pl.pallas_call(
        _gdn_kernel,
        grid=(BATCH, GDN_HEADS),
        in_specs=in_specs,
        out_specs=[tok_spec(CTX_LEN, 0), tok_spec(SEQ, 0)],
        out_shape=[jax.ShapeDtypeStruct((BATCH, CTX_LEN, GDN_W), F32),
                   jax.ShapeDtypeStruct((BATCH, SEQ, GDN_W), F32)],
        scratch_shapes=[pltpu.VMEM((n_tok, GDN_DK), F32), pltpu.VMEM((n_tok, GDN_DK), F32),
                        pltpu.VMEM((n_tok, GDN_DV), F32), pltpu.VMEM((2, n_tok, GDN_DV), F32),
                        pltpu.VMEM((2, GDN_GROUP, GDN_DK, GDN_DV), BF16),
                        pltpu.VMEM((2, GDN_GROUP, GDN_DK, GDN_DV), F32),
                        pltpu.VMEM((2, GDN_GROUP, GDN_CK, GDN_DK), BF16),
                        pltpu.VMEM((2, GDN_GROUP, GDN_CK, GDN_DV), F32),
                        pltpu.VMEM((2, GDN_GROUP, 1, GDN_DV), F32),
                        pltpu.VMEM((2, GDN_DK, GDN_DV), F32)],
        compiler_params=_params(2), name="gdn",
    )(qkv_c, qkv_c, qkv_c, z_c, qkv_l, qkv_l, qkv_l, z_l, bg, norm_g)


def _residual_norm(x, branch, gate, g, b):
    return _layer_norm(DN_ALPHA * x + _per_sample(branch, gate, lambda a, c: a * c), g, b)


def _out_even_kernel(x_ref, a_ref, o_ref, g1_ref, wa_ref, wo_ref, lng_ref, lnb_ref, out_ref, stage_ref):
    tm = x_ref.shape[0]
    for p in range(tm // PIECE):
        rows = slice(p * PIECE, (p + 1) * PIECE)
        o = _load_token_major(o_ref, stage_ref, p * (PIECE // BATCH), PIECE // BATCH)
        m = _dot(a_ref[rows, :], wa_ref[...]) + _dot(o.astype(BF16), wo_ref[...])
        out_ref[rows, :] = _residual_norm(x_ref[rows, :], m, g1_ref[...], lng_ref[...], lnb_ref[...])


def _out_even(x, a, o, mod, wa, wo, ln_g, ln_b, *, tm):
    rows = x.shape[0]
    row = lambda c: pl.BlockSpec((tm, c), lambda i: (i, 0))
    return pl.pallas_call(
        _out_even_kernel,
        grid=(rows // tm,),
        in_specs=[row(D_MODEL), row(CF_W), pl.BlockSpec((BATCH, tm // BATCH, GDN_W), lambda i: (0, i, 0)),
                  mod.spec(G1),
                  _const_spec((CF_W, D_MODEL)), _const_spec((GDN_W, D_MODEL)),
                  _const_spec((1, D_MODEL)), _const_spec((1, D_MODEL))],
        out_specs=row(D_MODEL),
        out_shape=jax.ShapeDtypeStruct((rows, D_MODEL), F32),
        scratch_shapes=[pltpu.VMEM((GDN_W // LANES, PIECE, LANES), F32)],
        compiler_params=_params(1), name="out_even",
    )(x, a, o, mod.arr, wa, wo, ln_g, ln_b)


def _gelu_tanh(x):
    return 0.5 * x * (1.0 + jnp.tanh(math.sqrt(2.0 / math.pi) * (x + 0.044715 * (x * x * x))))


def _out_odd_kernel(x_ref, u_ref, yf_ref, yb_ref, s_ref, g1_ref, d_ref, gw_ref, gb_ref, wa_ref, wo_ref,
                    lng_ref, lnb_ref, out_ref):
    tm = x_ref.shape[0]
    for p in range(tm // PIECE):
        rows = slice(p * PIECE, (p + 1) * PIECE)
        y = d_ref[...] * u_ref[rows, :] + yf_ref[rows, :] + yb_ref[rows, :]
        zg = _gelu_tanh(y)
        s5 = zg * _sigmoid(_dot(zg.astype(BF16), gw_ref[...]) + gb_ref[...])
        m = _dot(s5.astype(BF16), wa_ref[...]) + _dot(s_ref[rows, :], wo_ref[...])
        out_ref[rows, :] = _residual_norm(x_ref[rows, :], m, g1_ref[...], lng_ref[...], lnb_ref[...])


def _out_odd(x, u, yf, yb, s, mod, d_skip, glu_w, glu_b, wa, wo, ln_g, ln_b, *, tm):
    rows = x.shape[0]
    row = lambda c: pl.BlockSpec((tm, c), lambda i: (i, 0))
    return pl.pallas_call(
        _out_odd_kernel,
        grid=(rows // tm,),
        in_specs=[row(D_MODEL), row(S5_W), row(S5_W), row(S5_W), row(SC_W), mod.spec(G1),
                  _const_spec((1, S5_W)), _const_spec((S5_W, S5_W)), _const_spec((1, S5_W)),
                  _const_spec((S5_W, D_MODEL)), _const_spec((SC_W, D_MODEL)),
                  _const_spec((1, D_MODEL)), _const_spec((1, D_MODEL))],
        out_specs=row(D_MODEL),
        out_shape=jax.ShapeDtypeStruct((rows, D_MODEL), F32),
        compiler_params=_params(1), name="out_odd",
    )(x, u, yf, yb, s, mod.arr, d_skip, glu_w, glu_b, wa, wo, ln_g, ln_b)


def _conv3(pad_ref, s, r0, n, cw_ref, shift):
    return (cw_ref[0:1, :] * pad_ref[s, r0:r0 + n, :]
            + cw_ref[1:2, :] * pad_ref[s, r0 + shift:r0 + shift + n, :]
            + cw_ref[2:3, :] * pad_ref[s, r0 + 2 * shift:r0 + 2 * shift + n, :])


def _ffn_kernel(x_ref, sc_ref, sh_ref, g2_ref, wu_ref, cw_ref, wd_ref, lng_ref, lnb_ref, out_ref,
                h_ref, val_ref, pad_ref, *, tm, shift, seg):
    j = pl.program_id(1)
    n_pieces = tm // PIECE
    tf = val_ref.shape[1]

    @pl.when(j == 0)
    def _():
        zeros = jnp.zeros((shift, tf), F32)
        for s in range(tm // seg):
            pad_ref[s, 0:shift, :] = zeros
            pad_ref[s, shift + seg:shift + seg + shift, :] = zeros
        for p in range(n_pieces):
            x = _load_rows(x_ref, p * PIECE, PIECE)
            h_ref[p * PIECE:(p + 1) * PIECE, :] = _modulate(x, sc_ref[...], sh_ref[...]).astype(BF16)

    for p in range(n_pieces):
        h = h_ref[p * PIECE:(p + 1) * PIECE, :]
        up = _dot(h, wu_ref[...])
        val_ref[p * PIECE:(p + 1) * PIECE, :] = up[:, :tf]
        s, off = divmod(p * PIECE, seg)
        pad_ref[s, shift + off:shift + off + PIECE, :] = up[:, tf:]

    for p in range(n_pieces):
        s, off = divmod(p * PIECE, seg)
        gate = _conv3(pad_ref, s, off, PIECE, cw_ref, shift)
        act = (val_ref[p * PIECE:(p + 1) * PIECE, :] * _silu(gate)).astype(BF16)
        part = _dot(act, wd_ref[...])

        @pl.when(j == 0)
        def _():
            _store_rows(out_ref, p * PIECE, part)

        @pl.when(j > 0)
        def _():
            _store_rows(out_ref, p * PIECE, _load_rows(out_ref, p * PIECE, PIECE) + part)

    @pl.when(j == pl.num_programs(1) - 1)
    def _():
        for p in range(n_pieces):
            x = _load_rows(x_ref, p * PIECE, PIECE)
            y = _residual_norm(x, _load_rows(out_ref, p * PIECE, PIECE), g2_ref[...], lng_ref[...], lnb_ref[...])
            _store_rows(out_ref, p * PIECE, y)


def _ffn(x, mod, w_up, cw, w_down, ln_g, ln_b, *, tiling, shift, seg, tf):
    tm = tiling.tm
    n_f = FFN_F // tf
    out = pl.pallas_call(
        functools.partial(_ffn_kernel, tm=tm, shift=shift, seg=seg),
        grid=(tiling.n_tiles, n_f),
        in_specs=[tiling.spec(D_MODEL), mod.spec(SC2), mod.spec(SH2), mod.spec(G2),
                  pl.BlockSpec((D_MODEL, 2 * tf), lambda i, j: (0, j)),
                  pl.BlockSpec((3, tf), lambda i, j: (0, j)),
                  pl.BlockSpec((tf, D_MODEL), lambda i, j: (j, 0)),
                  _const_spec((1, D_MODEL)), _const_spec((1, D_MODEL))],
        out_specs=tiling.spec(D_MODEL),
        out_shape=jax.ShapeDtypeStruct(tiling.shape(D_MODEL), F32),
        scratch_shapes=[pltpu.VMEM((tm, D_MODEL), BF16), pltpu.VMEM((tm, tf), F32),
                        pltpu.VMEM((tm // seg, seg + 2 * shift, tf), F32)],
        compiler_params=_params(2), name="conv_ffn",
    )(tiling.view(x), mod.arr, mod.arr, mod.arr, w_up, cw, w_down, ln_g, ln_b)
    return tiling.unview(out)


def _ffn_resident_kernel(x_ref, sc_ref, sh_ref, g2_ref, wu_ref, cw_ref, wd_ref, lng_ref, lnb_ref, out_ref,
                         h_ref, val_ref, pad_ref, *, tm, shift, seg, tf):
    n_pieces = tm // PIECE
    zeros = jnp.zeros((shift, tf), F32)
    for s in range(tm // seg):
        pad_ref[s, 0:shift, :] = zeros
        pad_ref[s, shift + seg:shift + seg + shift, :] = zeros
    for p in range(n_pieces):
        x = _load_rows(x_ref, p * PIECE, PIECE)
        h_ref[p * PIECE:(p + 1) * PIECE, :] = _modulate(x, sc_ref[...], sh_ref[...]).astype(BF16)

    for j in range(FFN_F // tf):
        for p in range(n_pieces):
            up = _dot(h_ref[p * PIECE:(p + 1) * PIECE, :], wu_ref[:, 2 * j * tf:2 * (j + 1) * tf])
            val_ref[p * PIECE:(p + 1) * PIECE, :] = up[:, :tf]
            s, off = divmod(p * PIECE, seg)
            pad_ref[s, shift + off:shift + off + PIECE, :] = up[:, tf:]
        for p in range(n_pieces):
            s, off = divmod(p * PIECE, seg)
            gate = _conv3(pad_ref, s, off, PIECE, cw_ref.at[:, j * tf:(j + 1) * tf], shift)
            act = (val_ref[p * PIECE:(p + 1) * PIECE, :] * _silu(gate)).astype(BF16)
            part = _dot(act, wd_ref[j * tf:(j + 1) * tf, :])
            if j > 0:
                part = _load_rows(out_ref, p * PIECE, PIECE) + part
            _store_rows(out_ref, p * PIECE, part)

    for p in range(n_pieces):
        x = _load_rows(x_ref, p * PIECE, PIECE)
        y = _residual_norm(x, _load_rows(out_ref, p * PIECE, PIECE), g2_ref[...], lng_ref[...], lnb_ref[...])
        _store_rows(out_ref, p * PIECE, y)


def _ffn_resident(x, mod, w_up, cw, w_down, ln_g, ln_b, *, tiling, shift, seg, tf):
    tm = tiling.tm
    resident = lambda shape: pl.BlockSpec(shape, lambda i: (0, 0), pipeline_mode=pl.Buffered(1))
    out = pl.pallas_call(
        functools.partial(_ffn_resident_kernel, tm=tm, shift=shift, seg=seg, tf=tf),
        grid=(tiling.n_tiles,),
        in_specs=[tiling.spec(D_MODEL), mod.spec(SC2), mod.spec(SH2), mod.spec(G2),
                  resident((D_MODEL, 2 * FFN_F)), _const_spec((3, FFN_F)), resident((FFN_F, D_MODEL)),
                  _const_spec((1, D_MODEL)), _const_spec((1, D_MODEL))],
        out_specs=tiling.spec(D_MODEL),
        out_shape=jax.ShapeDtypeStruct(tiling.shape(D_MODEL), F32),
        scratch_shapes=[pltpu.VMEM((tm, D_MODEL), BF16), pltpu.VMEM((tm, tf), F32),
                        pltpu.VMEM((tm // seg, seg + 2 * shift, tf), F32)],
        compiler_params=_params(1), name="conv_ffn_resident",
    )(tiling.view(x), mod.arr, mod.arr, mod.arr, w_up, cw, w_down, ln_g, ln_b)
    return tiling.unview(out)


def _odd_in_kernel(x_ref, sc_ref, sh_ref, wu_ref, wb_ref, wc_ref, wx_ref, cw_ref, u_ref, s_ref,
                   h_ref, b_ref, pad_ref, *, tm, shift, seg):
    j = pl.program_id(1)
    n_pieces = tm // PIECE
    cb = u_ref.shape[-1]

    @pl.when(j == 0)
    def _():
        zeros = jnp.zeros((shift, cb), F32)
        for s in range(tm // seg):
            pad_ref[s, 0:shift, :] = zeros
            pad_ref[s, shift + seg:shift + seg + shift, :] = zeros
        for p in range(n_pieces):
            x = _load_rows(x_ref, p * PIECE, PIECE)
            h_ref[p * PIECE:(p + 1) * PIECE, :] = _modulate(x, sc_ref[...], sh_ref[...]).astype(BF16)

    for p in range(n_pieces):
        h = h_ref[p * PIECE:(p + 1) * PIECE, :]
        _store_rows(u_ref, p * PIECE, _dot(h, wu_ref[...]))
        b_ref[p * PIECE:(p + 1) * PIECE, :] = _dot(h, wb_ref[...])
        s, off = divmod(p * PIECE, seg)
        pad_ref[s, shift + off:shift + off + PIECE, :] = _dot(h, wc_ref[...]) * _dot(h, wx_ref[...])

    for p in range(n_pieces):
        s, off = divmod(p * PIECE, seg)
        gated = b_ref[p * PIECE:(p + 1) * PIECE, :] * _conv3(pad_ref, s, off, PIECE, cw_ref, shift)
        _store_rows(s_ref, p * PIECE, gated.astype(BF16))


def _odd_in(x, mod, w_in, cw, *, tiling, shift, seg):
    tm = tiling.tm
    cb = 2 * LANES
    nb = S5_W // cb
    wspec = lambda k: pl.BlockSpec((D_MODEL, cb), lambda i, j: (0, k * nb + j))
    u, s = pl.pallas_call(
        functools.partial(_odd_in_kernel, tm=tm, shift=shift, seg=seg),
        grid=(tiling.n_tiles, nb),
        in_specs=[tiling.spec(D_MODEL), mod.spec(SC1), mod.spec(SH1),
                  wspec(0), wspec(1), wspec(2), wspec(3),
                  pl.BlockSpec((3, cb), lambda i, j: (0, j))],
        out_specs=[tiling.spec(cb, lambda i, j: j), tiling.spec(cb, lambda i, j: j)],
        out_shape=[jax.ShapeDtypeStruct(tiling.shape(S5_W), F32), jax.ShapeDtypeStruct(tiling.shape(SC_W), BF16)],
        scratch_shapes=[pltpu.VMEM((tm, D_MODEL), BF16), pltpu.VMEM((tm, cb), F32),
                        pltpu.VMEM((tm // seg, seg + 2 * shift, cb), F32)],
        compiler_params=_params(2), name="odd_in",
    )(tiling.view(x), mod.arr, mod.arr, w_in, w_in, w_in, w_in, cw)
    return tiling.unview(u), tiling.unview(s)


def _s5_kernel(uf_ref, ub_ref, wb_ref, wc_ref, a_ref, x0_ref, yf_ref, yb_ref, xfin_ref, bu_ref, st_ref):
    t = pl.program_id(0)
    rows = uf_ref.shape[0]
    n_tok = rows // SUBLANES
    hs = S5_HSTATE

    @pl.when(t == 0)
    def _():
        st_ref[...] = x0_ref[...]

    for d, u_ref in enumerate((uf_ref, ub_ref)):
        for hf in range(2):
            u = u_ref[:, hf * S5_HALF:(hf + 1) * S5_HALF].astype(BF16)
            bu_ref[d, :, hf * 2 * hs:(hf + 1) * 2 * hs] = _dot(u, wb_ref[d, hf])

    for d, y_ref in enumerate((yf_ref, yb_ref)):
        for hf in range(2):
            c_re = hf * 2 * hs
            c_im = c_re + hs
            a_re = jnp.broadcast_to(a_ref[d, hf, 0:1, :], (SUBLANES, hs))
            a_im = jnp.broadcast_to(a_ref[d, hf, 1:2, :], (SUBLANES, hs))
            xr, xi = st_ref[d, :, c_re:c_re + hs], st_ref[d, :, c_im:c_im + hs]
            for i in range(n_tok):
                tok = i if d == 0 else n_tok - 1 - i
                r = slice(tok * SUBLANES, (tok + 1) * SUBLANES)
                xr, xi = (a_re * xr - a_im * xi + bu_ref[d, r, c_re:c_re + hs],
                          a_re * xi + a_im * xr + bu_ref[d, r, c_im:c_im + hs])
                bu_ref[d, r, c_re:c_re + hs] = xr
                bu_ref[d, r, c_im:c_im + hs] = xi
            st_ref[d, :, c_re:c_re + hs] = xr
            st_ref[d, :, c_im:c_im + hs] = xi
        for hf in range(2):
            xs = bu_ref[d, :, hf * 2 * hs:(hf + 1) * 2 * hs].astype(BF16)
            y_ref[:, hf * S5_HALF:(hf + 1) * S5_HALF] = _dot(xs, wc_ref[d, hf])

    @pl.when(t == pl.num_programs(0) - 1)
    def _():
        xfin_ref[...] = st_ref[...]


def _s5(u, wb, wc, a, x0):
    rows = u.shape[0]
    tr = S5_TOK * SUBLANES
    nt = rows // tr
    state = (2, SUBLANES, 4 * S5_HSTATE)
    return pl.pallas_call(
        _s5_kernel,
        grid=(nt,),
        in_specs=[pl.BlockSpec((tr, S5_W), lambda t: (t, 0)),
                  pl.BlockSpec((tr, S5_W), lambda t: (nt - 1 - t, 0)),
                  _const_spec((2, 2, S5_HALF, 2 * S5_HSTATE)), _const_spec((2, 2, 2 * S5_HSTATE, S5_HALF)),
                  _const_spec((2, 2, 2, S5_HSTATE)), _const_spec(state)],
        out_specs=[pl.BlockSpec((tr, S5_W), lambda t: (t, 0)),
                   pl.BlockSpec((tr, S5_W), lambda t: (nt - 1 - t, 0)),
                   _const_spec(state)],
        out_shape=[jax.ShapeDtypeStruct((rows, S5_W), F32), jax.ShapeDtypeStruct((rows, S5_W), F32),
                   jax.ShapeDtypeStruct(state, F32)],
        scratch_shapes=[pltpu.VMEM((2, tr, 4 * S5_HSTATE), F32), pltpu.VMEM(state, F32)],
        compiler_params=_params(1), name="s5_scan",
    )(u, u, wb, wc, a, x0)


def _s5_weights(lam_re, lam_im, log_dt, b_re, b_im, c_re, c_im):
    eye = jnp.eye(S5_G // 2, dtype=F32)
    wbs, wcs, avs = [], [], []
    for di in range(2):
        lr, li = lam_re[di], lam_im[di]
        dt = jnp.exp(log_dt[di])[:, None]
        mag = jnp.exp(lr * dt)
        ar, ai = mag * jnp.cos(li * dt), mag * jnp.sin(li * dt)
        den = lr * lr + li * li
        fr = ((ar - 1.0) * lr + ai * li) / den
        fi = (ai * lr - (ar - 1.0) * li) / den
        bbr = fr[..., None] * b_re - fi[..., None] * b_im
        bbi = fr[..., None] * b_im + fi[..., None] * b_re

        def in_block(t):
            t = t.reshape(2, S5_G // 2, S5_N, S5_P)
            return jnp.einsum("hgnp,gk->hgpkn", t, eye).reshape(2, S5_HALF, S5_HSTATE)

        def out_block(t):
            t = t.reshape(2, S5_G // 2, S5_P, S5_N)
            return jnp.einsum("hgpn,gk->hgnkp", t, eye).reshape(2, S5_HSTATE, S5_HALF)

        wbs.append(jnp.concatenate([in_block(bbr), in_block(bbi)], axis=-1))
        wcs.append(jnp.concatenate([out_block(c_re), out_block(-c_im)], axis=1))
        avs.append(jnp.stack([ar.reshape(2, S5_HSTATE), ai.reshape(2, S5_HSTATE)], axis=1))
    return jnp.stack(wbs).astype(BF16), jnp.stack(wcs).astype(BF16), jnp.stack(avs)


def _bg_rows(bg, n_tok):
    t = bg.reshape(n_tok, BATCH, LANES)[:, :, :4 * GDN_HEADS].reshape(n_tok, BATCH, 4, GDN_HEADS)
    return t.transpose(1, 3, 2, 0).reshape(BATCH, GDN_HEADS, 4, n_tok // GDN_CK, GDN_CK)


def _to_token_major(a):
    return a.transpose(1, 0, 2).reshape(a.shape[0] * a.shape[1], a.shape[2])


def kernel(x, c, ctx, c_ctx, ada_w, ada_b, ln_g, ln_b, ev_w_in, ev_w_out, cf_conv, cf_ln_g, cf_ln_b, gdn_conv,
           gdn_a_log, gdn_dt_bias, gdn_norm_g, od_w_in, od_w_out, s5_lam_re, s5_lam_im, s5_log_dt, s5_b_re,
           s5_b_im, s5_c_re, s5_c_im, s5_d, s5_glu_w, s5_glu_b, sc_conv, ffn_w_up, ffn_conv, ffn_w_down):
    xl = _to_token_major(x)
    cl = _to_token_major(ctx)

    cvec = jnp.concatenate([c, jnp.broadcast_to(c_ctx[None, :], (SUBLANES, D_MODEL))], axis=0)
    mods = _ada(cvec, ada_w, ada_b).reshape(DEPTH, 2, SUBLANES, 6 * D_MODEL)

    lat_row = _Tiling("row", ROWS_LAT, 2 * SEG_ROW)
    lat_col = _Tiling("col", ROWS_LAT, 2 * SEG_ROW)
    ctx_row = _Tiling("row", ROWS_CTX, ROWS_CTX)
    lat_along_row = dict(tiling=lat_row, shift=SUBLANES, seg=SEG_ROW)
    lat_along_col = dict(tiling=lat_col, shift=lat_col.rg, seg=lat_col.tm)
    ctx_conv = dict(tiling=ctx_row, shift=SUBLANES, seg=ROWS_CTX)

    row2 = lambda v: v.reshape(1, -1)
    for i in range(DEPTH):
        j = i // 2
        last = i == DEPTH - 1
        m_lat, m_ctx = _Mod(mods, i, 0), _Mod(mods, i, 1)
        lng1, lnb1, lng2, lnb2 = row2(ln_g[i, 0]), row2(ln_b[i, 0]), row2(ln_g[i, 1]), row2(ln_b[i, 1])
        w_down, f_cw = ffn_w_down[i].astype(BF16), ffn_conv[i]

        def up_blocks(tf, w_up=ffn_w_up[i]):
            cols = [w_up[:, k * FFN_F + j * tf:k * FFN_F + (j + 1) * tf] for j in range(FFN_F // tf) for k in range(2)]
            return jnp.concatenate(cols, axis=1).astype(BF16)

        if i % 2 == 0:
            w_in = ev_w_in[j]
            w_cf = w_in[:, :2 * CF_W].astype(BF16)
            w_qkv = w_in[:, 2 * CF_W:2 * CF_W + GDN_CONV_W].astype(BF16)
            w_zbg = jnp.pad(w_in[:, 2 * CF_W + GDN_CONV_W:], ((0, 0), (0, LANES - 4 * GDN_HEADS))).astype(BF16)
            pad16 = lambda v: jnp.pad(v.reshape(1, -1), ((0, 0), (2 * GDN_HEADS, LANES - 4 * GDN_HEADS)))
            alog, dtb = pad16(gdn_a_log[j]), pad16(gdn_dt_bias[j])
            w_out = ev_w_out[j].astype(BF16)
            cf_ln = (row2(cf_ln_g[j]), row2(cf_ln_b[j]))
            a_c = _cf_mixer(cl, m_ctx, w_cf, cf_conv[j], *cf_ln, tm=ROWS_CTX, seg=ROWS_CTX)
            qkv_c = _qkv_proj(cl, m_ctx, w_qkv, gdn_conv[j], tm=ROWS_CTX, seg=ROWS_CTX)
            z_c, bg_c = _zbg_proj(cl, m_ctx, w_zbg, alog, dtb, tm=ROWS_CTX)
            a_l, qkv_l, z_l, bg_l = _even_in(xl, m_lat, w_cf, w_qkv, w_zbg, cf_conv[j], *cf_ln, gdn_conv[j], alog, dtb)
            bg = jnp.concatenate([_bg_rows(bg_c, CTX_LEN), _bg_rows(bg_l, SEQ)], axis=3)
            o_c, o_l = _gdn(qkv_c, z_c, qkv_l, z_l, bg, row2(gdn_norm_g[j]))
            x1 = _out_even(xl, a_l, o_l, m_lat, w_out[:CF_W], w_out[CF_W:], lng1, lnb1, tm=2 * SEG_ROW)
            if not last:
                c1 = _out_even(cl, a_c, o_c, m_ctx, w_out[:CF_W], w_out[CF_W:], lng1, lnb1, tm=2 * SEG_ROW)
            lat_ffn = lat_along_col
        else:
            w_in = od_w_in[j].astype(BF16)
            w_out = od_w_out[j].astype(BF16)
            wb, wc, av = _s5_weights(s5_lam_re[j], s5_lam_im[j], s5_log_dt[j], s5_b_re[j], s5_b_im[j],
                                     s5_c_re[j], s5_c_im[j])
            u_c, s_c = _odd_in(cl, m_ctx, w_in, sc_conv[j], **ctx_conv)
            u_l, s_l = _odd_in(xl, m_lat, w_in, sc_conv[j], **lat_along_col)
            zero_state = jnp.zeros((2, SUBLANES, 4 * S5_HSTATE), F32)
            yf_c, yb_c, fin_c = _s5(u_c, wb, wc, av, zero_state)
            yf_l, yb_l, _ = _s5(u_l, wb, wc, av, fin_c)
            odd_w = (row2(s5_d[j]), s5_glu_w[j].astype(BF16), row2(s5_glu_b[j]), w_out[:S5_W], w_out[S5_W:])
            x1 = _out_odd(xl, u_l, yf_l, yb_l, s_l, m_lat, *odd_w, lng1, lnb1, tm=2 * SEG_ROW)
            if not last:
                c1 = _out_odd(cl, u_c, yf_c, yb_c, s_c, m_ctx, *odd_w, lng1, lnb1, tm=2 * SEG_ROW)
            lat_ffn = lat_along_row

        xl = _ffn_resident(x1, m_lat, up_blocks(FFN_TF_LAT), f_cw, w_down, lng2, lnb2,
                           tf=FFN_TF_LAT, **lat_ffn)
        if not last:
            cl = _ffn(c1, m_ctx, up_blocks(FFN_TF_CTX), f_cw, w_down, lng2, lnb2,
                      tf=FFN_TF_CTX, **ctx_conv)

    return xl.reshape(SEQ, BATCH, D_MODEL).transpose(1, 0, 2)
```

```python
import functools
import math

import jax
import jax.numpy as jnp
from jax import lax
from jax.experimental import pallas as pl
from jax.experimental.pallas import tpu as pltpu

D_MODEL = 1024
BATCH = 8
SEQ = 2048
DEPTH = 4
GRID_W = 64
GRID_H = SEQ // GRID_W
CTX_LEN = 256
CF_W = 512
CF_CONV = 31
GDN_HEADS = 4
GDN_DK = 128
GDN_DV = 128
GDN_QK = GDN_HEADS * GDN_DK
GDN_W = GDN_HEADS * GDN_DV
GDN_CONV_W = 2 * GDN_QK + GDN_W
S5_W = 512
S5_P = 16
S5_G = S5_W // S5_P
S5_N = 64
SC_W = 512
FFN_F = 2816
DN_ALPHA = (2 * DEPTH) ** 0.25
LN_EPS = 1e-5
RMS_EPS = 1e-6

SUBLANES = 8
LANES = 128
VMEM_LIMIT_BYTES = 56 * 1024 * 1024

ROWS_LAT = SEQ * BATCH
ROWS_CTX = CTX_LEN * BATCH
SEG_ROW = GRID_W * BATCH
PIECE = 512
CONV_ROWS = 64
FFN_TF_LAT = FFN_F // 2
FFN_TF_CTX = 256
S5_HALF = S5_W // 2
S5_HSTATE = (S5_G // 2) * S5_N
S5_TOK = 64
GDN_CK = 64
GDN_BASE = 16
GDN_NCHUNK = (CTX_LEN + SEQ) // GDN_CK
GDN_GROUP = 9

F32 = jnp.float32
BF16 = jnp.bfloat16

assert BATCH == SUBLANES


def _dot(a, b):
    return jnp.dot(a, b, preferred_element_type=F32)


def _dot_nt(a, b):
    return lax.dot_general(a, b, (((1,), (1,)), ((), ())), preferred_element_type=F32)


def _dot_tn(a, b):
    return lax.dot_general(a, b, (((0,), (0,)), ((), ())), preferred_element_type=F32)


def _sigmoid(x):
    return 1.0 / (1.0 + jnp.exp(-x))


def _silu(x):
    return x * _sigmoid(x)


def _per_sample(x, vec, op):
    r, c = x.shape
    x3 = x.reshape(r // SUBLANES, SUBLANES, c)
    return op(x3, vec[None]).reshape(r, c)


def _modulate(x, scale, shift):
    y = _per_sample(x, 1.0 + scale, lambda a, b: a * b)
    return _per_sample(y, shift, lambda a, b: a + b)


def _layer_norm(x, g, b):
    mu = jnp.mean(x, axis=-1, keepdims=True)
    xc = x - mu
    var = jnp.mean(xc * xc, axis=-1, keepdims=True)
    return xc * lax.rsqrt(var + LN_EPS) * g + b


def _load_rows(ref, start, n):
    if len(ref.shape) == 2:
        return ref[start:start + n, :]
    rg = ref.shape[1]
    return ref[start // rg:(start + n) // rg, :, :].reshape(n, ref.shape[2])


def _store_rows(ref, start, value):
    n = value.shape[0]
    if len(ref.shape) == 2:
        ref[start:start + n, :] = value
    else:
        rg = ref.shape[1]
        ref[start // rg:(start + n) // rg, :, :] = value.reshape(n // rg, rg, value.shape[1])


def _params(n_axes):
    return pltpu.CompilerParams(dimension_semantics=("arbitrary",) * n_axes,
                                vmem_limit_bytes=VMEM_LIMIT_BYTES)


class _Tiling:
    def __init__(self, kind, rows, tm):
        self.kind, self.rows, self.tm = kind, rows, tm
        self.n_tiles = rows // tm
        if kind == "col":
            self.rg = tm // GRID_H

    def view(self, a):
        if self.kind == "row":
            return a
        return a.reshape(GRID_H, SEG_ROW, a.shape[-1])

    def unview(self, a):
        return a.reshape(self.rows, a.shape[-1])

    def shape(self, c):
        return (self.rows, c) if self.kind == "row" else (GRID_H, SEG_ROW, c)

    def spec(self, c, col=None):
        col = col or (lambda *ij: 0)
        if self.kind == "row":
            return pl.BlockSpec((self.tm, c), lambda *ij: (ij[0], col(*ij)))
        return pl.BlockSpec((GRID_H, self.rg, c), lambda *ij: (0, ij[0], col(*ij)))


def _const_spec(shape):
    nd = len(shape)
    return pl.BlockSpec(shape, lambda *ij: (0,) * nd)


SH1, SC1, G1, SH2, SC2, G2 = range(6)


class _Mod:
    def __init__(self, arr, layer, stream):
        self.arr, self.layer, self.stream = arr, layer, stream

    def spec(self, col):
        layer, stream = self.layer, self.stream
        return pl.BlockSpec((None, None, SUBLANES, D_MODEL), lambda *ij: (layer, stream, 0, col))


def _ada_kernel(c_ref, w_ref, b_ref, o_ref):
    s = _silu(c_ref[...]).astype(BF16)
    o_ref[...] = _dot(s, w_ref[...].astype(BF16)) + b_ref[...]


def _ada(cvec, ada_w, ada_b):
    return pl.pallas_call(
        _ada_kernel,
        grid=(DEPTH, 6),
        in_specs=[_const_spec((2 * SUBLANES, D_MODEL)),
                  pl.BlockSpec((None, D_MODEL, D_MODEL), lambda i, j: (i, 0, j)),
                  pl.BlockSpec((None, 1, D_MODEL), lambda i, j: (i, 0, j))],
        out_specs=pl.BlockSpec((None, 2 * SUBLANES, D_MODEL), lambda i, j: (i, 0, j)),
        out_shape=jax.ShapeDtypeStruct((DEPTH, 2 * SUBLANES, 6 * D_MODEL), F32),
        compiler_params=_params(2), name="ada",
    )(cvec, ada_w, ada_b.reshape(DEPTH, 1, 6 * D_MODEL))


def _cf_kernel(x_ref, sc_ref, sh_ref, w_ref, cw_ref, g_ref, b_ref, o_ref, pad_ref, *, seg):
    tm = x_ref.shape[0]
    halo = (CF_CONV // 2) * SUBLANES
    zeros = jnp.zeros((halo, CF_W), F32)
    for s in range(tm // seg):
        pad_ref[s, 0:halo, :] = zeros
        pad_ref[s, halo + seg:halo + seg + halo, :] = zeros
    def project(p):
        h = _modulate(x_ref[p * PIECE:(p + 1) * PIECE, :], sc_ref[...], sh_ref[...]).astype(BF16)
        y = _dot(h, w_ref[...])
        glu = y[:, :CF_W] * _sigmoid(y[:, CF_W:])
        s, off = divmod(p * PIECE, seg)
        pad_ref[s, halo + off:halo + off + PIECE, :] = glu

    for p in range(tm // PIECE):
        project(p)
    for s in range(tm // seg):
        def body(i, carry, s=s):
            for r0 in (pl.multiple_of(i * 2 * CONV_ROWS, CONV_ROWS), pl.multiple_of((i * 2 + 1) * CONV_ROWS, CONV_ROWS)):
                acc = cw_ref[0:1, :] * pad_ref[s, pl.ds(r0, CONV_ROWS), :]
                for k in range(1, CF_CONV):
                    acc = acc + cw_ref[k:k + 1, :] * pad_ref[s, pl.ds(r0 + k * SUBLANES, CONV_ROWS), :]
                o_ref[pl.ds(s * seg + r0, CONV_ROWS), :] = _silu(_layer_norm(acc, g_ref[...], b_ref[...])).astype(BF16)
            return carry
        lax.fori_loop(0, seg // (2 * CONV_ROWS), body, 0)


def _cf_mixer(x, mod, w, cw, ln_g, ln_b, *, tm, seg):
    rows = x.shape[0]
    halo = (CF_CONV // 2) * SUBLANES
    return pl.pallas_call(
        functools.partial(_cf_kernel, seg=seg),
        grid=(rows // tm,),
        in_specs=[pl.BlockSpec((tm, D_MODEL), lambda i: (i, 0)), mod.spec(SC1), mod.spec(SH1),
                  _const_spec((D_MODEL, 2 * CF_W)), _const_spec((CF_CONV, CF_W)),
                  _const_spec((1, CF_W)), _const_spec((1, CF_W))],
        out_specs=pl.BlockSpec((tm, CF_W), lambda i: (i, 0)),
        out_shape=jax.ShapeDtypeStruct((rows, CF_W), BF16),
        scratch_shapes=[pltpu.VMEM((tm // seg, seg + 2 * halo, CF_W), F32)],
        compiler_params=_params(1), name="cf_mixer",
    )(x, mod.arr, mod.arr, w, cw, ln_g, ln_b)


def _store_batch_major(dst_ref, stage_ref, value, tok0, n_tok, stage_row0=0):
    for c in range(stage_ref.shape[0]):
        lanes = slice(c * LANES, (c + 1) * LANES)
        stage_ref[c, stage_row0:stage_row0 + n_tok * BATCH, :] = value[:, lanes]
        for b in range(BATCH):
            dst_ref[b, tok0:tok0 + n_tok, lanes] = stage_ref[c, pl.ds(stage_row0 + b, n_tok, stride=BATCH), :]


def _load_token_major(src_ref, stage_ref, tok0, n_tok):
    cols = []
    for c in range(stage_ref.shape[0]):
        lanes = slice(c * LANES, (c + 1) * LANES)
        for b in range(BATCH):
            stage_ref[c, pl.ds(b, n_tok, stride=BATCH), :] = src_ref[b, pl.ds(tok0, n_tok), lanes]
        cols.append(stage_ref[c])
    return jnp.concatenate(cols, axis=-1)


def _qkv_kernel(x_ref, sc_ref, sh_ref, w_ref, cw_ref, o_ref, h_ref, pad_ref, stage_ref, *, seg):
    j = pl.program_id(1)
    tm = x_ref.shape[0]
    halo = SUBLANES

    @pl.when(j == 0)
    def _():
        zeros = jnp.zeros((halo, GDN_QK), F32)
        for s in range(tm // seg):
            pad_ref[s, 0:halo, :] = zeros
            pad_ref[s, halo + seg:halo + seg + halo, :] = zeros
        for p in range(tm // PIECE):
            rows = slice(p * PIECE, (p + 1) * PIECE)
            h_ref[rows, :] = _modulate(x_ref[rows, :], sc_ref[...], sh_ref[...]).astype(BF16)

    is_qk = j < 2
    scale = jnp.where(j == 0, GDN_DK ** -0.5, 1.0).astype(F32)
    n_pieces = tm // PIECE
    blk = 2 * CONV_ROWS

    def project(p):
        s, off = divmod(p * PIECE, seg)
        pad_ref[s, halo + off:halo + off + PIECE, :] = _dot(h_ref[p * PIECE:(p + 1) * PIECE, :], w_ref[...])

    def finish(p):
        s, off = divmod(p * PIECE, seg)
        for r0 in range(off, off + PIECE, blk):
            y = (cw_ref[0:1, :] * pad_ref[s, r0:r0 + blk, :]
                 + cw_ref[1:2, :] * pad_ref[s, r0 + halo:r0 + halo + blk, :]
                 + cw_ref[2:3, :] * pad_ref[s, r0 + 2 * halo:r0 + 2 * halo + blk, :])
            y = _silu(y)
            heads = []
            for hh in range(GDN_HEADS):
                t = y[:, hh * GDN_DK:(hh + 1) * GDN_DK]
                nrm = t * (lax.rsqrt(jnp.sum(t * t, axis=-1, keepdims=True) + RMS_EPS) * scale)
                heads.append(jnp.where(is_qk, nrm, t))
            row0 = s * seg + r0
            _store_batch_major(o_ref, stage_ref, jnp.concatenate(heads, axis=-1), row0 // SUBLANES, blk // SUBLANES,
                               stage_row0=row0)

    project(0)
    for p in range(n_pieces):
        if p + 1 < n_pieces:
            project(p + 1)
        finish(p)


def _qkv_proj(x, mod, w, cw, *, tm, seg):
    rows = x.shape[0]
    return pl.pallas_call(
        functools.partial(_qkv_kernel, seg=seg),
        grid=(rows // tm, 3),
        in_specs=[pl.BlockSpec((tm, D_MODEL), lambda i, j: (i, 0)), mod.spec(SC1), mod.spec(SH1),
                  pl.BlockSpec((D_MODEL, GDN_QK), lambda i, j: (0, j)),
                  pl.BlockSpec((3, GDN_QK), lambda i, j: (0, j))],
        out_specs=pl.BlockSpec((BATCH, tm // BATCH, GDN_QK), lambda i, j: (0, i, j)),
        out_shape=jax.ShapeDtypeStruct((BATCH, rows // BATCH, GDN_CONV_W), F32),
        scratch_shapes=[pltpu.VMEM((tm, D_MODEL), BF16),
                        pltpu.VMEM((tm // seg, seg + 2 * SUBLANES, GDN_QK), F32),
                        pltpu.VMEM((GDN_QK // LANES, tm, LANES), F32)],
        compiler_params=_params(2), name="qkv_proj",
    )(x, mod.arr, mod.arr, w, cw)


def _zbg_kernel(x_ref, sc_ref, sh_ref, w_ref, alog_ref, dtb_ref, z_ref, bg_ref, stage_ref):
    tm = x_ref.shape[0]
    for p in range(tm // PIECE):
        rows = slice(p * PIECE, (p + 1) * PIECE)
        h = _modulate(x_ref[rows, :], sc_ref[...], sh_ref[...]).astype(BF16)
        y = _dot(h, w_ref[...])
        _store_batch_major(z_ref, stage_ref, y[:, :GDN_W], p * (PIECE // BATCH), PIECE // BATCH)
        s = y[:, GDN_W:]
        t = s + dtb_ref[...]
        softplus = jnp.maximum(t, 0.0) + jnp.log1p(jnp.exp(-jnp.abs(t)))
        g = -jnp.exp(alog_ref[...]) * softplus
        lane = lax.broadcasted_iota(jnp.int32, s.shape, 1)
        bg_ref[rows, :] = jnp.where(lane < 2 * GDN_HEADS, _sigmoid(s), g)


def _even_in_kernel(x_ref, sc_ref, sh_ref, wcf_ref, wqkv_ref, wzbg_ref, cfw_ref, lng_ref, lnb_ref, gcw_ref,
                    alog_ref, dtb_ref, a_ref, qkv_ref, z_ref, bg_ref, cfpad_ref, qpad_ref, stage_ref, *xstage):
    cf_halo = (CF_CONV // 2) * SUBLANES
    q_halo = SUBLANES
    cfpad_ref[0:cf_halo, :] = jnp.zeros((cf_halo, CF_W), F32)
    cfpad_ref[cf_halo + SEG_ROW:, :] = jnp.zeros((cf_halo, CF_W), F32)
    qpad_ref[0:q_halo, :] = jnp.zeros((q_halo, GDN_CONV_W), F32)
    qpad_ref[q_halo + SEG_ROW:, :] = jnp.zeros((q_halo, GDN_CONV_W), F32)

    x = _load_token_major(x_ref, xstage[0], 0, SEG_ROW // BATCH) if xstage else x_ref[...]
    h = _modulate(x, sc_ref[...], sh_ref[...]).astype(BF16)
    y = _dot(h, wcf_ref[...])
    cfpad_ref[cf_halo:cf_halo + SEG_ROW, :] = y[:, :CF_W] * _sigmoid(y[:, CF_W:])
    for j in range(3):
        cols = slice(j * GDN_QK, (j + 1) * GDN_QK)
        qpad_ref[q_halo:q_halo + SEG_ROW, cols] = _dot(h, wqkv_ref[:, cols])
    yz = _dot(h, wzbg_ref[...])

    for r0 in range(0, SEG_ROW, CONV_ROWS):
        acc = cfw_ref[0:1, :] * cfpad_ref[r0:r0 + CONV_ROWS, :]
        for k in range(1, CF_CONV):
            acc = acc + cfw_ref[k:k + 1, :] * cfpad_ref[r0 + k * SUBLANES:r0 + k * SUBLANES + CONV_ROWS, :]
        a_ref[r0:r0 + CONV_ROWS, :] = _silu(_layer_norm(acc, lng_ref[...], lnb_ref[...])).astype(BF16)

    _store_batch_major(z_ref, stage_ref.at[3], yz[:, :GDN_W], 0, SEG_ROW // BATCH)
    s = yz[:, GDN_W:]
    t = s + dtb_ref[...]
    softplus = jnp.maximum(t, 0.0) + jnp.log1p(jnp.exp(-jnp.abs(t)))
    lane = lax.broadcasted_iota(jnp.int32, s.shape, 1)
    bg_ref[...] = jnp.where(lane < 2 * GDN_HEADS, _sigmoid(s), -jnp.exp(alog_ref[...]) * softplus)

    blk = 2 * CONV_ROWS
    for j in range(3):
        cols = slice(j * GDN_QK, (j + 1) * GDN_QK)
        for r0 in range(0, SEG_ROW, blk):
            yq = (gcw_ref[0:1, cols] * qpad_ref[r0:r0 + blk, cols]
                  + gcw_ref[1:2, cols] * qpad_ref[r0 + q_halo:r0 + q_halo + blk, cols]
                  + gcw_ref[2:3, cols] * qpad_ref[r0 + 2 * q_halo:r0 + 2 * q_halo + blk, cols])
            yq = _silu(yq)
            if j < 2:
                scale = GDN_DK ** -0.5 if j == 0 else 1.0
                heads = []
                for hh in range(GDN_HEADS):
                    th = yq[:, hh * GDN_DK:(hh + 1) * GDN_DK]
                    heads.append(th * (lax.rsqrt(jnp.sum(th * th, axis=-1, keepdims=True) + RMS_EPS) * scale))
                yq = jnp.concatenate(heads, axis=-1)
            _store_batch_major(qkv_ref.at[:, :, cols], stage_ref.at[j], yq, r0 // SUBLANES, blk // SUBLANES, stage_row0=r0)


def _even_in(x, mod, w_cf, w_qkv, w_zbg, cf_w, ln_g, ln_b, gdn_w, alog, dtb):
    batch_major = x.ndim == 3
    rows = ROWS_LAT
    n_tok = SEG_ROW // BATCH
    x_spec = (pl.BlockSpec((BATCH, n_tok, D_MODEL), lambda i: (0, i, 0)) if batch_major
              else pl.BlockSpec((SEG_ROW, D_MODEL), lambda i: (i, 0)))
    x_stage = [pltpu.VMEM((D_MODEL // LANES, SEG_ROW, LANES), F32)] if batch_major else []
    return pl.pallas_call(
        _even_in_kernel,
        grid=(rows // SEG_ROW,),
        in_specs=[x_spec, mod.spec(SC1), mod.spec(SH1),
                  _const_spec((D_MODEL, 2 * CF_W)), _const_spec((D_MODEL, GDN_CONV_W)), _const_spec((D_MODEL, GDN_W + LANES)),
                  _const_spec((CF_CONV, CF_W)), _const_spec((1, CF_W)), _const_spec((1, CF_W)),
                  _const_spec((3, GDN_CONV_W)), _const_spec((1, LANES)), _const_spec((1, LANES))],
        out_specs=[pl.BlockSpec((SEG_ROW, CF_W), lambda i: (i, 0)),
                   pl.BlockSpec((BATCH, n_tok, GDN_CONV_W), lambda i: (0, i, 0)),
                   pl.BlockSpec((BATCH, n_tok, GDN_W), lambda i: (0, i, 0)),
                   pl.BlockSpec((SEG_ROW, LANES), lambda i: (i, 0))],
        out_shape=[jax.ShapeDtypeStruct((rows, CF_W), BF16),
                   jax.ShapeDtypeStruct((BATCH, rows // BATCH, GDN_CONV_W), F32),
                   jax.ShapeDtypeStruct((BATCH, rows // BATCH, GDN_W), F32),
                   jax.ShapeDtypeStruct((rows, LANES), F32)],
        scratch_shapes=[pltpu.VMEM((SEG_ROW + 2 * (CF_CONV // 2) * SUBLANES, CF_W), F32),
                        pltpu.VMEM((SEG_ROW + 2 * SUBLANES, GDN_CONV_W), F32),
                        pltpu.VMEM((4, GDN_QK // LANES, SEG_ROW, LANES), F32)] + x_stage,
        compiler_params=_params(1), name="even_in",
    )(x, mod.arr, mod.arr, w_cf, w_qkv, w_zbg, cf_w, ln_g, ln_b, gdn_w, alog, dtb)


def _zbg_proj(x, mod, w, alog, dtb, *, tm):
    rows = x.shape[0]
    return pl.pallas_call(
        _zbg_kernel,
        grid=(rows // tm,),
        in_specs=[pl.BlockSpec((tm, D_MODEL), lambda i: (i, 0)), mod.spec(SC1), mod.spec(SH1),
                  _const_spec((D_MODEL, GDN_W + LANES)), _const_spec((1, LANES)), _const_spec((1, LANES))],
        out_specs=[pl.BlockSpec((BATCH, tm // BATCH, GDN_W), lambda i: (0, i, 0)),
                   pl.BlockSpec((tm, LANES), lambda i: (i, 0))],
        out_shape=[jax.ShapeDtypeStruct((BATCH, rows // BATCH, GDN_W), F32), jax.ShapeDtypeStruct((rows, LANES), F32)],
        scratch_shapes=[pltpu.VMEM((GDN_W // LANES, PIECE, LANES), F32)],
        compiler_params=_params(1), name="zbg_proj",
    )(x, mod.arr, mod.arr, w, alog, dtb)


def _gdn_decays(beta_row, g_row, rev):
    c = GDN_CK
    ri = lax.broadcasted_iota(jnp.int32, (c, c), 0)
    ci = lax.broadcasted_iota(jnp.int32, (c, c), 1)
    eye = ri == ci
    incl = (ri <= ci) if rev else (ri >= ci)
    incl_t = (ci <= ri) if rev else (ci >= ri)
    strict = (ri < ci) if rev else (ri > ci)
    g_b = jnp.broadcast_to(g_row, (c, c))
    beta_b = jnp.broadcast_to(beta_row, (c, c))
    g_col = jnp.sum(jnp.where(eye, g_b, 0.0), axis=1, keepdims=True)
    beta_col = jnp.sum(jnp.where(eye, beta_b, 0.0), axis=1, keepdims=True)
    gc_col = jnp.sum(jnp.where(incl, g_b, 0.0), axis=1, keepdims=True)
    gc_row = jnp.sum(jnp.where(incl_t, g_col, 0.0), axis=0, keepdims=True)
    g_sum = jnp.sum(g_row, axis=1, keepdims=True)
    decay = jnp.exp(jnp.where(incl, gc_col - gc_row, -1e30))
    return dict(strict=strict, beta_col=beta_col, decay=decay, e_col=jnp.exp(gc_col),
                e_end=jnp.exp(g_sum - gc_col), g_tot=jnp.exp(g_sum))


def _gdn_chunk_terms(chains, side_tasks=()):
    side_tasks = list(side_tasks)

    def side():
        if side_tasks:
            side_tasks.pop(0)()

    c = GDN_CK
    ri = lax.broadcasted_iota(jnp.int32, (c, c), 0)
    ci = lax.broadcasted_iota(jnp.int32, (c, c), 1)
    same = lambda s: (ri // s) == (ci // s)
    nils = [jnp.where(ch["strict"], -(ch["beta_col"] * ch["kk"] * ch["decay"]), 0.0) for ch in chains]
    nds = [jnp.where(same(GDN_BASE), n, 0.0) for n in nils]
    invs = [jnp.where(ri == ci, 1.0, 0.0) + nd for nd in nds]
    p16s = [nd.astype(BF16) for nd in nds]
    side()
    for _ in range(int(math.log2(GDN_BASE)) - 1):
        p16s = [_dot(p16, p16).astype(BF16) for p16 in p16s]
        side()
        invs = [inv + _dot(p16, inv.astype(BF16)) for p16, inv in zip(p16s, invs)]
    s = GDN_BASE
    while s < c:
        off = jnp.logical_and(same(2 * s), jnp.logical_not(same(s)))
        inv16s = [inv.astype(BF16) for inv in invs]
        side()
        m16s = [_dot(inv16, jnp.where(off, n, 0.0).astype(BF16)).astype(BF16) for inv16, n in zip(inv16s, nils)]
        side()
        invs = [inv + _dot(m16, inv16) for inv, m16, inv16 in zip(invs, m16s, inv16s)]
        s *= 2
    side()
    y16s = [_dot(inv.astype(BF16),
                 jnp.concatenate([ch["v"] * ch["beta_col"], ch["k"] * (ch["beta_col"] * ch["e_col"])], axis=-1).astype(BF16)
                 ).astype(BF16) for inv, ch in zip(invs, chains)]
    side()
    kts = [_dot_tn((ch["k"] * ch["e_end"]).astype(BF16), y16) for ch, y16 in zip(chains, y16s)]
    qys = [_dot((ch["qk_raw"] * ch["decay"]).astype(BF16), y16) for ch, y16 in zip(chains, y16s)]
    while side_tasks:
        side()
    out = []
    for ch, kt, qy in zip(chains, kts, qys):
        q_eff = ch["q"] * ch["e_col"] - qy[:, GDN_DV:]
        out.append(((-kt[:, GDN_DV:]).astype(BF16), kt[:, :GDN_DV], q_eff.astype(BF16), qy[:, :GDN_DV],
                    jnp.broadcast_to(ch["g_tot"], (1, GDN_DV))))
    return out


def _gdn_kernel(qc, kc, vc, zc, ql, kl, vl, zl, bg, ng_ref, oc_ref, ol_ref,
                q_all, k_all, v_all, o_dir, a_ref, b_ref, qe_ref, oz_ref, gt_ref, s_ref):
    ncc = CTX_LEN // GDN_CK
    for src, dst in ((qc, q_all), (kc, k_all), (vc, v_all)):
        dst[0:CTX_LEN, :] = src[...]
    for src, dst in ((ql, q_all), (kl, k_all), (vl, v_all)):
        for r in range(SEQ // PIECE):
            dst[CTX_LEN + r * PIECE:CTX_LEN + (r + 1) * PIECE, :] = src[r * PIECE:(r + 1) * PIECE, :]

    def chunk_at(d, p):
        if d == 0:
            return p
        return jnp.where(p < ncc, ncc - 1 - p, GDN_NCHUNK + ncc - 1 - p)

    def rows_of(cc):
        if isinstance(cc, int):
            return pl.ds(cc * GDN_CK, GDN_CK)
        return pl.ds(pl.multiple_of(cc * GDN_CK, GDN_CK), GDN_CK)

    def prepare(g, side_tasks=()):
        chains = []
        for jj in range(GDN_GROUP):
            for d in range(2):
                cc = chunk_at(d, g * GDN_GROUP + jj)
                rows = rows_of(cc)
                qv, kv, vv = q_all[rows, :], k_all[rows, :], v_all[rows, :]
                k16 = kv.astype(BF16)
                kq = _dot_nt(jnp.concatenate([k16, qv.astype(BF16)], axis=0), k16)
                chains.append(dict(q=qv, k=kv, v=vv, kk=kq[:GDN_CK], qk_raw=kq[GDN_CK:],
                                   **_gdn_decays(bg[d, pl.ds(cc, 1), :], bg[2 + d, pl.ds(cc, 1), :], rev=(d == 1))))
        terms = _gdn_chunk_terms(chains, side_tasks)
        for n, (a_neg, b_mat, q_eff, o_zero, g_tot) in enumerate(terms):
            jj, d = divmod(n, 2)
            a_ref[d, jj] = a_neg
            b_ref[d, jj] = b_mat
            qe_ref[d, jj] = q_eff
            oz_ref[d, jj] = o_zero
            gt_ref[d, jj] = g_tot

    def advance_tasks(g):
        def step(jj):
            for d in range(2):
                rows = rows_of(chunk_at(d, g * GDN_GROUP + jj))
                s = s_ref[d]
                s16 = s.astype(BF16)
                o_dir[d, rows, :] = oz_ref[d, jj] + _dot(qe_ref[d, jj], s16)
                s_ref[d] = s * gt_ref[d, jj] + _dot(a_ref[d, jj], s16) + b_ref[d, jj]
        return [functools.partial(step, jj) for jj in range(GDN_GROUP)]

    s_ref[...] = jnp.zeros(s_ref.shape, F32)
    n_groups = GDN_NCHUNK // GDN_GROUP
    prepare(0)

    def body(g, carry):
        prepare(g, advance_tasks(g - 1))
        return carry
    lax.fori_loop(1, n_groups, body, 0)
    for task in advance_tasks(n_groups - 1):
        task()

    blk = 256
    for z, o, tok0 in ((zc, oc_ref, 0), (zl, ol_ref, CTX_LEN)):
        for r in range(z.shape[0] // blk):
            rows = slice(r * blk, (r + 1) * blk)
            src = slice(tok0 + r * blk, tok0 + (r + 1) * blk)
            ov = o_dir[0, src, :] + o_dir[1, src, :]
            ov = ov * lax.rsqrt(jnp.mean(ov * ov, axis=-1, keepdims=True) + RMS_EPS) * ng_ref[...]
            o[rows, :] = ov * _silu(z[rows, :])


def _gdn(qkv_c, z_c, qkv_l, z_l, bg, norm_g):
    def tok_spec(n_tok, col0):
        return pl.BlockSpec((None, n_tok, GDN_DK), lambda b, h: (b, 0, col0 + h))

    in_specs = []
    for n_tok in (CTX_LEN, SEQ):
        in_specs += [tok_spec(n_tok, 0), tok_spec(n_tok, GDN_HEADS), tok_spec(n_tok, 2 * GDN_HEADS), tok_spec(n_tok, 0)]
    in_specs += [pl.BlockSpec((None, None, 4, GDN_NCHUNK, GDN_CK), lambda b, h: (b, h, 0, 0, 0)),
                 _const_spec((1, GDN_DV))]
    n_tok = CTX_LEN + SEQ
    return pl.pallas_call(
        _gdn_kernel,
        grid=(BATCH, GDN_HEADS),
        in_specs=in_specs,
        out_specs=[tok_spec(CTX_LEN, 0), tok_spec(SEQ, 0)],
        out_shape=[jax.ShapeDtypeStruct((BATCH, CTX_LEN, GDN_W), F32),
                   jax.ShapeDtypeStruct((BATCH, SEQ, GDN_W), F32)],
        scratch_shapes=[pltpu.VMEM((n_tok, GDN_DK), F32), pltpu.VMEM((n_tok, GDN_DK), F32),
                        pltpu.VMEM((n_tok, GDN_DV), F32), pltpu.VMEM((2, n_tok, GDN_DV), F32),
                        pltpu.VMEM((2, GDN_GROUP, GDN_DK, GDN_DV), BF16),
                        pltpu.VMEM((2, GDN_GROUP, GDN_DK, GDN_DV), F32),
                        pltpu.VMEM((2, GDN_GROUP, GDN_CK, GDN_DK), BF16),
                        pltpu.VMEM((2, GDN_GROUP, GDN_CK, GDN_DV), F32),
                        pltpu.VMEM((2, GDN_GROUP, 1, GDN_DV), F32),
                        pltpu.VMEM((2, GDN_DK, GDN_DV), F32)],
        compiler_params=_params(2), name="gdn",
    )(qkv_c, qkv_c, qkv_c, z_c, qkv_l, qkv_l, qkv_l, z_l, bg, norm_g)


def _residual_norm(x, branch, gate, g, b):
    return _layer_norm(DN_ALPHA * x + _per_sample(branch, gate, lambda a, c: a * c), g, b)


def _out_even_kernel(x_ref, a_ref, o_ref, g1_ref, wa_ref, wo_ref, lng_ref, lnb_ref, out_ref, stage_ref, *xstage):
    tm = out_ref.shape[0]
    for p in range(tm // PIECE):
        rows = slice(p * PIECE, (p + 1) * PIECE)
        tok0, n_tok = p * (PIECE // BATCH), PIECE // BATCH
        o = _load_token_major(o_ref, stage_ref, tok0, n_tok)
        x = _load_token_major(x_ref, xstage[0], tok0, n_tok) if xstage else x_ref[rows, :]
        m = _dot(a_ref[rows, :], wa_ref[...]) + _dot(o.astype(BF16), wo_ref[...])
        out_ref[rows, :] = _residual_norm(x, m, g1_ref[...], lng_ref[...], lnb_ref[...])


def _out_even(x, a, o, mod, wa, wo, ln_g, ln_b, *, tm):
    batch_major = x.ndim == 3
    rows = a.shape[0]
    row = lambda c: pl.BlockSpec((tm, c), lambda i: (i, 0))
    by_batch = lambda c: pl.BlockSpec((BATCH, tm // BATCH, c), lambda i: (0, i, 0))
    x_stage = [pltpu.VMEM((D_MODEL // LANES, PIECE, LANES), F32)] if batch_major else []
    return pl.pallas_call(
        _out_even_kernel,
        grid=(rows // tm,),
        in_specs=[by_batch(D_MODEL) if batch_major else row(D_MODEL), row(CF_W), by_batch(GDN_W),
                  mod.spec(G1),
                  _const_spec((CF_W, D_MODEL)), _const_spec((GDN_W, D_MODEL)),
                  _const_spec((1, D_MODEL)), _const_spec((1, D_MODEL))],
        out_specs=row(D_MODEL),
        out_shape=jax.ShapeDtypeStruct((rows, D_MODEL), F32),
        scratch_shapes=[pltpu.VMEM((GDN_W // LANES, PIECE, LANES), F32)] + x_stage,
        compiler_params=_params(1), name="out_even",
    )(x, a, o, mod.arr, wa, wo, ln_g, ln_b)


def _gelu_tanh(x):
    return 0.5 * x * (1.0 + jnp.tanh(math.sqrt(2.0 / math.pi) * (x + 0.044715 * (x * x * x))))


def _out_odd_kernel(x_ref, u_ref, yf_ref, yb_ref, s_ref, g1_ref, d_ref, gw_ref, gb_ref, wa_ref, wo_ref,
                    lng_ref, lnb_ref, out_ref):
    tm = x_ref.shape[0]
    for p in range(tm // PIECE):
        rows = slice(p * PIECE, (p + 1) * PIECE)
        y = d_ref[...] * u_ref[rows, :] + yf_ref[rows, :] + yb_ref[rows, :]
        zg = _gelu_tanh(y)
        s5 = zg * _sigmoid(_dot(zg.astype(BF16), gw_ref[...]) + gb_ref[...])
        m = _dot(s5.astype(BF16), wa_ref[...]) + _dot(s_ref[rows, :], wo_ref[...])
        out_ref[rows, :] = _residual_norm(x_ref[rows, :], m, g1_ref[...], lng_ref[...], lnb_ref[...])


def _out_odd(x, u, yf, yb, s, mod, d_skip, glu_w, glu_b, wa, wo, ln_g, ln_b, *, tm):
    rows = x.shape[0]
    row = lambda c: pl.BlockSpec((tm, c), lambda i: (i, 0))
    return pl.pallas_call(
        _out_odd_kernel,
        grid=(rows // tm,),
        in_specs=[row(D_MODEL), row(S5_W), row(S5_W), row(S5_W), row(SC_W), mod.spec(G1),
                  _const_spec((1, S5_W)), _const_spec((S5_W, S5_W)), _const_spec((1, S5_W)),
                  _const_spec((S5_W, D_MODEL)), _const_spec((SC_W, D_MODEL)),
                  _const_spec((1, D_MODEL)), _const_spec((1, D_MODEL))],
        out_specs=row(D_MODEL),
        out_shape=jax.ShapeDtypeStruct((rows, D_MODEL), F32),
        compiler_params=_params(1), name="out_odd",
    )(x, u, yf, yb, s, mod.arr, d_skip, glu_w, glu_b, wa, wo, ln_g, ln_b)


def _conv3(pad_ref, s, r0, n, cw_ref, shift):
    return (cw_ref[0:1, :] * pad_ref[s, r0:r0 + n, :]
            + cw_ref[1:2, :] * pad_ref[s, r0 + shift:r0 + shift + n, :]
            + cw_ref[2:3, :] * pad_ref[s, r0 + 2 * shift:r0 + 2 * shift + n, :])


def _ffn_kernel(x_ref, sc_ref, sh_ref, g2_ref, wu_ref, cw_ref, wd_ref, lng_ref, lnb_ref, out_ref,
                h_ref, val_ref, pad_ref, *, tm, shift, seg):
    j = pl.program_id(1)
    n_pieces = tm // PIECE
    tf = val_ref.shape[1]

    @pl.when(j == 0)
    def _():
        zeros = jnp.zeros((shift, tf), F32)
        for s in range(tm // seg):
            pad_ref[s, 0:shift, :] = zeros
            pad_ref[s, shift + seg:shift + seg + shift, :] = zeros
        for p in range(n_pieces):
            x = _load_rows(x_ref, p * PIECE, PIECE)
            h_ref[p * PIECE:(p + 1) * PIECE, :] = _modulate(x, sc_ref[...], sh_ref[...]).astype(BF16)

    for p in range(n_pieces):
        h = h_ref[p * PIECE:(p + 1) * PIECE, :]
        up = _dot(h, wu_ref[...])
        val_ref[p * PIECE:(p + 1) * PIECE, :] = up[:, :tf]
        s, off = divmod(p * PIECE, seg)
        pad_ref[s, shift + off:shift + off + PIECE, :] = up[:, tf:]

    for p in range(n_pieces):
        s, off = divmod(p * PIECE, seg)
        gate = _conv3(pad_ref, s, off, PIECE, cw_ref, shift)
        act = (val_ref[p * PIECE:(p + 1) * PIECE, :] * _silu(gate)).astype(BF16)
        part = _dot(act, wd_ref[...])

        @pl.when(j == 0)
        def _():
            _store_rows(out_ref, p * PIECE, part)

        @pl.when(j > 0)
        def _():
            _store_rows(out_ref, p * PIECE, _load_rows(out_ref, p * PIECE, PIECE) + part)

    @pl.when(j == pl.num_programs(1) - 1)
    def _():
        for p in range(n_pieces):
            x = _load_rows(x_ref, p * PIECE, PIECE)
            y = _residual_norm(x, _load_rows(out_ref, p * PIECE, PIECE), g2_ref[...], lng_ref[...], lnb_ref[...])
            _store_rows(out_ref, p * PIECE, y)


def _ffn(x, mod, w_up, cw, w_down, ln_g, ln_b, *, tiling, shift, seg, tf):
    tm = tiling.tm
    n_f = FFN_F // tf
    out = pl.pallas_call(
        functools.partial(_ffn_kernel, tm=tm, shift=shift, seg=seg),
        grid=(tiling.n_tiles, n_f),
        in_specs=[tiling.spec(D_MODEL), mod.spec(SC2), mod.spec(SH2), mod.spec(G2),
                  pl.BlockSpec((D_MODEL, 2 * tf), lambda i, j: (0, j)),
                  pl.BlockSpec((3, tf), lambda i, j: (0, j)),
                  pl.BlockSpec((tf, D_MODEL), lambda i, j: (j, 0)),
                  _const_spec((1, D_MODEL)), _const_spec((1, D_MODEL))],
        out_specs=tiling.spec(D_MODEL),
        out_shape=jax.ShapeDtypeStruct(tiling.shape(D_MODEL), F32),
        scratch_shapes=[pltpu.VMEM((tm, D_MODEL), BF16), pltpu.VMEM((tm, tf), F32),
                        pltpu.VMEM((tm // seg, seg + 2 * shift, tf), F32)],
        compiler_params=_params(2), name="conv_ffn",
    )(tiling.view(x), mod.arr, mod.arr, mod.arr, w_up, cw, w_down, ln_g, ln_b)
    return tiling.unview(out)


def _ffn_resident_kernel(x_ref, sc_ref, sh_ref, g2_ref, wu_ref, cw_ref, wd_ref, lng_ref, lnb_ref, out_ref,
                         h_ref, val_ref, pad_ref, *, tm, shift, seg, tf):
    n_pieces = tm // PIECE
    zeros = jnp.zeros((shift, tf), F32)
    for s in range(tm // seg):
        pad_ref[s, 0:shift, :] = zeros
        pad_ref[s, shift + seg:shift + seg + shift, :] = zeros
    for p in range(n_pieces):
        x = _load_rows(x_ref, p * PIECE, PIECE)
        h_ref[p * PIECE:(p + 1) * PIECE, :] = _modulate(x, sc_ref[...], sh_ref[...]).astype(BF16)

    for j in range(FFN_F // tf):
        for p in range(n_pieces):
            up = _dot(h_ref[p * PIECE:(p + 1) * PIECE, :], wu_ref[:, 2 * j * tf:2 * (j + 1) * tf])
            val_ref[p * PIECE:(p + 1) * PIECE, :] = up[:, :tf]
            s, off = divmod(p * PIECE, seg)
            pad_ref[s, shift + off:shift + off + PIECE, :] = up[:, tf:]
        for p in range(n_pieces):
            s, off = divmod(p * PIECE, seg)
            gate = _conv3(pad_ref, s, off, PIECE, cw_ref.at[:, j * tf:(j + 1) * tf], shift)
            act = (val_ref[p * PIECE:(p + 1) * PIECE, :] * _silu(gate)).astype(BF16)
            part = _dot(act, wd_ref[j * tf:(j + 1) * tf, :])
            if j > 0:
                part = _load_rows(out_ref, p * PIECE, PIECE) + part
            _store_rows(out_ref, p * PIECE, part)

    for p in range(n_pieces):
        x = _load_rows(x_ref, p * PIECE, PIECE)
        y = _residual_norm(x, _load_rows(out_ref, p * PIECE, PIECE), g2_ref[...], lng_ref[...], lnb_ref[...])
        _store_rows(out_ref, p * PIECE, y)


def _ffn_resident(x, mod, w_up, cw, w_down, ln_g, ln_b, *, tiling, shift, seg, tf):
    tm = tiling.tm
    resident = lambda shape: pl.BlockSpec(shape, lambda i: (0, 0), pipeline_mode=pl.Buffered(1))
    out = pl.pallas_call(
        functools.partial(_ffn_resident_kernel, tm=tm, shift=shift, seg=seg, tf=tf),
        grid=(tiling.n_tiles,),
        in_specs=[tiling.spec(D_MODEL), mod.spec(SC2), mod.spec(SH2), mod.spec(G2),
                  resident((D_MODEL, 2 * FFN_F)), _const_spec((3, FFN_F)), resident((FFN_F, D_MODEL)),
                  _const_spec((1, D_MODEL)), _const_spec((1, D_MODEL))],
        out_specs=tiling.spec(D_MODEL),
        out_shape=jax.ShapeDtypeStruct(tiling.shape(D_MODEL), F32),
        scratch_shapes=[pltpu.VMEM((tm, D_MODEL), BF16), pltpu.VMEM((tm, tf), F32),
                        pltpu.VMEM((tm // seg, seg + 2 * shift, tf), F32)],
        compiler_params=_params(1), name="conv_ffn_resident",
    )(tiling.view(x), mod.arr, mod.arr, mod.arr, w_up, cw, w_down, ln_g, ln_b)
    return tiling.unview(out)


def _odd_in_kernel(x_ref, sc_ref, sh_ref, wu_ref, wb_ref, wc_ref, wx_ref, cw_ref, u_ref, s_ref,
                   h_ref, b_ref, pad_ref, *, tm, shift, seg):
    j = pl.program_id(1)
    n_pieces = tm // PIECE
    cb = u_ref.shape[-1]

    @pl.when(j == 0)
    def _():
        zeros = jnp.zeros((shift, cb), F32)
        for s in range(tm // seg):
            pad_ref[s, 0:shift, :] = zeros
            pad_ref[s, shift + seg:shift + seg + shift, :] = zeros
        for p in range(n_pieces):
            x = _load_rows(x_ref, p * PIECE, PIECE)
            h_ref[p * PIECE:(p + 1) * PIECE, :] = _modulate(x, sc_ref[...], sh_ref[...]).astype(BF16)

    for p in range(n_pieces):
        h = h_ref[p * PIECE:(p + 1) * PIECE, :]
        _store_rows(u_ref, p * PIECE, _dot(h, wu_ref[...]))
        b_ref[p * PIECE:(p + 1) * PIECE, :] = _dot(h, wb_ref[...])
        s, off = divmod(p * PIECE, seg)
        pad_ref[s, shift + off:shift + off + PIECE, :] = _dot(h, wc_ref[...]) * _dot(h, wx_ref[...])

    for p in range(n_pieces):
        s, off = divmod(p * PIECE, seg)
        gated = b_ref[p * PIECE:(p + 1) * PIECE, :] * _conv3(pad_ref, s, off, PIECE, cw_ref, shift)
        _store_rows(s_ref, p * PIECE, gated.astype(BF16))


def _odd_in(x, mod, w_in, cw, *, tiling, shift, seg):
    tm = tiling.tm
    cb = 2 * LANES
    nb = S5_W // cb
    wspec = lambda k: pl.BlockSpec((D_MODEL, cb), lambda i, j: (0, k * nb + j))
    u, s = pl.pallas_call(
        functools.partial(_odd_in_kernel, tm=tm, shift=shift, seg=seg),
        grid=(tiling.n_tiles, nb),
        in_specs=[tiling.spec(D_MODEL), mod.spec(SC1), mod.spec(SH1),
                  wspec(0), wspec(1), wspec(2), wspec(3),
                  pl.BlockSpec((3, cb), lambda i, j: (0, j))],
        out_specs=[tiling.spec(cb, lambda i, j: j), tiling.spec(cb, lambda i, j: j)],
        out_shape=[jax.ShapeDtypeStruct(tiling.shape(S5_W), F32), jax.ShapeDtypeStruct(tiling.shape(SC_W), BF16)],
        scratch_shapes=[pltpu.VMEM((tm, D_MODEL), BF16), pltpu.VMEM((tm, cb), F32),
                        pltpu.VMEM((tm // seg, seg + 2 * shift, cb), F32)],
        compiler_params=_params(2), name="odd_in",
    )(tiling.view(x), mod.arr, mod.arr, w_in, w_in, w_in, w_in, cw)
    return tiling.unview(u), tiling.unview(s)


def _s5_kernel(uf_ref, ub_ref, wb_ref, wc_ref, a_ref, x0_ref, yf_ref, yb_ref, xfin_ref, bu_ref, st_ref):
    t = pl.program_id(0)
    rows = uf_ref.shape[0]
    n_tok = rows // SUBLANES
    hs = S5_HSTATE

    @pl.when(t == 0)
    def _():
        st_ref[...] = x0_ref[...]

    for d, u_ref in enumerate((uf_ref, ub_ref)):
        for hf in range(2):
            u = u_ref[:, hf * S5_HALF:(hf + 1) * S5_HALF].astype(BF16)
            bu_ref[d, :, hf * 2 * hs:(hf + 1) * 2 * hs] = _dot(u, wb_ref[d, hf])

    for d, y_ref in enumerate((yf_ref, yb_ref)):
        for hf in range(2):
            c_re = hf * 2 * hs
            c_im = c_re + hs
            a_re = jnp.broadcast_to(a_ref[d, hf, 0:1, :], (SUBLANES, hs))
            a_im = jnp.broadcast_to(a_ref[d, hf, 1:2, :], (SUBLANES, hs))
            xr, xi = st_ref[d, :, c_re:c_re + hs], st_ref[d, :, c_im:c_im + hs]
            for i in range(n_tok):
                tok = i if d == 0 else n_tok - 1 - i
                r = slice(tok * SUBLANES, (tok + 1) * SUBLANES)
                xr, xi = (a_re * xr - a_im * xi + bu_ref[d, r, c_re:c_re + hs],
                          a_re * xi + a_im * xr + bu_ref[d, r, c_im:c_im + hs])
                bu_ref[d, r, c_re:c_re + hs] = xr
                bu_ref[d, r, c_im:c_im + hs] = xi
            st_ref[d, :, c_re:c_re + hs] = xr
            st_ref[d, :, c_im:c_im + hs] = xi
        for hf in range(2):
            xs = bu_ref[d, :, hf * 2 * hs:(hf + 1) * 2 * hs].astype(BF16)
            y_ref[:, hf * S5_HALF:(hf + 1) * S5_HALF] = _dot(xs, wc_ref[d, hf])

    @pl.when(t == pl.num_programs(0) - 1)
    def _():
        xfin_ref[...] = st_ref[...]


def _s5(u, wb, wc, a, x0):
    rows = u.shape[0]
    tr = S5_TOK * SUBLANES
    nt = rows // tr
    state = (2, SUBLANES, 4 * S5_HSTATE)
    return pl.pallas_call(
        _s5_kernel,
        grid=(nt,),
        in_specs=[pl.BlockSpec((tr, S5_W), lambda t: (t, 0)),
                  pl.BlockSpec((tr, S5_W), lambda t: (nt - 1 - t, 0)),
                  _const_spec((2, 2, S5_HALF, 2 * S5_HSTATE)), _const_spec((2, 2, 2 * S5_HSTATE, S5_HALF)),
                  _const_spec((2, 2, 2, S5_HSTATE)), _const_spec(state)],
        out_specs=[pl.BlockSpec((tr, S5_W), lambda t: (t, 0)),
                   pl.BlockSpec((tr, S5_W), lambda t: (nt - 1 - t, 0)),
                   _const_spec(state)],
        out_shape=[jax.ShapeDtypeStruct((rows, S5_W), F32), jax.ShapeDtypeStruct((rows, S5_W), F32),
                   jax.ShapeDtypeStruct(state, F32)],
        scratch_shapes=[pltpu.VMEM((2, tr, 4 * S5_HSTATE), F32), pltpu.VMEM(state, F32)],
        compiler_params=_params(1), name="s5_scan",
    )(u, u, wb, wc, a, x0)


def _s5_weights(lam_re, lam_im, log_dt, b_re, b_im, c_re, c_im):
    eye = jnp.eye(S5_G // 2, dtype=F32)
    wbs, wcs, avs = [], [], []
    for di in range(2):
        lr, li = lam_re[di], lam_im[di]
        dt = jnp.exp(log_dt[di])[:, None]
        mag = jnp.exp(lr * dt)
        ar, ai = mag * jnp.cos(li * dt), mag * jnp.sin(li * dt)
        den = lr * lr + li * li
        fr = ((ar - 1.0) * lr + ai * li) / den
        fi = (ai * lr - (ar - 1.0) * li) / den
        bbr = fr[..., None] * b_re - fi[..., None] * b_im
        bbi = fr[..., None] * b_im + fi[..., None] * b_re

        def in_block(t):
            t = t.reshape(2, S5_G // 2, S5_N, S5_P)
            return jnp.einsum("hgnp,gk->hgpkn", t, eye).reshape(2, S5_HALF, S5_HSTATE)

        def out_block(t):
            t = t.reshape(2, S5_G // 2, S5_P, S5_N)
            return jnp.einsum("hgpn,gk->hgnkp", t, eye).reshape(2, S5_HSTATE, S5_HALF)

        wbs.append(jnp.concatenate([in_block(bbr), in_block(bbi)], axis=-1))
        wcs.append(jnp.concatenate([out_block(c_re), out_block(-c_im)], axis=1))
        avs.append(jnp.stack([ar.reshape(2, S5_HSTATE), ai.reshape(2, S5_HSTATE)], axis=1))
    return jnp.stack(wbs).astype(BF16), jnp.stack(wcs).astype(BF16), jnp.stack(avs)


def _bg_rows(bg, n_tok):
    t = bg.reshape(n_tok, BATCH, LANES)[:, :, :4 * GDN_HEADS].reshape(n_tok, BATCH, 4, GDN_HEADS)
    return t.transpose(1, 3, 2, 0).reshape(BATCH, GDN_HEADS, 4, n_tok // GDN_CK, GDN_CK)


def _to_token_major(a):
    return a.transpose(1, 0, 2).reshape(a.shape[0] * a.shape[1], a.shape[2])


def kernel(x, c, ctx, c_ctx, ada_w, ada_b, ln_g, ln_b, ev_w_in, ev_w_out, cf_conv, cf_ln_g, cf_ln_b, gdn_conv,
           gdn_a_log, gdn_dt_bias, gdn_norm_g, od_w_in, od_w_out, s5_lam_re, s5_lam_im, s5_log_dt, s5_b_re,
           s5_b_im, s5_c_re, s5_c_im, s5_d, s5_glu_w, s5_glu_b, sc_conv, ffn_w_up, ffn_conv, ffn_w_down):
    xl = x
    cl = _to_token_major(ctx)

    cvec = jnp.concatenate([c, jnp.broadcast_to(c_ctx[None, :], (SUBLANES, D_MODEL))], axis=0)
    mods = _ada(cvec, ada_w, ada_b).reshape(DEPTH, 2, SUBLANES, 6 * D_MODEL)

    lat_row = _Tiling("row", ROWS_LAT, 2 * SEG_ROW)
    lat_col = _Tiling("col", ROWS_LAT, 2 * SEG_ROW)
    ctx_row = _Tiling("row", ROWS_CTX, ROWS_CTX)
    lat_along_row = dict(tiling=lat_row, shift=SUBLANES, seg=SEG_ROW)
    lat_along_col = dict(tiling=lat_col, shift=lat_col.rg, seg=lat_col.tm)
    ctx_conv = dict(tiling=ctx_row, shift=SUBLANES, seg=ROWS_CTX)

    row2 = lambda v: v.reshape(1, -1)
    for i in range(DEPTH):
        j = i // 2
        last = i == DEPTH - 1
        m_lat, m_ctx = _Mod(mods, i, 0), _Mod(mods, i, 1)
        lng1, lnb1, lng2, lnb2 = row2(ln_g[i, 0]), row2(ln_b[i, 0]), row2(ln_g[i, 1]), row2(ln_b[i, 1])
        w_down, f_cw = ffn_w_down[i].astype(BF16), ffn_conv[i]

        def up_blocks(tf, w_up=ffn_w_up[i]):
            cols = [w_up[:, k * FFN_F + j * tf:k * FFN_F + (j + 1) * tf] for j in range(FFN_F // tf) for k in range(2)]
            return jnp.concatenate(cols, axis=1).astype(BF16)

        if i % 2 == 0:
            w_in = ev_w_in[j]
            w_cf = w_in[:, :2 * CF_W].astype(BF16)
            w_qkv = w_in[:, 2 * CF_W:2 * CF_W + GDN_CONV_W].astype(BF16)
            w_zbg = jnp.pad(w_in[:, 2 * CF_W + GDN_CONV_W:], ((0, 0), (0, LANES - 4 * GDN_HEADS))).astype(BF16)
            pad16 = lambda v: jnp.pad(v.reshape(1, -1), ((0, 0), (2 * GDN_HEADS, LANES - 4 * GDN_HEADS)))
            alog, dtb = pad16(gdn_a_log[j]), pad16(gdn_dt_bias[j])
            w_out = ev_w_out[j].astype(BF16)
            cf_ln = (row2(cf_ln_g[j]), row2(cf_ln_b[j]))
            a_c = _cf_mixer(cl, m_ctx, w_cf, cf_conv[j], *cf_ln, tm=ROWS_CTX, seg=ROWS_CTX)
            qkv_c = _qkv_proj(cl, m_ctx, w_qkv, gdn_conv[j], tm=ROWS_CTX, seg=ROWS_CTX)
            z_c, bg_c = _zbg_proj(cl, m_ctx, w_zbg, alog, dtb, tm=ROWS_CTX)
            a_l, qkv_l, z_l, bg_l = _even_in(xl, m_lat, w_cf, w_qkv, w_zbg, cf_conv[j], *cf_ln, gdn_conv[j], alog, dtb)
            bg = jnp.concatenate([_bg_rows(bg_c, CTX_LEN), _bg_rows(bg_l, SEQ)], axis=3)
            o_c, o_l = _gdn(qkv_c, z_c, qkv_l, z_l, bg, row2(gdn_norm_g[j]))
            x1 = _out_even(xl, a_l, o_l, m_lat, w_out[:CF_W], w_out[CF_W:], lng1, lnb1, tm=2 * SEG_ROW)
            if not last:
                c1 = _out_even(cl, a_c, o_c, m_ctx, w_out[:CF_W], w_out[CF_W:], lng1, lnb1, tm=2 * SEG_ROW)
            lat_ffn = lat_along_col
        else:
            w_in = od_w_in[j].astype(BF16)
            w_out = od_w_out[j].astype(BF16)
            wb, wc, av = _s5_weights(s5_lam_re[j], s5_lam_im[j], s5_log_dt[j], s5_b_re[j], s5_b_im[j],
                                     s5_c_re[j], s5_c_im[j])
            u_c, s_c = _odd_in(cl, m_ctx, w_in, sc_conv[j], **ctx_conv)
            u_l, s_l = _odd_in(xl, m_lat, w_in, sc_conv[j], **lat_along_col)
            zero_state = jnp.zeros((2, SUBLANES, 4 * S5_HSTATE), F32)
            yf_c, yb_c, fin_c = _s5(u_c, wb, wc, av, zero_state)
            yf_l, yb_l, _ = _s5(u_l, wb, wc, av, fin_c)
            odd_w = (row2(s5_d[j]), s5_glu_w[j].astype(BF16), row2(s5_glu_b[j]), w_out[:S5_W], w_out[S5_W:])
            x1 = _out_odd(xl, u_l, yf_l, yb_l, s_l, m_lat, *odd_w, lng1, lnb1, tm=2 * SEG_ROW)
            if not last:
                c1 = _out_odd(cl, u_c, yf_c, yb_c, s_c, m_ctx, *odd_w, lng1, lnb1, tm=2 * SEG_ROW)
            lat_ffn = lat_along_row

        xl = _ffn_resident(x1, m_lat, up_blocks(FFN_TF_LAT), f_cw, w_down, lng2, lnb2,
                           tf=FFN_TF_LAT, **lat_ffn)
        if not last:
            cl = _ffn(c1, m_ctx, up_blocks(FFN_TF_CTX), f_cw, w_down, lng2, lnb2,
                      tf=FFN_TF_CTX, **ctx_conv)

    return xl.reshape(SEQ, BATCH, D_MODEL).transpose(1, 0, 2)
```

```python
import functools
import math

import jax
import jax.numpy as jnp
from jax import lax
from jax.experimental import pallas as pl
from jax.experimental.pallas import tpu as pltpu

D_MODEL = 1024
BATCH = 8
SEQ = 2048
DEPTH = 4
GRID_W = 64
GRID_H = SEQ // GRID_W
CTX_LEN = 256
CF_W = 512
CF_CONV = 31
GDN_HEADS = 4
GDN_DK = 128
GDN_DV = 128
GDN_QK = GDN_HEADS * GDN_DK
GDN_W = GDN_HEADS * GDN_DV
GDN_CONV_W = 2 * GDN_QK + GDN_W
S5_W = 512
S5_P = 16
S5_G = S5_W // S5_P
S5_N = 64
SC_W = 512
FFN_F = 2816
DN_ALPHA = (2 * DEPTH) ** 0.25
LN_EPS = 1e-5
RMS_EPS = 1e-6

SUBLANES = 8
LANES = 128
VMEM_LIMIT_BYTES = 56 * 1024 * 1024

ROWS_LAT = SEQ * BATCH
ROWS_CTX = CTX_LEN * BATCH
SEG_ROW = GRID_W * BATCH
PIECE = 512
CONV_ROWS = 64
FFN_TF_LAT = FFN_F // 2
FFN_TF_CTX = 256
S5_HALF = S5_W // 2
S5_HSTATE = (S5_G // 2) * S5_N
S5_TOK = 64
GDN_CK = 64
GDN_BASE = 16
GDN_NCHUNK = (CTX_LEN + SEQ) // GDN_CK
GDN_GROUP = 9

F32 = jnp.float32
BF16 = jnp.bfloat16

assert BATCH == SUBLANES


def _dot(a, b):
    return jnp.dot(a, b, preferred_element_type=F32)


def _dot_nt(a, b):
    return lax.dot_general(a, b, (((1,), (1,)), ((), ())), preferred_element_type=F32)


def _dot_tn(a, b):
    return lax.dot_general(a, b, (((0,), (0,)), ((), ())), preferred_element_type=F32)


def _sigmoid(x):
    return 1.0 / (1.0 + jnp.exp(-x))


def _silu(x):
    return x * _sigmoid(x)


def _per_sample(x, vec, op):
    r, c = x.shape
    x3 = x.reshape(r // SUBLANES, SUBLANES, c)
    return op(x3, vec[None]).reshape(r, c)


def _modulate(x, scale, shift):
    y = _per_sample(x, 1.0 + scale, lambda a, b: a * b)
    return _per_sample(y, shift, lambda a, b: a + b)


def _layer_norm(x, g, b):
    mu = jnp.mean(x, axis=-1, keepdims=True)
    xc = x - mu
    var = jnp.mean(xc * xc, axis=-1, keepdims=True)
    return xc * lax.rsqrt(var + LN_EPS) * g + b


def _load_rows(ref, start, n):
    if len(ref.shape) == 2:
        return ref[start:start + n, :]
    rg = ref.shape[1]
    return ref[start // rg:(start + n) // rg, :, :].reshape(n, ref.shape[2])


def _store_rows(ref, start, value):
    n = value.shape[0]
    if len(ref.shape) == 2:
        ref[start:start + n, :] = value
    else:
        rg = ref.shape[1]
        ref[start // rg:(start + n) // rg, :, :] = value.reshape(n // rg, rg, value.shape[1])


def _params(n_axes):
    return pltpu.CompilerParams(dimension_semantics=("arbitrary",) * n_axes,
                                vmem_limit_bytes=VMEM_LIMIT_BYTES)


class _Tiling:
    def __init__(self, kind, rows, tm):
        self.kind, self.rows, self.tm = kind, rows, tm
        self.n_tiles = rows // tm
        if kind == "col":
            self.rg = tm // GRID_H

    def view(self, a):
        if self.kind == "row":
            return a
        return a.reshape(GRID_H, SEG_ROW, a.shape[-1])

    def unview(self, a):
        return a.reshape(self.rows, a.shape[-1])

    def shape(self, c):
        return (self.rows, c) if self.kind == "row" else (GRID_H, SEG_ROW, c)

    def spec(self, c, col=None):
        col = col or (lambda *ij: 0)
        if self.kind == "row":
            return pl.BlockSpec((self.tm, c), lambda *ij: (ij[0], col(*ij)))
        return pl.BlockSpec((GRID_H, self.rg, c), lambda *ij: (0, ij[0], col(*ij)))


def _const_spec(shape):
    nd = len(shape)
    return pl.BlockSpec(shape, lambda *ij: (0,) * nd)


SH1, SC1, G1, SH2, SC2, G2 = range(6)


class _Mod:
    def __init__(self, arr, layer, stream):
        self.arr, self.layer, self.stream = arr, layer, stream

    def spec(self, col):
        layer, stream = self.layer, self.stream
        return pl.BlockSpec((None, None, SUBLANES, D_MODEL), lambda *ij: (layer, stream, 0, col))


def _ada_kernel(c_ref, w_ref, b_ref, o_ref):
    s = _silu(c_ref[...]).astype(BF16)
    o_ref[...] = _dot(s, w_ref[...].astype(BF16)) + b_ref[...]


def _ada(cvec, ada_w, ada_b):
    return pl.pallas_call(
        _ada_kernel,
        grid=(DEPTH, 6),
        in_specs=[_const_spec((2 * SUBLANES, D_MODEL)),
                  pl.BlockSpec((None, D_MODEL, D_MODEL), lambda i, j: (i, 0, j)),
                  pl.BlockSpec((None, 1, D_MODEL), lambda i, j: (i, 0, j))],
        out_specs=pl.BlockSpec((None, 2 * SUBLANES, D_MODEL), lambda i, j: (i, 0, j)),
        out_shape=jax.ShapeDtypeStruct((DEPTH, 2 * SUBLANES, 6 * D_MODEL), F32),
        compiler_params=_params(2), name="ada",
    )(cvec, ada_w, ada_b.reshape(DEPTH, 1, 6 * D_MODEL))


def _cf_kernel(x_ref, sc_ref, sh_ref, w_ref, cw_ref, g_ref, b_ref, o_ref, pad_ref, *, seg):
    tm = x_ref.shape[0]
    halo = (CF_CONV // 2) * SUBLANES
    zeros = jnp.zeros((halo, CF_W), F32)
    for s in range(tm // seg):
        pad_ref[s, 0:halo, :] = zeros
        pad_ref[s, halo + seg:halo + seg + halo, :] = zeros
    def project(p):
        h = _modulate(x_ref[p * PIECE:(p + 1) * PIECE, :], sc_ref[...], sh_ref[...]).astype(BF16)
        y = _dot(h, w_ref[...])
        glu = y[:, :CF_W] * _sigmoid(y[:, CF_W:])
        s, off = divmod(p * PIECE, seg)
        pad_ref[s, halo + off:halo + off + PIECE, :] = glu

    for p in range(tm // PIECE):
        project(p)
    for s in range(tm // seg):
        def body(i, carry, s=s):
            for r0 in (pl.multiple_of(i * 2 * CONV_ROWS, CONV_ROWS), pl.multiple_of((i * 2 + 1) * CONV_ROWS, CONV_ROWS)):
                acc = cw_ref[0:1, :] * pad_ref[s, pl.ds(r0, CONV_ROWS), :]
                for k in range(1, CF_CONV):
                    acc = acc + cw_ref[k:k + 1, :] * pad_ref[s, pl.ds(r0 + k * SUBLANES, CONV_ROWS), :]
                o_ref[pl.ds(s * seg + r0, CONV_ROWS), :] = _silu(_layer_norm(acc, g_ref[...], b_ref[...])).astype(BF16)
            return carry
        lax.fori_loop(0, seg // (2 * CONV_ROWS), body, 0)


def _cf_mixer(x, mod, w, cw, ln_g, ln_b, *, tm, seg):
    rows = x.shape[0]
    halo = (CF_CONV // 2) * SUBLANES
    return pl.pallas_call(
        functools.partial(_cf_kernel, seg=seg),
        grid=(rows // tm,),
        in_specs=[pl.BlockSpec((tm, D_MODEL), lambda i: (i, 0)), mod.spec(SC1), mod.spec(SH1),
                  _const_spec((D_MODEL, 2 * CF_W)), _const_spec((CF_CONV, CF_W)),
                  _const_spec((1, CF_W)), _const_spec((1, CF_W))],
        out_specs=pl.BlockSpec((tm, CF_W), lambda i: (i, 0)),
        out_shape=jax.ShapeDtypeStruct((rows, CF_W), BF16),
        scratch_shapes=[pltpu.VMEM((tm // seg, seg + 2 * halo, CF_W), F32)],
        compiler_params=_params(1), name="cf_mixer",
    )(x, mod.arr, mod.arr, w, cw, ln_g, ln_b)


def _store_batch_major(dst_ref, stage_ref, value, tok0, n_tok, stage_row0=0):
    for c in range(stage_ref.shape[0]):
        lanes = slice(c * LANES, (c + 1) * LANES)
        stage_ref[c, stage_row0:stage_row0 + n_tok * BATCH, :] = value[:, lanes]
        for b in range(BATCH):
            dst_ref[b, tok0:tok0 + n_tok, lanes] = stage_ref[c, pl.ds(stage_row0 + b, n_tok, stride=BATCH), :]


def _load_token_major(src_ref, stage_ref, tok0, n_tok):
    cols = []
    for c in range(stage_ref.shape[0]):
        lanes = slice(c * LANES, (c + 1) * LANES)
        for b in range(BATCH):
            stage_ref[c, pl.ds(b, n_tok, stride=BATCH), :] = src_ref[b, pl.ds(tok0, n_tok), lanes]
        cols.append(stage_ref[c])
    return jnp.concatenate(cols, axis=-1)


def _qkv_kernel(x_ref, sc_ref, sh_ref, w_ref, cw_ref, o_ref, h_ref, pad_ref, stage_ref, *, seg):
    j = pl.program_id(1)
    tm = x_ref.shape[0]
    halo = SUBLANES

    @pl.when(j == 0)
    def _():
        zeros = jnp.zeros((halo, GDN_QK), F32)
        for s in range(tm // seg):
            pad_ref[s, 0:halo, :] = zeros
            pad_ref[s, halo + seg:halo + seg + halo, :] = zeros
        for p in range(tm // PIECE):
            rows = slice(p * PIECE, (p + 1) * PIECE)
            h_ref[rows, :] = _modulate(x_ref[rows, :], sc_ref[...], sh_ref[...]).astype(BF16)

    is_qk = j < 2
    scale = jnp.where(j == 0, GDN_DK ** -0.5, 1.0).astype(F32)
    n_pieces = tm // PIECE
    blk = 2 * CONV_ROWS

    def project(p):
        s, off = divmod(p * PIECE, seg)
        pad_ref[s, halo + off:halo + off + PIECE, :] = _dot(h_ref[p * PIECE:(p + 1) * PIECE, :], w_ref[...])

    def finish(p):
        s, off = divmod(p * PIECE, seg)
        for r0 in range(off, off + PIECE, blk):
            y = (cw_ref[0:1, :] * pad_ref[s, r0:r0 + blk, :]
                 + cw_ref[1:2, :] * pad_ref[s, r0 + halo:r0 + halo + blk, :]
                 + cw_ref[2:3, :] * pad_ref[s, r0 + 2 * halo:r0 + 2 * halo + blk, :])
            y = _silu(y)
            heads = []
            for hh in range(GDN_HEADS):
                t = y[:, hh * GDN_DK:(hh + 1) * GDN_DK]
                nrm = t * (lax.rsqrt(jnp.sum(t * t, axis=-1, keepdims=True) + RMS_EPS) * scale)
                heads.append(jnp.where(is_qk, nrm, t))
            row0 = s * seg + r0
            _store_batch_major(o_ref, stage_ref, jnp.concatenate(heads, axis=-1), row0 // SUBLANES, blk // SUBLANES,
                               stage_row0=row0)

    project(0)
    for p in range(n_pieces):
        if p + 1 < n_pieces:
            project(p + 1)
        finish(p)


def _qkv_proj(x, mod, w, cw, *, tm, seg):
    rows = x.shape[0]
    return pl.pallas_call(
        functools.partial(_qkv_kernel, seg=seg),
        grid=(rows // tm, 3),
        in_specs=[pl.BlockSpec((tm, D_MODEL), lambda i, j: (i, 0)), mod.spec(SC1), mod.spec(SH1),
                  pl.BlockSpec((D_MODEL, GDN_QK), lambda i, j: (0, j)),
                  pl.BlockSpec((3, GDN_QK), lambda i, j: (0, j))],
        out_specs=pl.BlockSpec((BATCH, tm // BATCH, GDN_QK), lambda i, j: (0, i, j)),
        out_shape=jax.ShapeDtypeStruct((BATCH, rows // BATCH, GDN_CONV_W), F32),
        scratch_shapes=[pltpu.VMEM((tm, D_MODEL), BF16),
                        pltpu.VMEM((tm // seg, seg + 2 * SUBLANES, GDN_QK), F32),
                        pltpu.VMEM((GDN_QK // LANES, tm, LANES), F32)],
        compiler_params=_params(2), name="qkv_proj",
    )(x, mod.arr, mod.arr, w, cw)


def _zbg_kernel(x_ref, sc_ref, sh_ref, w_ref, alog_ref, dtb_ref, z_ref, bg_ref, stage_ref):
    tm = x_ref.shape[0]
    for p in range(tm // PIECE):
        rows = slice(p * PIECE, (p + 1) * PIECE)
        h = _modulate(x_ref[rows, :], sc_ref[...], sh_ref[...]).astype(BF16)
        y = _dot(h, w_ref[...])
        _store_batch_major(z_ref, stage_ref, y[:, :GDN_W], p * (PIECE // BATCH), PIECE // BATCH)
        s = y[:, GDN_W:]
        t = s + dtb_ref[...]
        softplus = jnp.maximum(t, 0.0) + jnp.log1p(jnp.exp(-jnp.abs(t)))
        g = -jnp.exp(alog_ref[...]) * softplus
        lane = lax.broadcasted_iota(jnp.int32, s.shape, 1)
        bg_ref[rows, :] = jnp.where(lane < 2 * GDN_HEADS, _sigmoid(s), g)


def _even_in_kernel(x_ref, sc_ref, sh_ref, wcf_ref, wqkv_ref, wzbg_ref, cfw_ref, lng_ref, lnb_ref, gcw_ref,
                    alog_ref, dtb_ref, a_ref, qkv_ref, z_ref, bg_ref, cfpad_ref, qpad_ref, stage_ref, *xstage):
    cf_halo = (CF_CONV // 2) * SUBLANES
    q_halo = SUBLANES
    cfpad_ref[0:cf_halo, :] = jnp.zeros((cf_halo, CF_W), F32)
    cfpad_ref[cf_halo + SEG_ROW:, :] = jnp.zeros((cf_halo, CF_W), F32)
    qpad_ref[0:q_halo, :] = jnp.zeros((q_halo, GDN_CONV_W), F32)
    qpad_ref[q_halo + SEG_ROW:, :] = jnp.zeros((q_halo, GDN_CONV_W), F32)

    x = _load_token_major(x_ref, xstage[0], 0, SEG_ROW // BATCH) if xstage else x_ref[...]
    h = _modulate(x, sc_ref[...], sh_ref[...]).astype(BF16)
    y = _dot(h, wcf_ref[...])
    cfpad_ref[cf_halo:cf_halo + SEG_ROW, :] = y[:, :CF_W] * _sigmoid(y[:, CF_W:])
    for j in range(3):
        cols = slice(j * GDN_QK, (j + 1) * GDN_QK)
        qpad_ref[q_halo:q_halo + SEG_ROW, cols] = _dot(h, wqkv_ref[:, cols])
    yz = _dot(h, wzbg_ref[...])

    for r0 in range(0, SEG_ROW, CONV_ROWS):
        acc = cfw_ref[0:1, :] * cfpad_ref[r0:r0 + CONV_ROWS, :]
        for k in range(1, CF_CONV):
            acc = acc + cfw_ref[k:k + 1, :] * cfpad_ref[r0 + k * SUBLANES:r0 + k * SUBLANES + CONV_ROWS, :]
        a_ref[r0:r0 + CONV_ROWS, :] = _silu(_layer_norm(acc, lng_ref[...], lnb_ref[...])).astype(BF16)

    _store_batch_major(z_ref, stage_ref.at[3], yz[:, :GDN_W], 0, SEG_ROW // BATCH)
    s = yz[:, GDN_W:]
    t = s + dtb_ref[...]
    softplus = jnp.maximum(t, 0.0) + jnp.log1p(jnp.exp(-jnp.abs(t)))
    lane = lax.broadcasted_iota(jnp.int32, s.shape, 1)
    bg_ref[...] = jnp.where(lane < 2 * GDN_HEADS, _sigmoid(s), -jnp.exp(alog_ref[...]) * softplus)

    blk = 2 * CONV_ROWS
    for j in range(3):
        cols = slice(j * GDN_QK, (j + 1) * GDN_QK)
        for r0 in range(0, SEG_ROW, blk):
            yq = (gcw_ref[0:1, cols] * qpad_ref[r0:r0 + blk, cols]
                  + gcw_ref[1:2, cols] * qpad_ref[r0 + q_halo:r0 + q_halo + blk, cols]
                  + gcw_ref[2:3, cols] * qpad_ref[r0 + 2 * q_halo:r0 + 2 * q_halo + blk, cols])
            yq = _silu(yq)
            if j < 2:
                scale = GDN_DK ** -0.5 if j == 0 else 1.0
                heads = []
                for hh in range(GDN_HEADS):
                    th = yq[:, hh * GDN_DK:(hh + 1) * GDN_DK]
                    heads.append(th * (lax.rsqrt(jnp.sum(th * th, axis=-1, keepdims=True) + RMS_EPS) * scale))
                yq = jnp.concatenate(heads, axis=-1)
            _store_batch_major(qkv_ref.at[:, :, cols], stage_ref.at[j], yq, r0 // SUBLANES, blk // SUBLANES, stage_row0=r0)


def _even_in(x, mod, w_cf, w_qkv, w_zbg, cf_w, ln_g, ln_b, gdn_w, alog, dtb):
    batch_major = x.ndim == 3
    rows = ROWS_LAT
    n_tok = SEG_ROW // BATCH
    x_spec = (pl.BlockSpec((BATCH, n_tok, D_MODEL), lambda i: (0, i, 0)) if batch_major
              else pl.BlockSpec((SEG_ROW, D_MODEL), lambda i: (i, 0)))
    x_stage = [pltpu.VMEM((D_MODEL // LANES, SEG_ROW, LANES), F32)] if batch_major else []
    return pl.pallas_call(
        _even_in_kernel,
        grid=(rows // SEG_ROW,),
        in_specs=[x_spec, mod.spec(SC1), mod.spec(SH1),
                  _const_spec((D_MODEL, 2 * CF_W)), _const_spec((D_MODEL, GDN_CONV_W)), _const_spec((D_MODEL, GDN_W + LANES)),
                  _const_spec((CF_CONV, CF_W)), _const_spec((1, CF_W)), _const_spec((1, CF_W)),
                  _const_spec((3, GDN_CONV_W)), _const_spec((1, LANES)), _const_spec((1, LANES))],
        out_specs=[pl.BlockSpec((SEG_ROW, CF_W), lambda i: (i, 0)),
                   pl.BlockSpec((BATCH, n_tok, GDN_CONV_W), lambda i: (0, i, 0)),
                   pl.BlockSpec((BATCH, n_tok, GDN_W), lambda i: (0, i, 0)),
                   pl.BlockSpec((SEG_ROW, LANES), lambda i: (i, 0))],
        out_shape=[jax.ShapeDtypeStruct((rows, CF_W), BF16),
                   jax.ShapeDtypeStruct((BATCH, rows // BATCH, GDN_CONV_W), F32),
                   jax.ShapeDtypeStruct((BATCH, rows // BATCH, GDN_W), F32),
                   jax.ShapeDtypeStruct((rows, LANES), F32)],
        scratch_shapes=[pltpu.VMEM((SEG_ROW + 2 * (CF_CONV // 2) * SUBLANES, CF_W), F32),
                        pltpu.VMEM((SEG_ROW + 2 * SUBLANES, GDN_CONV_W), F32),
                        pltpu.VMEM((4, GDN_QK // LANES, SEG_ROW, LANES), F32)] + x_stage,
        compiler_params=_params(1), name="even_in",
    )(x, mod.arr, mod.arr, w_cf, w_qkv, w_zbg, cf_w, ln_g, ln_b, gdn_w, alog, dtb)


def _zbg_proj(x, mod, w, alog, dtb, *, tm):
    rows = x.shape[0]
    return pl.pallas_call(
        _zbg_kernel,
        grid=(rows // tm,),
        in_specs=[pl.BlockSpec((tm, D_MODEL), lambda i: (i, 0)), mod.spec(SC1), mod.spec(SH1),
                  _const_spec((D_MODEL, GDN_W + LANES)), _const_spec((1, LANES)), _const_spec((1, LANES))],
        out_specs=[pl.BlockSpec((BATCH, tm // BATCH, GDN_W), lambda i: (0, i, 0)),
                   pl.BlockSpec((tm, LANES), lambda i: (i, 0))],
        out_shape=[jax.ShapeDtypeStruct((BATCH, rows // BATCH, GDN_W), F32), jax.ShapeDtypeStruct((rows, LANES), F32)],
        scratch_shapes=[pltpu.VMEM((GDN_W // LANES, PIECE, LANES), F32)],
        compiler_params=_params(1), name="zbg_proj",
    )(x, mod.arr, mod.arr, w, alog, dtb)


def _gdn_decays(beta_row, g_row, rev):
    c = GDN_CK
    ri = lax.broadcasted_iota(jnp.int32, (c, c), 0)
    ci = lax.broadcasted_iota(jnp.int32, (c, c), 1)
    eye = ri == ci
    incl = (ri <= ci) if rev else (ri >= ci)
    incl_t = (ci <= ri) if rev else (ci >= ri)
    strict = (ri < ci) if rev else (ri > ci)
    g_b = jnp.broadcast_to(g_row, (c, c))
    beta_b = jnp.broadcast_to(beta_row, (c, c))
    g_col = jnp.sum(jnp.where(eye, g_b, 0.0), axis=1, keepdims=True)
    beta_col = jnp.sum(jnp.where(eye, beta_b, 0.0), axis=1, keepdims=True)
    gc_col = jnp.sum(jnp.where(incl, g_b, 0.0), axis=1, keepdims=True)
    gc_row = jnp.sum(jnp.where(incl_t, g_col, 0.0), axis=0, keepdims=True)
    g_sum = jnp.sum(g_row, axis=1, keepdims=True)
    decay = jnp.exp(jnp.where(incl, gc_col - gc_row, -1e30))
    return dict(strict=strict, beta_col=beta_col, decay=decay, e_col=jnp.exp(gc_col),
                e_end=jnp.exp(g_sum - gc_col), g_tot=jnp.exp(g_sum))


def _gdn_chunk_terms(chains, side_tasks=()):
    side_tasks = list(side_tasks)

    def side():
        if side_tasks:
            side_tasks.pop(0)()

    c = GDN_CK
    ri = lax.broadcasted_iota(jnp.int32, (c, c), 0)
    ci = lax.broadcasted_iota(jnp.int32, (c, c), 1)
    same = lambda s: (ri // s) == (ci // s)
    nils = [jnp.where(ch["strict"], -(ch["beta_col"] * ch["kk"] * ch["decay"]), 0.0) for ch in chains]
    nds = [jnp.where(same(GDN_BASE), n, 0.0) for n in nils]
    invs = [jnp.where(ri == ci, 1.0, 0.0) + nd for nd in nds]
    p16s = [nd.astype(BF16) for nd in nds]
    side()
    for _ in range(int(math.log2(GDN_BASE)) - 1):
        p16s = [_dot(p16, p16).astype(BF16) for p16 in p16s]
        side()
        invs = [inv + _dot(p16, inv.astype(BF16)) for p16, inv in zip(p16s, invs)]
    s = GDN_BASE
    while s < c:
        off = jnp.logical_and(same(2 * s), jnp.logical_not(same(s)))
        inv16s = [inv.astype(BF16) for inv in invs]
        side()
        m16s = [_dot(inv16, jnp.where(off, n, 0.0).astype(BF16)).astype(BF16) for inv16, n in zip(inv16s, nils)]
        side()
        invs = [inv + _dot(m16, inv16) for inv, m16, inv16 in zip(invs, m16s, inv16s)]
        s *= 2
    side()
    y16s = [_dot(inv.astype(BF16),
                 jnp.concatenate([ch["v"] * ch["beta_col"], ch["k"] * (ch["beta_col"] * ch["e_col"])], axis=-1).astype(BF16)
                 ).astype(BF16) for inv, ch in zip(invs, chains)]
    side()
    kts = [_dot_tn((ch["k"] * ch["e_end"]).astype(BF16), y16) for ch, y16 in zip(chains, y16s)]
    qys = [_dot((ch["qk_raw"] * ch["decay"]).astype(BF16), y16) for ch, y16 in zip(chains, y16s)]
    while side_tasks:
        side()
    out = []
    for ch, kt, qy in zip(chains, kts, qys):
        q_eff = ch["q"] * ch["e_col"] - qy[:, GDN_DV:]
        out.append(((-kt[:, GDN_DV:]).astype(BF16), kt[:, :GDN_DV], q_eff.astype(BF16), qy[:, :GDN_DV],
                    jnp.broadcast_to(ch["g_tot"], (1, GDN_DV))))
    return out


def _gdn_kernel(qc, kc, vc, zc, ql, kl, vl, zl, bg, ng_ref, oc_ref, ol_ref,
                q_all, k_all, v_all, o_dir, a_ref, b_ref, qe_ref, oz_ref, gt_ref, s_ref):
    ncc = CTX_LEN // GDN_CK
    for src, dst in ((qc, q_all), (kc, k_all), (vc, v_all)):
        dst[0:CTX_LEN, :] = src[...]
    for src, dst in ((ql, q_all), (kl, k_all), (vl, v_all)):
        for r in range(SEQ // PIECE):
            dst[CTX_LEN + r * PIECE:CTX_LEN + (r + 1) * PIECE, :] = src[r * PIECE:(r + 1) * PIECE, :]

    def chunk_at(d, p):
        if d == 0:
            return p
        return jnp.where(p < ncc, ncc - 1 - p, GDN_NCHUNK + ncc - 1 - p)

    def rows_of(cc):
        if isinstance(cc, int):
            return pl.ds(cc * GDN_CK, GDN_CK)
        return pl.ds(pl.multiple_of(cc * GDN_CK, GDN_CK), GDN_CK)

    def prepare(g, side_tasks=()):
        chains = []
        for jj in range(GDN_GROUP):
            for d in range(2):
                cc = chunk_at(d, g * GDN_GROUP + jj)
                rows = rows_of(cc)
                qv, kv, vv = q_all[rows, :], k_all[rows, :], v_all[rows, :]
                k16 = kv.astype(BF16)
                kq = _dot_nt(jnp.concatenate([k16, qv.astype(BF16)], axis=0), k16)
                chains.append(dict(q=qv, k=kv, v=vv, kk=kq[:GDN_CK], qk_raw=kq[GDN_CK:],
                                   **_gdn_decays(bg[d, pl.ds(cc, 1), :], bg[2 + d, pl.ds(cc, 1), :], rev=(d == 1))))
        terms = _gdn_chunk_terms(chains, side_tasks)
        for n, (a_neg, b_mat, q_eff, o_zero, g_tot) in enumerate(terms):
            jj, d = divmod(n, 2)
            a_ref[d, jj] = a_neg
            b_ref[d, jj] = b_mat
            qe_ref[d, jj] = q_eff
            oz_ref[d, jj] = o_zero
            gt_ref[d, jj] = g_tot

    def advance_tasks(g):
        def step(jj):
            for d in range(2):
                rows = rows_of(chunk_at(d, g * GDN_GROUP + jj))
                s = s_ref[d]
                s16 = s.astype(BF16)
                o_dir[d, rows, :] = oz_ref[d, jj] + _dot(qe_ref[d, jj], s16)
                s_ref[d] = s * gt_ref[d, jj] + _dot(a_ref[d, jj], s16) + b_ref[d, jj]
        return [functools.partial(step, jj) for jj in range(GDN_GROUP)]

    s_ref[...] = jnp.zeros(s_ref.shape, F32)
    n_groups = GDN_NCHUNK // GDN_GROUP
    prepare(0)

    def body(g, carry):
        prepare(g, advance_tasks(g - 1))
        return carry
    lax.fori_loop(1, n_groups, body, 0)
    for task in advance_tasks(n_groups - 1):
        task()

    blk = 256
    for z, o, tok0 in ((zc, oc_ref, 0), (zl, ol_ref, CTX_LEN)):
        for r in range(z.shape[0] // blk):
            rows = slice(r * blk, (r + 1) * blk)
            src = slice(tok0 + r * blk, tok0 + (r + 1) * blk)
            ov = o_dir[0, src, :] + o_dir[1, src, :]
            ov = ov * lax.rsqrt(jnp.mean(ov * ov, axis=-1, keepdims=True) + RMS_EPS) * ng_ref[...]
            o[rows, :] = ov * _silu(z[rows, :])


def _gdn(qkv_c, z_c, qkv_l, z_l, bg, norm_g):
    def tok_spec(n_tok, col0):
        return pl.BlockSpec((None, n_tok, GDN_DK), lambda b, h: (b, 0, col0 + h))

    in_specs = []
    for n_tok in (CTX_LEN, SEQ):
        in_specs += [tok_spec(n_tok, 0), tok_spec(n_tok, GDN_HEADS), tok_spec(n_tok, 2 * GDN_HEADS), tok_spec(n_tok, 0)]
    in_specs += [pl.BlockSpec((None, None, 4, GDN_NCHUNK, GDN_CK), lambda b, h: (b, h, 0, 0, 0)),
                 _const_spec((1, GDN_DV))]
    n_tok = CTX_LEN + SEQ
    return pl.pallas_call(
        _gdn_kernel,
        grid=(BATCH, GDN_HEADS),
        in_specs=in_specs,
        out_specs=[tok_spec(CTX_LEN, 0), tok_spec(SEQ, 0)],
        out_shape=[jax.ShapeDtypeStruct((BATCH, CTX_LEN, GDN_W), F32),
                   jax.ShapeDtypeStruct((BATCH, SEQ, GDN_W), F32)],
        scratch_shapes=[pltpu.VMEM((n_tok, GDN_DK), F32), pltpu.VMEM((n_tok, GDN_DK), F32),
                        pltpu.VMEM((n_tok, GDN_DV), F32), pltpu.VMEM((2, n_tok, GDN_DV), F32),
                        pltpu.VMEM((2, GDN_GROUP, GDN_DK, GDN_DV), BF16),
                        pltpu.VMEM((2, GDN_GROUP, GDN_DK, GDN_DV), F32),
                        pltpu.VMEM((2, GDN_GROUP, GDN_CK, GDN_DK), BF16),
                        pltpu.VMEM((2, GDN_GROUP, GDN_CK, GDN_DV), F32),
                        pltpu.VMEM((2, GDN_GROUP, 1, GDN_DV), F32),
                        pltpu.VMEM((2, GDN_DK, GDN_DV), F32)],
        compiler_params=_params(2), name="gdn",
    )(qkv_c, qkv_c, qkv_c, z_c, qkv_l, qkv_l, qkv_l, z_l, bg, norm_g)


def _residual_norm(x, branch, gate, g, b):
    return _layer_norm(DN_ALPHA * x + _per_sample(branch, gate, lambda a, c: a * c), g, b)


def _out_even_kernel(x_ref, a_ref, o_ref, g1_ref, wa_ref, wo_ref, lng_ref, lnb_ref, out_ref, stage_ref, *xstage):
    tm = out_ref.shape[0]
    for p in range(tm // PIECE):
        rows = slice(p * PIECE, (p + 1) * PIECE)
        tok0, n_tok = p * (PIECE // BATCH), PIECE // BATCH
        o = _load_token_major(o_ref, stage_ref, tok0, n_tok)
        x = _load_token_major(x_ref, xstage[0], tok0, n_tok) if xstage else x_ref[rows, :]
        m = _dot(a_ref[rows, :], wa_ref[...]) + _dot(o.astype(BF16), wo_ref[...])
        out_ref[rows, :] = _residual_norm(x, m, g1_ref[...], lng_ref[...], lnb_ref[...])


def _out_even(x, a, o, mod, wa, wo, ln_g, ln_b, *, tm):
    batch_major = x.ndim == 3
    rows = a.shape[0]
    row = lambda c: pl.BlockSpec((tm, c), lambda i: (i, 0))
    by_batch = lambda c: pl.BlockSpec((BATCH, tm // BATCH, c), lambda i: (0, i, 0))
    x_stage = [pltpu.VMEM((D_MODEL // LANES, PIECE, LANES), F32)] if batch_major else []
    return pl.pallas_call(
        _out_even_kernel,
        grid=(rows // tm,),
        in_specs=[by_batch(D_MODEL) if batch_major else row(D_MODEL), row(CF_W), by_batch(GDN_W),
                  mod.spec(G1),
                  _const_spec((CF_W, D_MODEL)), _const_spec((GDN_W, D_MODEL)),
                  _const_spec((1, D_MODEL)), _const_spec((1, D_MODEL))],
        out_specs=row(D_MODEL),
        out_shape=jax.ShapeDtypeStruct((rows, D_MODEL), F32),
        scratch_shapes=[pltpu.VMEM((GDN_W // LANES, PIECE, LANES), F32)] + x_stage,
        compiler_params=_params(1), name="out_even",
    )(x, a, o, mod.arr, wa, wo, ln_g, ln_b)


def _gelu_tanh(x):
    return 0.5 * x * (1.0 + jnp.tanh(math.sqrt(2.0 / math.pi) * (x + 0.044715 * (x * x * x))))


def _out_odd_kernel(x_ref, u_ref, yf_ref, yb_ref, s_ref, g1_ref, d_ref, gw_ref, gb_ref, wa_ref, wo_ref,
                    lng_ref, lnb_ref, out_ref):
    tm = x_ref.shape[0]
    for p in range(tm // PIECE):
        rows = slice(p * PIECE, (p + 1) * PIECE)
        y = d_ref[...] * u_ref[rows, :] + yf_ref[rows, :] + yb_ref[rows, :]
        zg = _gelu_tanh(y)
        s5 = zg * _sigmoid(_dot(zg.astype(BF16), gw_ref[...]) + gb_ref[...])
        m = _dot(s5.astype(BF16), wa_ref[...]) + _dot(s_ref[rows, :], wo_ref[...])
        out_ref[rows, :] = _residual_norm(x_ref[rows, :], m, g1_ref[...], lng_ref[...], lnb_ref[...])


def _out_odd(x, u, yf, yb, s, mod, d_skip, glu_w, glu_b, wa, wo, ln_g, ln_b, *, tm):
    rows = x.shape[0]
    row = lambda c: pl.BlockSpec((tm, c), lambda i: (i, 0))
    return pl.pallas_call(
        _out_odd_kernel,
        grid=(rows // tm,),
        in_specs=[row(D_MODEL), row(S5_W), row(S5_W), row(S5_W), row(SC_W), mod.spec(G1),
                  _const_spec((1, S5_W)), _const_spec((S5_W, S5_W)), _const_spec((1, S5_W)),
                  _const_spec((S5_W, D_MODEL)), _const_spec((SC_W, D_MODEL)),
                  _const_spec((1, D_MODEL)), _const_spec((1, D_MODEL))],
        out_specs=row(D_MODEL),
        out_shape=jax.ShapeDtypeStruct((rows, D_MODEL), F32),
        compiler_params=_params(1), name="out_odd",
    )(x, u, yf, yb, s, mod.arr, d_skip, glu_w, glu_b, wa, wo, ln_g, ln_b)


def _conv3(pad_ref, s, r0, n, cw_ref, shift):
    return (cw_ref[0:1, :] * pad_ref[s, r0:r0 + n, :]
            + cw_ref[1:2, :] * pad_ref[s, r0 + shift:r0 + shift + n, :]
            + cw_ref[2:3, :] * pad_ref[s, r0 + 2 * shift:r0 + 2 * shift + n, :])


def _ffn_kernel(x_ref, sc_ref, sh_ref, g2_ref, wu_ref, cw_ref, wd_ref, lng_ref, lnb_ref, out_ref,
                h_ref, val_ref, pad_ref, *, tm, shift, seg):
    j = pl.program_id(1)
    n_pieces = tm // PIECE
    tf = val_ref.shape[1]

    @pl.when(j == 0)
    def _():
        zeros = jnp.zeros((shift, tf), F32)
        for s in range(tm // seg):
            pad_ref[s, 0:shift, :] = zeros
            pad_ref[s, shift + seg:shift + seg + shift, :] = zeros
        for p in range(n_pieces):
            x = _load_rows(x_ref, p * PIECE, PIECE)
            h_ref[p * PIECE:(p + 1) * PIECE, :] = _modulate(x, sc_ref[...], sh_ref[...]).astype(BF16)

    for p in range(n_pieces):
        h = h_ref[p * PIECE:(p + 1) * PIECE, :]
        up = _dot(h, wu_ref[...])
        val_ref[p * PIECE:(p + 1) * PIECE, :] = up[:, :tf]
        s, off = divmod(p * PIECE, seg)
        pad_ref[s, shift + off:shift + off + PIECE, :] = up[:, tf:]

    for p in range(n_pieces):
        s, off = divmod(p * PIECE, seg)
        gate = _conv3(pad_ref, s, off, PIECE, cw_ref, shift)
        act = (val_ref[p * PIECE:(p + 1) * PIECE, :] * _silu(gate)).astype(BF16)
        part = _dot(act, wd_ref[...])

        @pl.when(j == 0)
        def _():
            _store_rows(out_ref, p * PIECE, part)

        @pl.when(j > 0)
        def _():
            _store_rows(out_ref, p * PIECE, _load_rows(out_ref, p * PIECE, PIECE) + part)

    @pl.when(j == pl.num_programs(1) - 1)
    def _():
        for p in range(n_pieces):
            x = _load_rows(x_ref, p * PIECE, PIECE)
            y = _residual_norm(x, _load_rows(out_ref, p * PIECE, PIECE), g2_ref[...], lng_ref[...], lnb_ref[...])
            _store_rows(out_ref, p * PIECE, y)


def _ffn(x, mod, w_up, cw, w_down, ln_g, ln_b, *, tiling, shift, seg, tf):
    tm = tiling.tm
    n_f = FFN_F // tf
    out = pl.pallas_call(
        functools.partial(_ffn_kernel, tm=tm, shift=shift, seg=seg),
        grid=(tiling.n_tiles, n_f),
        in_specs=[tiling.spec(D_MODEL), mod.spec(SC2), mod.spec(SH2), mod.spec(G2),
                  pl.BlockSpec((D_MODEL, 2 * tf), lambda i, j: (0, j)),
                  pl.BlockSpec((3, tf), lambda i, j: (0, j)),
                  pl.BlockSpec((tf, D_MODEL), lambda i, j: (j, 0)),
                  _const_spec((1, D_MODEL)), _const_spec((1, D_MODEL))],
        out_specs=tiling.spec(D_MODEL),
        out_shape=jax.ShapeDtypeStruct(tiling.shape(D_MODEL), F32),
        scratch_shapes=[pltpu.VMEM((tm, D_MODEL), BF16), pltpu.VMEM((tm, tf), F32),
                        pltpu.VMEM((tm // seg, seg + 2 * shift, tf), F32)],
        compiler_params=_params(2), name="conv_ffn",
    )(tiling.view(x), mod.arr, mod.arr, mod.arr, w_up, cw, w_down, ln_g, ln_b)
    return tiling.unview(out)


def _ffn_resident_kernel(x_ref, sc_ref, sh_ref, g2_ref, wu_ref, cw_ref, wd_ref, lng_ref, lnb_ref, out_ref,
                         h_ref, val_ref, pad_ref, *batch_major_out, tm, shift, seg, tf):
    n_pieces = tm // PIECE
    acc_ref = batch_major_out[0] if batch_major_out else out_ref
    zeros = jnp.zeros((shift, tf), F32)
    for s in range(tm // seg):
        pad_ref[s, 0:shift, :] = zeros
        pad_ref[s, shift + seg:shift + seg + shift, :] = zeros
    for p in range(n_pieces):
        x = _load_rows(x_ref, p * PIECE, PIECE)
        h_ref[p * PIECE:(p + 1) * PIECE, :] = _modulate(x, sc_ref[...], sh_ref[...]).astype(BF16)

    for j in range(FFN_F // tf):
        for p in range(n_pieces):
            up = _dot(h_ref[p * PIECE:(p + 1) * PIECE, :], wu_ref[:, 2 * j * tf:2 * (j + 1) * tf])
            val_ref[p * PIECE:(p + 1) * PIECE, :] = up[:, :tf]
            s, off = divmod(p * PIECE, seg)
            pad_ref[s, shift + off:shift + off + PIECE, :] = up[:, tf:]
        for p in range(n_pieces):
            s, off = divmod(p * PIECE, seg)
            gate = _conv3(pad_ref, s, off, PIECE, cw_ref.at[:, j * tf:(j + 1) * tf], shift)
            act = (val_ref[p * PIECE:(p + 1) * PIECE, :] * _silu(gate)).astype(BF16)
            part = _dot(act, wd_ref[j * tf:(j + 1) * tf, :])
            if j > 0:
                part = _load_rows(acc_ref, p * PIECE, PIECE) + part
            _store_rows(acc_ref, p * PIECE, part)

    for p in range(n_pieces):
        x = _load_rows(x_ref, p * PIECE, PIECE)
        y = _residual_norm(x, _load_rows(acc_ref, p * PIECE, PIECE), g2_ref[...], lng_ref[...], lnb_ref[...])
        if batch_major_out:
            _store_batch_major(out_ref, batch_major_out[1], y, p * (PIECE // BATCH), PIECE // BATCH)
        else:
            _store_rows(out_ref, p * PIECE, y)


def _ffn_resident(x, mod, w_up, cw, w_down, ln_g, ln_b, *, tiling, shift, seg, tf, batch_major_out=False):
    tm = tiling.tm
    resident = lambda shape: pl.BlockSpec(shape, lambda i: (0, 0), pipeline_mode=pl.Buffered(1))
    out_spec, out_shape, out_scratch = tiling.spec(D_MODEL), tiling.shape(D_MODEL), []
    if batch_major_out:
        assert tiling.kind == "row"
        out_spec = pl.BlockSpec((BATCH, tm // BATCH, D_MODEL), lambda i: (0, i, 0))
        out_shape = (BATCH, tiling.rows // BATCH, D_MODEL)
        out_scratch = [pltpu.VMEM((tm, D_MODEL), F32), pltpu.VMEM((D_MODEL // LANES, PIECE, LANES), F32)]
    out = pl.pallas_call(
        functools.partial(_ffn_resident_kernel, tm=tm, shift=shift, seg=seg, tf=tf),
        grid=(tiling.n_tiles,),
        in_specs=[tiling.spec(D_MODEL), mod.spec(SC2), mod.spec(SH2), mod.spec(G2),
                  resident((D_MODEL, 2 * FFN_F)), _const_spec((3, FFN_F)), resident((FFN_F, D_MODEL)),
                  _const_spec((1, D_MODEL)), _const_spec((1, D_MODEL))],
        out_specs=out_spec,
        out_shape=jax.ShapeDtypeStruct(out_shape, F32),
        scratch_shapes=[pltpu.VMEM((tm, D_MODEL), BF16), pltpu.VMEM((tm, tf), F32),
                        pltpu.VMEM((tm // seg, seg + 2 * shift, tf), F32)] + out_scratch,
        compiler_params=_params(1), name="conv_ffn_resident",
    )(tiling.view(x), mod.arr, mod.arr, mod.arr, w_up, cw, w_down, ln_g, ln_b)
    return out if batch_major_out else tiling.unview(out)


def _odd_in_kernel(x_ref, sc_ref, sh_ref, wu_ref, wb_ref, wc_ref, wx_ref, cw_ref, u_ref, s_ref,
                   h_ref, b_ref, pad_ref, *, tm, shift, seg):
    j = pl.program_id(1)
    n_pieces = tm // PIECE
    cb = u_ref.shape[-1]

    @pl.when(j == 0)
    def _():
        zeros = jnp.zeros((shift, cb), F32)
        for s in range(tm // seg):
            pad_ref[s, 0:shift, :] = zeros
            pad_ref[s, shift + seg:shift + seg + shift, :] = zeros
        for p in range(n_pieces):
            x = _load_rows(x_ref, p * PIECE, PIECE)
            h_ref[p * PIECE:(p + 1) * PIECE, :] = _modulate(x, sc_ref[...], sh_ref[...]).astype(BF16)

    for p in range(n_pieces):
        h = h_ref[p * PIECE:(p + 1) * PIECE, :]
        _store_rows(u_ref, p * PIECE, _dot(h, wu_ref[...]))
        b_ref[p * PIECE:(p + 1) * PIECE, :] = _dot(h, wb_ref[...])
        s, off = divmod(p * PIECE, seg)
        pad_ref[s, shift + off:shift + off + PIECE, :] = _dot(h, wc_ref[...]) * _dot(h, wx_ref[...])

    for p in range(n_pieces):
        s, off = divmod(p * PIECE, seg)
        gated = b_ref[p * PIECE:(p + 1) * PIECE, :] * _conv3(pad_ref, s, off, PIECE, cw_ref, shift)
        _store_rows(s_ref, p * PIECE, gated.astype(BF16))


def _odd_in(x, mod, w_in, cw, *, tiling, shift, seg):
    tm = tiling.tm
    cb = 2 * LANES
    nb = S5_W // cb
    wspec = lambda k: pl.BlockSpec((D_MODEL, cb), lambda i, j: (0, k * nb + j))
    u, s = pl.pallas_call(
        functools.partial(_odd_in_kernel, tm=tm, shift=shift, seg=seg),
        grid=(tiling.n_tiles, nb),
        in_specs=[tiling.spec(D_MODEL), mod.spec(SC1), mod.spec(SH1),
                  wspec(0), wspec(1), wspec(2), wspec(3),
                  pl.BlockSpec((3, cb), lambda i, j: (0, j))],
        out_specs=[tiling.spec(cb, lambda i, j: j), tiling.spec(cb, lambda i, j: j)],
        out_shape=[jax.ShapeDtypeStruct(tiling.shape(S5_W), F32), jax.ShapeDtypeStruct(tiling.shape(SC_W), BF16)],
        scratch_shapes=[pltpu.VMEM((tm, D_MODEL), BF16), pltpu.VMEM((tm, cb), F32),
                        pltpu.VMEM((tm // seg, seg + 2 * shift, cb), F32)],
        compiler_params=_params(2), name="odd_in",
    )(tiling.view(x), mod.arr, mod.arr, w_in, w_in, w_in, w_in, cw)
    return tiling.unview(u), tiling.unview(s)


def _s5_kernel(uf_ref, ub_ref, wb_ref, wc_ref, a_ref, x0_ref, yf_ref, yb_ref, xfin_ref, bu_ref, st_ref):
    t = pl.program_id(0)
    rows = uf_ref.shape[0]
    n_tok = rows // SUBLANES
    hs = S5_HSTATE

    @pl.when(t == 0)
    def _():
        st_ref[...] = x0_ref[...]

    for d, u_ref in enumerate((uf_ref, ub_ref)):
        for hf in range(2):
            u = u_ref[:, hf * S5_HALF:(hf + 1) * S5_HALF].astype(BF16)
            bu_ref[d, :, hf * 2 * hs:(hf + 1) * 2 * hs] = _dot(u, wb_ref[d, hf])

    for d, y_ref in enumerate((yf_ref, yb_ref)):
        for hf in range(2):
            c_re = hf * 2 * hs
            c_im = c_re + hs
            a_re = jnp.broadcast_to(a_ref[d, hf, 0:1, :], (SUBLANES, hs))
            a_im = jnp.broadcast_to(a_ref[d, hf, 1:2, :], (SUBLANES, hs))
            xr, xi = st_ref[d, :, c_re:c_re + hs], st_ref[d, :, c_im:c_im + hs]
            for i in range(n_tok):
                tok = i if d == 0 else n_tok - 1 - i
                r = slice(tok * SUBLANES, (tok + 1) * SUBLANES)
                xr, xi = (a_re * xr - a_im * xi + bu_ref[d, r, c_re:c_re + hs],
                          a_re * xi + a_im * xr + bu_ref[d, r, c_im:c_im + hs])
                bu_ref[d, r, c_re:c_re + hs] = xr
                bu_ref[d, r, c_im:c_im + hs] = xi
            st_ref[d, :, c_re:c_re + hs] = xr
            st_ref[d, :, c_im:c_im + hs] = xi
        for hf in range(2):
            xs = bu_ref[d, :, hf * 2 * hs:(hf + 1) * 2 * hs].astype(BF16)
            y_ref[:, hf * S5_HALF:(hf + 1) * S5_HALF] = _dot(xs, wc_ref[d, hf])

    @pl.when(t == pl.num_programs(0) - 1)
    def _():
        xfin_ref[...] = st_ref[...]


def _s5(u, wb, wc, a, x0):
    rows = u.shape[0]
    tr = S5_TOK * SUBLANES
    nt = rows // tr
    state = (2, SUBLANES, 4 * S5_HSTATE)
    return pl.pallas_call(
        _s5_kernel,
        grid=(nt,),
        in_specs=[pl.BlockSpec((tr, S5_W), lambda t: (t, 0)),
                  pl.BlockSpec((tr, S5_W), lambda t: (nt - 1 - t, 0)),
                  _const_spec((2, 2, S5_HALF, 2 * S5_HSTATE)), _const_spec((2, 2, 2 * S5_HSTATE, S5_HALF)),
                  _const_spec((2, 2, 2, S5_HSTATE)), _const_spec(state)],
        out_specs=[pl.BlockSpec((tr, S5_W), lambda t: (t, 0)),
                   pl.BlockSpec((tr, S5_W), lambda t: (nt - 1 - t, 0)),
                   _const_spec(state)],
        out_shape=[jax.ShapeDtypeStruct((rows, S5_W), F32), jax.ShapeDtypeStruct((rows, S5_W), F32),
                   jax.ShapeDtypeStruct(state, F32)],
        scratch_shapes=[pltpu.VMEM((2, tr, 4 * S5_HSTATE), F32), pltpu.VMEM(state, F32)],
        compiler_params=_params(1), name="s5_scan",
    )(u, u, wb, wc, a, x0)


def _s5_weights(lam_re, lam_im, log_dt, b_re, b_im, c_re, c_im):
    eye = jnp.eye(S5_G // 2, dtype=F32)
    wbs, wcs, avs = [], [], []
    for di in range(2):
        lr, li = lam_re[di], lam_im[di]
        dt = jnp.exp(log_dt[di])[:, None]
        mag = jnp.exp(lr * dt)
        ar, ai = mag * jnp.cos(li * dt), mag * jnp.sin(li * dt)
        den = lr * lr + li * li
        fr = ((ar - 1.0) * lr + ai * li) / den
        fi = (ai * lr - (ar - 1.0) * li) / den
        bbr = fr[..., None] * b_re - fi[..., None] * b_im
        bbi = fr[..., None] * b_im + fi[..., None] * b_re

        def in_block(t):
            t = t.reshape(2, S5_G // 2, S5_N, S5_P)
            return jnp.einsum("hgnp,gk->hgpkn", t, eye).reshape(2, S5_HALF, S5_HSTATE)

        def out_block(t):
            t = t.reshape(2, S5_G // 2, S5_P, S5_N)
            return jnp.einsum("hgpn,gk->hgnkp", t, eye).reshape(2, S5_HSTATE, S5_HALF)

        wbs.append(jnp.concatenate([in_block(bbr), in_block(bbi)], axis=-1))
        wcs.append(jnp.concatenate([out_block(c_re), out_block(-c_im)], axis=1))
        avs.append(jnp.stack([ar.reshape(2, S5_HSTATE), ai.reshape(2, S5_HSTATE)], axis=1))
    return jnp.stack(wbs).astype(BF16), jnp.stack(wcs).astype(BF16), jnp.stack(avs)


def _bg_rows(bg, n_tok):
    t = bg.reshape(n_tok, BATCH, LANES)[:, :, :4 * GDN_HEADS].reshape(n_tok, BATCH, 4, GDN_HEADS)
    return t.transpose(1, 3, 2, 0).reshape(BATCH, GDN_HEADS, 4, n_tok // GDN_CK, GDN_CK)


def _to_token_major(a):
    return a.transpose(1, 0, 2).reshape(a.shape[0] * a.shape[1], a.shape[2])


def kernel(x, c, ctx, c_ctx, ada_w, ada_b, ln_g, ln_b, ev_w_in, ev_w_out, cf_conv, cf_ln_g, cf_ln_b, gdn_conv,
           gdn_a_log, gdn_dt_bias, gdn_norm_g, od_w_in, od_w_out, s5_lam_re, s5_lam_im, s5_log_dt, s5_b_re,
           s5_b_im, s5_c_re, s5_c_im, s5_d, s5_glu_w, s5_glu_b, sc_conv, ffn_w_up, ffn_conv, ffn_w_down):
    xl = x
    cl = _to_token_major(ctx)

    cvec = jnp.concatenate([c, jnp.broadcast_to(c_ctx[None, :], (SUBLANES, D_MODEL))], axis=0)
    mods = _ada(cvec, ada_w, ada_b).reshape(DEPTH, 2, SUBLANES, 6 * D_MODEL)

    lat_row = _Tiling("row", ROWS_LAT, 2 * SEG_ROW)
    lat_col = _Tiling("col", ROWS_LAT, 2 * SEG_ROW)
    ctx_row = _Tiling("row", ROWS_CTX, ROWS_CTX)
    lat_along_row = dict(tiling=lat_row, shift=SUBLANES, seg=SEG_ROW)
    lat_along_col = dict(tiling=lat_col, shift=lat_col.rg, seg=lat_col.tm)
    ctx_conv = dict(tiling=ctx_row, shift=SUBLANES, seg=ROWS_CTX)

    row2 = lambda v: v.reshape(1, -1)
    for i in range(DEPTH):
        j = i // 2
        last = i == DEPTH - 1
        m_lat, m_ctx = _Mod(mods, i, 0), _Mod(mods, i, 1)
        lng1, lnb1, lng2, lnb2 = row2(ln_g[i, 0]), row2(ln_b[i, 0]), row2(ln_g[i, 1]), row2(ln_b[i, 1])
        w_down, f_cw = ffn_w_down[i].astype(BF16), ffn_conv[i]

        def up_blocks(tf, w_up=ffn_w_up[i]):
            cols = [w_up[:, k * FFN_F + j * tf:k * FFN_F + (j + 1) * tf] for j in range(FFN_F // tf) for k in range(2)]
            return jnp.concatenate(cols, axis=1).astype(BF16)

        if i % 2 == 0:
            w_in = ev_w_in[j]
            w_cf = w_in[:, :2 * CF_W].astype(BF16)
            w_qkv = w_in[:, 2 * CF_W:2 * CF_W + GDN_CONV_W].astype(BF16)
            w_zbg = jnp.pad(w_in[:, 2 * CF_W + GDN_CONV_W:], ((0, 0), (0, LANES - 4 * GDN_HEADS))).astype(BF16)
            pad16 = lambda v: jnp.pad(v.reshape(1, -1), ((0, 0), (2 * GDN_HEADS, LANES - 4 * GDN_HEADS)))
            alog, dtb = pad16(gdn_a_log[j]), pad16(gdn_dt_bias[j])
            w_out = ev_w_out[j].astype(BF16)
            cf_ln = (row2(cf_ln_g[j]), row2(cf_ln_b[j]))
            a_c = _cf_mixer(cl, m_ctx, w_cf, cf_conv[j], *cf_ln, tm=ROWS_CTX, seg=ROWS_CTX)
            qkv_c = _qkv_proj(cl, m_ctx, w_qkv, gdn_conv[j], tm=ROWS_CTX, seg=ROWS_CTX)
            z_c, bg_c = _zbg_proj(cl, m_ctx, w_zbg, alog, dtb, tm=ROWS_CTX)
            a_l, qkv_l, z_l, bg_l = _even_in(xl, m_lat, w_cf, w_qkv, w_zbg, cf_conv[j], *cf_ln, gdn_conv[j], alog, dtb)
            bg = jnp.concatenate([_bg_rows(bg_c, CTX_LEN), _bg_rows(bg_l, SEQ)], axis=3)
            o_c, o_l = _gdn(qkv_c, z_c, qkv_l, z_l, bg, row2(gdn_norm_g[j]))
            x1 = _out_even(xl, a_l, o_l, m_lat, w_out[:CF_W], w_out[CF_W:], lng1, lnb1, tm=2 * SEG_ROW)
            if not last:
                c1 = _out_even(cl, a_c, o_c, m_ctx, w_out[:CF_W], w_out[CF_W:], lng1, lnb1, tm=2 * SEG_ROW)
            lat_ffn = lat_along_col
        else:
            w_in = od_w_in[j].astype(BF16)
            w_out = od_w_out[j].astype(BF16)
            wb, wc, av = _s5_weights(s5_lam_re[j], s5_lam_im[j], s5_log_dt[j], s5_b_re[j], s5_b_im[j],
                                     s5_c_re[j], s5_c_im[j])
            u_c, s_c = _odd_in(cl, m_ctx, w_in, sc_conv[j], **ctx_conv)
            u_l, s_l = _odd_in(xl, m_lat, w_in, sc_conv[j], **lat_along_col)
            zero_state = jnp.zeros((2, SUBLANES, 4 * S5_HSTATE), F32)
            yf_c, yb_c, fin_c = _s5(u_c, wb, wc, av, zero_state)
            yf_l, yb_l, _ = _s5(u_l, wb, wc, av, fin_c)
            odd_w = (row2(s5_d[j]), s5_glu_w[j].astype(BF16), row2(s5_glu_b[j]), w_out[:S5_W], w_out[S5_W:])
            x1 = _out_odd(xl, u_l, yf_l, yb_l, s_l, m_lat, *odd_w, lng1, lnb1, tm=2 * SEG_ROW)
            if not last:
                c1 = _out_odd(cl, u_c, yf_c, yb_c, s_c, m_ctx, *odd_w, lng1, lnb1, tm=2 * SEG_ROW)
            lat_ffn = lat_along_row

        xl = _ffn_resident(x1, m_lat, up_blocks(FFN_TF_LAT), f_cw, w_down, lng2, lnb2,
                           tf=FFN_TF_LAT, batch_major_out=last, **lat_ffn)
        if not last:
            cl = _ffn(c1, m_ctx, up_blocks(FFN_TF_CTX), f_cw, w_down, lng2, lnb2,
                      tf=FFN_TF_CTX, **ctx_conv)

    return xl
```
